```python
import math
import jax, jax.numpy as jnp
from jax import lax
import numpy as np

D_MODEL = 1024
BATCH = 32
SEQ = 256
DEPTH = 2
DEC_BATCH = 8
DEC_SEQ = 4096
PAST_LEN = 256

GRID_W = 64
EPS = 1e-6
ROPE_THETA = 10000.0
N_MOD = 9

ATTN_HEADS = 8
ATTN_KV_HEADS = 2
HEAD_DIM = 64
ATTN_GROUP = ATTN_HEADS // ATTN_KV_HEADS
Q_BLOCK = 128

RET_HEADS = 8
RET_DK = 64
RET_DV = 64
RET_CHUNK = 128
RET_DECAY_EXP_FWD = 5.0
RET_DECAY_EXP_BWD = 5.5

DN_HEADS = 4
DN_DK = 128
DN_DV = 128
DN_CHUNK = 64
CONV_K = 3

D_FF = 2816
N_BRANCH = 3

ATTN_Q_W = ATTN_HEADS * HEAD_DIM
ATTN_KV_W = ATTN_KV_HEADS * HEAD_DIM
RET_K_W = RET_HEADS * RET_DK
RET_V_W = RET_HEADS * RET_DV
DN_K_W = DN_HEADS * DN_DK
DN_V_W = DN_HEADS * DN_DV
IN_SPLITS = (ATTN_Q_W, ATTN_KV_W, ATTN_KV_W,
             RET_K_W, RET_K_W, RET_V_W, RET_V_W,
             DN_K_W, DN_K_W, DN_V_W, 2 * DN_HEADS, 2 * DN_HEADS, DN_V_W,
             N_BRANCH * D_MODEL)
N_IN = sum(IN_SPLITS)

kernel_name = 'hybrid_diffusion_prefix_trunk_step'


def rmsnorm(x, w):
    xf = x.astype(jnp.float32)
    y = xf * lax.rsqrt(jnp.mean(xf * xf, axis=-1, keepdims=True) + EPS)
    return (y * w.astype(jnp.float32)).astype(x.dtype)


def l2norm(x):
    return x * lax.rsqrt(jnp.sum(x * x, axis=-1, keepdims=True) + EPS)


def modulate(x, shift, scale):
    return x * (1 + scale) + shift


def swiglu(x, w_in, w_out):
    gate, up = jnp.split(x @ w_in, 2, axis=-1)
    return (jax.nn.silu(gate) * up) @ w_out


def flip(a):
    return a[:, ::-1]


def axial_rope(rows, dim):
    n_freq = dim // 4
    inv = ROPE_THETA ** (-jnp.arange(n_freq, dtype=jnp.float32) / n_freq)
    row = jnp.repeat(jnp.arange(rows, dtype=jnp.float32), GRID_W)
    col = jnp.tile(jnp.arange(GRID_W, dtype=jnp.float32), rows)
    ang = jnp.concatenate([row[:, None] * inv, col[:, None] * inv], axis=-1)
    return jnp.cos(ang), jnp.sin(ang)


def apply_rope(x, cos, sin):
    half = x.shape[-1] // 2
    xf = x.astype(jnp.float32)
    x1, x2 = xf[..., :half], xf[..., half:]
    c, s = cos[None, :, None, :], sin[None, :, None, :]
    return jnp.concatenate([x1 * c - x2 * s, x2 * c + x1 * s], axis=-1).astype(x.dtype)


def block_attention(q, k, v):
    B, T = q.shape[0], q.shape[1]
    nb = T // Q_BLOCK
    qb = q.reshape(B, nb, Q_BLOCK, ATTN_KV_HEADS, ATTN_GROUP, HEAD_DIM).transpose(1, 0, 2, 3, 4, 5)
    kf = k.astype(jnp.float32)
    vf = v.astype(jnp.float32)
    scale = HEAD_DIM ** -0.5

    def one_block(qblk):
        s = jnp.einsum('bqkgd,bskd->bkgqs', qblk.astype(jnp.float32), kf) * scale
        p = jax.nn.softmax(s, axis=-1)
        return jnp.einsum('bkgqs,bskd->bqkgd', p, vf)

    o = lax.map(one_block, qb)
    return o.transpose(1, 0, 2, 3, 4, 5).reshape(B, T, ATTN_Q_W).astype(q.dtype)


def retention_log_decays():
    h = jnp.arange(RET_HEADS, dtype=jnp.float32)
    fwd = jnp.log1p(-jnp.exp2(-RET_DECAY_EXP_FWD - h))
    bwd = jnp.log1p(-jnp.exp2(-RET_DECAY_EXP_BWD - h))
    return fwd, bwd


def retention_scan(q, k, v, log_gamma, s0):
    B, T, H = q.shape[0], q.shape[1], q.shape[2]
    C = RET_CHUNK
    n = T // C

    def chunks(a):
        return a.reshape(B, n, C, H, a.shape[-1]).transpose(1, 0, 3, 2, 4)

    pos = jnp.arange(C, dtype=jnp.float32)
    diff = pos[:, None] - pos[None, :]
    inner = jnp.where(diff >= 0, jnp.exp(log_gamma[:, None, None] * jnp.maximum(diff, 0.0)), 0.0)
    q_decay = jnp.exp(log_gamma[:, None] * (pos + 1.0))
    k_decay = jnp.exp(log_gamma[:, None] * (C - 1.0 - pos))
    chunk_decay = jnp.exp(log_gamma * C)

    def step(S, inp):
        qc, kc, vc = inp
        att = jnp.einsum('bhqd,bhkd->bhqk', qc, kc) * inner
        o = (jnp.einsum('bhqk,bhkv->bhqv', att, vc)
             + jnp.einsum('bhqd,bhdv->bhqv', qc * q_decay[None, :, :, None], S))
        S = (S * chunk_decay[None, :, None, None]
             + jnp.einsum('bhkd,bhkv->bhdv', kc * k_decay[None, :, :, None], vc))
        return S, o

    S, o = lax.scan(step, s0, (chunks(q), chunks(k), chunks(v)))
    return o.transpose(1, 0, 3, 2, 4).reshape(B, T, H, -1), S


def gated_delta_scan(q, k, v, g, beta, s0):
    B, T, H = q.shape[0], q.shape[1], q.shape[2]
    C = DN_CHUNK
    n = T // C

    def chunks(a):
        return a.reshape((B, n, C, H) + a.shape[3:]).transpose((1, 0, 3, 2) + tuple(range(4, a.ndim + 1)))

    qc, kc, vc, gc, bc = chunks(q), chunks(k), chunks(v), chunks(g), chunks(beta)
    G = jnp.cumsum(gc, axis=-1)
    pos = jnp.arange(C)
    incl = pos[:, None] >= pos[None, :]
    strict = pos[:, None] > pos[None, :]
    L = jnp.exp(jnp.where(incl, G[..., :, None] - G[..., None, :], -jnp.inf))
    kb = kc * bc[..., None]
    A = jnp.where(strict, jnp.einsum('nbhid,nbhjd->nbhij', kb, kc) * L, 0.0)
    IA = A + jnp.eye(C, dtype=A.dtype)
    u = lax.linalg.triangular_solve(IA, vc * bc[..., None], left_side=True, lower=True, unit_diagonal=True)
    w = lax.linalg.triangular_solve(IA, kb * jnp.exp(G)[..., None], left_side=True, lower=True, unit_diagonal=True)

    def step(S, inp):
        qi, ki, ui, wi, Gi, Li = inp
        v_new = ui - jnp.einsum('bhcd,bhdv->bhcv', wi, S)
        att = jnp.einsum('bhid,bhjd->bhij', qi, ki) * Li
        o = (jnp.einsum('bhcd,bhdv->bhcv', qi * jnp.exp(Gi)[..., None], S)
             + jnp.einsum('bhij,bhjv->bhiv', att, v_new))
        g_last = Gi[..., -1:]
        S = (S * jnp.exp(g_last)[..., None]
             + jnp.einsum('bhcd,bhcv->bhdv', ki * jnp.exp(g_last - Gi)[..., None], v_new))
        return S, o

    S, o = lax.scan(step, s0, (qc, kc, u, w, G, L))
    return o.transpose(1, 0, 3, 2, 4).reshape(B, T, H, -1), S


def centred_depthwise_conv(x, w):
    K, C = w.shape
    return lax.conv_general_dilated(x, w[:, None, :].astype(x.dtype), window_strides=(1,),
                                    padding=[(K // 2, K - 1 - K // 2)],
                                    dimension_numbers=('NWC', 'WIO', 'NWC'),
                                    feature_group_count=C)


def token_mixing(h, p, rows, ctx_k, ctx_v, ret_s0, dn_s0):
    f32 = jnp.float32
    B, T = h.shape[0], h.shape[1]
    points = np.cumsum(IN_SPLITS)[:-1].tolist()
    (aq, ak, av, rq, rk, rv, rg, dq, dk, dv, da, db, dg, mg) = jnp.split(h @ p['w_in'], points, axis=-1)

    aq = rmsnorm(aq.reshape(B, T, ATTN_HEADS, HEAD_DIM), p['attn_q_norm'])
    ak = rmsnorm(ak.reshape(B, T, ATTN_KV_HEADS, HEAD_DIM), p['attn_k_norm'])
    av = av.reshape(B, T, ATTN_KV_HEADS, HEAD_DIM)
    if rows is None:
        attn_out = block_attention(aq, ak, av)
    else:
        cos, sin = axial_rope(rows, HEAD_DIM)
        keys = jnp.concatenate([apply_rope(ak, cos, sin), ctx_k.astype(ak.dtype)], axis=1)
        vals = jnp.concatenate([av, ctx_v.astype(av.dtype)], axis=1)
        attn_out = block_attention(apply_rope(aq, cos, sin), keys, vals)

    rq = rq.reshape(B, T, RET_HEADS, RET_DK)
    rk = rk.reshape(B, T, RET_HEADS, RET_DK)
    if rows is not None:
        rcos, rsin = axial_rope(rows, RET_DK)
        rq, rk = apply_rope(rq, rcos, rsin), apply_rope(rk, rcos, rsin)
    rq = rq.astype(f32) * RET_DK ** -0.5
    rk = rk.astype(f32)
    rv = rv.reshape(B, T, RET_HEADS, RET_DV).astype(f32)
    lg_f, lg_b = retention_log_decays()
    r_f, rs_f = retention_scan(rq, rk, rv, lg_f, ret_s0[:, 0].astype(f32))
    r_b, rs_b = retention_scan(flip(rq), flip(rk), flip(rv), lg_b, ret_s0[:, 1].astype(f32))
    ret = r_f + flip(r_b)
    mu = jnp.mean(ret, axis=-1, keepdims=True)
    var = jnp.mean(jnp.square(ret - mu), axis=-1, keepdims=True)
    ret = ((ret - mu) * lax.rsqrt(var + EPS)).reshape(B, T, RET_V_W) * p['ret_norm'].astype(f32)
    ret_out = (jax.nn.silu(rg.astype(f32)) * ret).astype(h.dtype)
    ret_state = jnp.stack([rs_f, rs_b], axis=1)

    qkv = jax.nn.silu(centred_depthwise_conv(jnp.concatenate([dq, dk, dv], axis=-1), p['dn_conv']))
    dq, dk, dv = jnp.split(qkv, [DN_K_W, 2 * DN_K_W], axis=-1)
    dq = l2norm(dq.reshape(B, T, DN_HEADS, DN_DK).astype(f32)) * DN_DK ** -0.5
    dk = l2norm(dk.reshape(B, T, DN_HEADS, DN_DK).astype(f32))
    dv = dv.reshape(B, T, DN_HEADS, DN_DV).astype(f32)
    da = da.reshape(B, T, 2, DN_HEADS).astype(f32)
    db = db.reshape(B, T, 2, DN_HEADS).astype(f32)
    g = -jnp.exp(p['dn_a_log'].astype(f32)) * jax.nn.softplus(da + p['dn_dt_bias'].astype(f32))
    beta = jax.nn.sigmoid(db)
    d_f, ds_f = gated_delta_scan(dq, dk, dv, g[:, :, 0], beta[:, :, 0], dn_s0[:, 0].astype(f32))
    d_b, ds_b = gated_delta_scan(flip(dq), flip(dk), flip(dv), flip(g[:, :, 1]), flip(beta[:, :, 1]),
                                 dn_s0[:, 1].astype(f32))
    dn = rmsnorm(d_f + flip(d_b), p['dn_norm']) * jax.nn.silu(dg.reshape(B, T, DN_HEADS, DN_DV).astype(f32))
    dn_out = dn.reshape(B, T, DN_V_W).astype(h.dtype)
    dn_state = jnp.stack([ds_f, ds_b], axis=1)

    mg = jax.nn.sigmoid(mg.astype(f32)).reshape(B, T, N_BRANCH, D_MODEL)
    merged = (mg[:, :, 0] * (attn_out @ p['w_br_attn']).astype(f32)
              + mg[:, :, 1] * (ret_out @ p['w_br_ret']).astype(f32)
              + mg[:, :, 2] * (dn_out @ p['w_br_dn']).astype(f32))
    out = merged.astype(h.dtype) @ p['w_out']
    return out, ak, av, ret_state, dn_state


def trunk_layer(x, cond, p, rows, ctx_k, ctx_v, ret_s0, dn_s0):
    mod = (jax.nn.silu(cond) @ p['w_mod'] + p['b_mod'])[:, None, :]
    sh1, sc1, g1, sh2, sc2, g2, sh3, sc3, g3 = jnp.split(mod, N_MOD, axis=-1)
    x = x + 0.5 * g1 * swiglu(modulate(rmsnorm(x, p['norm_ffn1']), sh1, sc1), p['ffn1_w_in'], p['ffn1_w_out'])
    mix, k, v, rs, ds = token_mixing(modulate(rmsnorm(x, p['norm_mix']), sh2, sc2), p, rows,
                                     ctx_k, ctx_v, ret_s0, dn_s0)
    x = x + g2 * mix
    x = x + 0.5 * g3 * swiglu(modulate(rmsnorm(x, p['norm_ffn2']), sh3, sc3), p['ffn2_w_in'], p['ffn2_w_out'])
    return x, k, v, rs, ds


def setup_inputs(seed: int = 0) -> dict:
    key = jax.random.key(seed)
    ks = jax.random.split(key, 40)
    f32 = jnp.float32
    D = D_MODEL

    def nrm(k, shape, s):
        return jax.random.normal(k, shape, f32) * s

    def gain(k, shape):
        return 1.0 + 0.02 * jax.random.normal(k, shape, f32)

    dt = jnp.exp(jax.random.uniform(ks[20], (DEPTH, 2, DN_HEADS), f32, math.log(1e-3), math.log(1e-1)))
    return {
        'x_prompt': nrm(ks[0], (BATCH, SEQ, D), 1.0),
        'x_sample': nrm(ks[1], (DEC_BATCH, DEC_SEQ, D), 1.0),
        'cache_k': nrm(ks[2], (DEC_BATCH, DEPTH, PAST_LEN, ATTN_KV_HEADS, HEAD_DIM), 1.0),
        'cache_v': nrm(ks[3], (DEC_BATCH, DEPTH, PAST_LEN, ATTN_KV_HEADS, HEAD_DIM), 1.0),
        'state_ret': nrm(ks[4], (DEC_BATCH, DEPTH, 2, RET_HEADS, RET_DK, RET_DV), 0.1),
        'state_delta': nrm(ks[5], (DEC_BATCH, DEPTH, 2, DN_HEADS, DN_DK, DN_DV), 0.1),
        'c': nrm(ks[6], (DEC_BATCH, D), 1.0),
        'c_ctx': nrm(ks[7], (D,), 1.0),
        'w_mod': nrm(ks[8], (DEPTH, D, N_MOD * D), 0.5 * D ** -0.5),
        'b_mod': nrm(ks[9], (DEPTH, N_MOD * D), 0.01),
        'norm_ffn1': gain(ks[10], (DEPTH, D)),
        'ffn1_w_in': nrm(ks[11], (DEPTH, D, 2 * D_FF), D ** -0.5),
        'ffn1_w_out': nrm(ks[12], (DEPTH, D_FF, D), D_FF ** -0.5),
        'norm_mix': gain(ks[13], (DEPTH, D)),
        'w_in': nrm(ks[14], (DEPTH, D, N_IN), D ** -0.5),
        'attn_q_norm': gain(ks[15], (DEPTH, HEAD_DIM)),
        'attn_k_norm': gain(ks[16], (DEPTH, HEAD_DIM)),
        'ret_norm': gain(ks[17], (DEPTH, RET_V_W)),
        'dn_conv': nrm(ks[18], (DEPTH, CONV_K, 2 * DN_K_W + DN_V_W), CONV_K ** -0.5),
        'dn_a_log': jnp.log(jax.random.uniform(ks[19], (DEPTH, 2, DN_HEADS), f32, 1.0, 16.0)),
        'dn_dt_bias': dt + jnp.log(-jnp.expm1(-dt)),
        'dn_norm': gain(ks[21], (DEPTH, DN_DV)),
        'w_br_attn': nrm(ks[22], (DEPTH, ATTN_Q_W, D), ATTN_Q_W ** -0.5),
        'w_br_ret': nrm(ks[23], (DEPTH, RET_V_W, D), RET_V_W ** -0.5),
        'w_br_dn': nrm(ks[24], (DEPTH, DN_V_W, D), DN_V_W ** -0.5),
        'w_out': nrm(ks[25], (DEPTH, D, D), D ** -0.5),
        'norm_ffn2': gain(ks[26], (DEPTH, D)),
        'ffn2_w_in': nrm(ks[27], (DEPTH, D, 2 * D_FF), D ** -0.5),
        'ffn2_w_out': nrm(ks[28], (DEPTH, D_FF, D), D_FF ** -0.5),
        'norm_final': gain(ks[29], (D,)),
    }


def reference(x_prompt, x_sample, cache_k, cache_v, state_ret, state_delta, c, c_ctx,
              w_mod, b_mod, norm_ffn1, ffn1_w_in, ffn1_w_out, norm_mix, w_in,
              attn_q_norm, attn_k_norm, ret_norm, dn_conv, dn_a_log, dn_dt_bias, dn_norm,
              w_br_attn, w_br_ret, w_br_dn, w_out, norm_ffn2, ffn2_w_in, ffn2_w_out, norm_final):
    def layer_params(l):
        return {
            'w_mod': w_mod[l], 'b_mod': b_mod[l],
            'norm_ffn1': norm_ffn1[l], 'ffn1_w_in': ffn1_w_in[l], 'ffn1_w_out': ffn1_w_out[l],
            'norm_mix': norm_mix[l], 'w_in': w_in[l],
            'attn_q_norm': attn_q_norm[l], 'attn_k_norm': attn_k_norm[l], 'ret_norm': ret_norm[l],
            'dn_conv': dn_conv[l], 'dn_a_log': dn_a_log[l], 'dn_dt_bias': dn_dt_bias[l], 'dn_norm': dn_norm[l],
            'w_br_attn': w_br_attn[l], 'w_br_ret': w_br_ret[l], 'w_br_dn': w_br_dn[l], 'w_out': w_out[l],
            'norm_ffn2': norm_ffn2[l], 'ffn2_w_in': ffn2_w_in[l], 'ffn2_w_out': ffn2_w_out[l],
        }

    bp = x_prompt.shape[0]
    cond_ctx = c_ctx[None, :]
    ret_zero = jnp.zeros((bp, 2, RET_HEADS, RET_DK, RET_DV), jnp.float32)
    dn_zero = jnp.zeros((bp, 2, DN_HEADS, DN_DK, DN_DV), jnp.float32)
    h = x_prompt
    ks, vs, rss, dss = [], [], [], []
    for l in range(DEPTH):
        h, k_l, v_l, rs_l, ds_l = trunk_layer(h, cond_ctx, layer_params(l), None, None, None, ret_zero, dn_zero)
        ks.append(k_l)
        vs.append(v_l)
        rss.append(rs_l)
        dss.append(ds_l)
    y_prompt = rmsnorm(h, norm_final)
    new_cache_k = jnp.stack(ks, axis=1)
    new_cache_v = jnp.stack(vs, axis=1)
    new_state_ret = jnp.stack(rss, axis=1)
    new_state_delta = jnp.stack(dss, axis=1)

    rows = x_sample.shape[1] // GRID_W
    hs = x_sample
    for l in range(DEPTH):
        hs, _, _, _, _ = trunk_layer(hs, c, layer_params(l), rows, cache_k[:, l], cache_v[:, l],
                                     state_ret[:, l], state_delta[:, l])
    y_sample = rmsnorm(hs, norm_final)
    return (y_prompt, y_sample, new_cache_k, new_cache_v, new_state_ret, new_state_delta)
```

```python
import functools
import math

import numpy as np
import jax
import jax.numpy as jnp
from jax import lax
from jax.experimental import pallas as pl
from jax.experimental.pallas import tpu as pltpu

F32 = jnp.float32
BF16 = jnp.bfloat16

EPS = 1e-6
ROPE_THETA = 10000.0
GRID_W = 64
N_MOD = 9

ATTN_HEADS = 8
ATTN_KV_HEADS = 2
HEAD_DIM = 64
RET_HEADS = 8
RET_DK = 64
RET_CHUNK = 128
RET_DECAY_EXP_FWD = 5.0
RET_DECAY_EXP_BWD = 5.5
DN_HEADS = 4
DN_DK = 128
DN_CHUNK = 64
N_BRANCH = 3

LANES = 128
SUBLANES = 8
VMEM_LIMIT = 56 * 1024 * 1024

ATTN_W = 768
RET_W = 2048
DN_W = 2048
AB_W = 128
MG_W = 3072
IN_W = ATTN_W + RET_W + DN_W + AB_W + MG_W


def _cparams(n_axes):
    return pltpu.CompilerParams(dimension_semantics=("parallel",) * n_axes,
                                vmem_limit_bytes=VMEM_LIMIT)


def _resident(shape):
    zeros = (0,) * len(shape)
    return pl.BlockSpec(shape, lambda *_: zeros, pipeline_mode=pl.Buffered(1))


def _dot(a, b):
    return jnp.dot(a.astype(BF16), b.astype(BF16), preferred_element_type=F32)


def _dot_nt(a, b):
    return lax.dot_general(a.astype(BF16), b.astype(BF16), (((1,), (1,)), ((), ())),
                           preferred_element_type=F32)


def _dot_tn(a, b):
    return lax.dot_general(a.astype(BF16), b.astype(BF16), (((0,), (0,)), ((), ())),
                           preferred_element_type=F32)


def _sigmoid(x):
    return 1.0 / (1.0 + jnp.exp(-x))


def _silu(x):
    return x * _sigmoid(x)


def _norm_mod(x, nw, shift, scale):
    y = x * lax.rsqrt(jnp.mean(x * x, axis=-1, keepdims=True) + EPS) * nw
    return y * (1.0 + scale) + shift


def _mod_kernel(c_ref, w_ref, b_ref, o_ref):
    o_ref[0] = _dot(_silu(c_ref[...]), w_ref[0]) + b_ref[0]


def _modulation(conds, w_mod, b_mod):
    depth, d, n = w_mod.shape
    nc = conds.shape[0]
    tn = n // N_MOD
    return pl.pallas_call(
        _mod_kernel,
        grid=(depth, n // tn),
        in_specs=[pl.BlockSpec((nc, d), lambda l, j: (0, 0)),
                  pl.BlockSpec((1, d, tn), lambda l, j: (l, 0, j)),
                  pl.BlockSpec((1, 1, tn), lambda l, j: (l, 0, j))],
        out_specs=pl.BlockSpec((1, nc, tn), lambda l, j: (l, 0, j)),
        out_shape=jax.ShapeDtypeStruct((depth, nc, n), F32),
        compiler_params=_cparams(2),
        name="modulation",
    )(conds, w_mod, b_mod.reshape(depth, 1, n))


FFN_CHUNK = 256


def _ffn_kernel(x_ref, mod_ref, nw_ref, win_ref, wout_ref, fnw_ref, o_ref, *, mod_base, dff, final):
    x = x_ref[...]
    m = mod_ref[0]
    shift, scale, gate = (m[mod_base + i:mod_base + i + 1] for i in range(3))
    h = _norm_mod(x, nw_ref[...], shift, scale).astype(BF16)
    acc = jnp.zeros(x.shape, F32)
    for c in range(dff // FFN_CHUNK):
        lo = c * FFN_CHUNK
        hg = jnp.dot(h, win_ref[:, lo:lo + FFN_CHUNK], preferred_element_type=F32)
        hu = jnp.dot(h, win_ref[:, dff + lo:dff + lo + FFN_CHUNK], preferred_element_type=F32)
        a = (_silu(hg) * hu).astype(BF16)
        acc = acc + jnp.dot(a, wout_ref[lo:lo + FFN_CHUNK, :], preferred_element_type=F32)
    y = x + 0.5 * gate * acc
    if final:
        y = y * lax.rsqrt(jnp.mean(y * y, axis=-1, keepdims=True) + EPS) * fnw_ref[...]
    o_ref[...] = y


def _ffn(x, mod, nw, w_in, w_out, fnw, *, mod_base, rows_per_cond, final, tm=512):
    rows, d = x.shape
    dff = w_out.shape[0]
    tiles_per_cond = rows_per_cond // tm
    return pl.pallas_call(
        functools.partial(_ffn_kernel, mod_base=mod_base, dff=dff, final=final),
        grid=(rows // tm,),
        in_specs=[pl.BlockSpec((tm, d), lambda i: (i, 0)),
                  pl.BlockSpec((1, N_MOD, d), lambda i: (i // tiles_per_cond, 0, 0)),
                  _resident((1, d)),
                  _resident(w_in.shape),
                  _resident(w_out.shape),
                  _resident((1, d))],
        out_specs=pl.BlockSpec((tm, d), lambda i: (i, 0)),
        out_shape=jax.ShapeDtypeStruct((rows, d), F32),
        compiler_params=_cparams(1),
        name="ffn",
    )(x, mod, nw, w_in, w_out, fnw)


def _swap_halves(x):
    n = x.shape[-1]
    lane = lax.broadcasted_iota(jnp.int32, x.shape, 1)
    first = (lane % HEAD_DIM) < (HEAD_DIM // 2)
    return jnp.where(first, pltpu.roll(x, n - HEAD_DIM // 2, 1), pltpu.roll(x, HEAD_DIM // 2, 1))


def _rope(x, cos, sin):
    reps = x.shape[-1] // LANES
    c = jnp.concatenate([cos] * reps, axis=1) if reps > 1 else cos
    s = jnp.concatenate([sin] * reps, axis=1) if reps > 1 else sin
    return x * c + _swap_halves(x) * s


def _head_rms(x, ones_bd, gain):
    sq = x * x
    hi = sq.astype(BF16)
    lo = (sq - hi.astype(F32)).astype(BF16)
    ss = (jnp.dot(hi, ones_bd, preferred_element_type=F32)
          + jnp.dot(lo, ones_bd, preferred_element_type=F32))
    return x * lax.rsqrt(ss * (1.0 / HEAD_DIM) + EPS) * gain


def _inproj_kernel(*refs, rope):
    if rope:
        (x_ref, mod_ref, nw_ref, w_ref, ones_ref, gq_ref, gk_ref, cos_ref, sin_ref,
         attn_ref, ret_ref, dn_ref, ab_ref, mg_ref) = refs
    else:
        (x_ref, mod_ref, nw_ref, w_ref, ones_ref, gq_ref, gk_ref,
         attn_ref, ret_ref, dn_ref, ab_ref, mg_ref) = refs
    m = mod_ref[0]
    h = _norm_mod(x_ref[...], nw_ref[...], m[3:4], m[4:5]).astype(BF16)

    def proj(lo, width):
        return jnp.dot(h, w_ref[:, lo:lo + width], preferred_element_type=F32)

    def rot(v):
        return _rope(v, cos_ref[...], sin_ref[...]) if rope else v

    a = proj(0, ATTN_W)
    q = _head_rms(a[:, :512], ones_ref[...], gq_ref[...])
    k = _head_rms(a[:, 512:640], ones_ref[:LANES, :LANES], gk_ref[...])
    attn_ref[:, :512] = rot(q) * (HEAD_DIM ** -0.5)
    attn_ref[:, 512:640] = rot(k)
    attn_ref[:, 640:768] = a[:, 640:768]

    r = proj(ATTN_W, RET_W)
    ret_ref[:, :512] = rot(r[:, :512]) * (RET_DK ** -0.5)
    ret_ref[:, 512:1024] = rot(r[:, 512:1024])
    ret_ref[:, 1024:] = r[:, 1024:]

    dn_ref[...] = proj(ATTN_W + RET_W, DN_W)
    ab_ref[...] = proj(ATTN_W + RET_W + DN_W, AB_W)
    mg_ref[...] = proj(ATTN_W + RET_W + DN_W + AB_W, MG_W)


def _inproj(x, mod, nw, w, ones_bd, gq, gk, rope_tabs, *, rows_per_cond, seq_len, tm=256):
    rows, d = x.shape
    tiles_per_cond = rows_per_cond // tm
    tiles_per_seq = seq_len // tm
    rope = rope_tabs is not None
    in_specs = [pl.BlockSpec((tm, d), lambda i: (i, 0)),
                pl.BlockSpec((1, N_MOD, d), lambda i: (i // tiles_per_cond, 0, 0)),
                _resident((1, d)),
                _resident(w.shape),
                _resident(ones_bd.shape),
                _resident(gq.shape),
                _resident(gk.shape)]
    args = [x, mod, nw, w, ones_bd, gq, gk]
    if rope:
        in_specs += [pl.BlockSpec((tm, LANES), lambda i: (i % tiles_per_seq, 0))] * 2
        args += list(rope_tabs)
    widths = (ATTN_W, RET_W, DN_W, AB_W, MG_W)
    return pl.pallas_call(
        functools.partial(_inproj_kernel, rope=rope),
        grid=(rows // tm,),
        in_specs=in_specs,
        out_specs=[pl.BlockSpec((tm, wd), lambda i: (i, 0)) for wd in widths],
        out_shape=[jax.ShapeDtypeStruct((rows, wd), F32) for wd in widths],
        compiler_params=_cparams(1),
        name="inproj",
    )(*args)


ATTN_GROUP = ATTN_HEADS // ATTN_KV_HEADS


def _attn_kernel(*refs, tq, kv_chunk, n_chunks, has_ctx):
    if has_ctx:
        q_ref, k_ref, v_ref, ck_ref, cv_ref, o_ref = refs
    else:
        q_ref, k_ref, v_ref, o_ref = refs
    lane = lax.broadcasted_iota(jnp.int32, (tq, LANES), 1)
    low = lane < HEAD_DIM
    rows = ATTN_GROUP * tq
    res = []
    for g in range(ATTN_KV_HEADS):
        keep = low if g == 0 else jnp.logical_not(low)
        parts = []
        for j in range(ATTN_GROUP):
            hd = g * ATTN_GROUP + j
            blk = q_ref[:, (hd // 2) * LANES:(hd // 2 + 1) * LANES]
            if hd % 2 != g:
                blk = pltpu.roll(blk, HEAD_DIM, 1)
            parts.append(jnp.where(keep, blk, 0.0))
        qs = jnp.concatenate(parts, axis=0).astype(BF16)

        def step(kc, vc, carry, qs=qs):
            m, l, acc = carry
            s = _dot_nt(qs, kc)
            m_new = jnp.maximum(m, jnp.max(s, axis=-1, keepdims=True))
            alpha = jnp.exp(m - m_new)
            p = jnp.exp(s - m_new)
            l = alpha * l + jnp.sum(p, axis=-1, keepdims=True)
            acc = alpha * acc + _dot(p, vc)
            return m_new, l, acc

        def body(c, carry, step=step):
            off = pl.multiple_of(c * kv_chunk, kv_chunk)
            return step(k_ref[pl.ds(off, kv_chunk), :], v_ref[pl.ds(off, kv_chunk), :], carry)

        carry = (jnp.full((rows, 1), -jnp.inf, F32), jnp.zeros((rows, 1), F32),
                 jnp.zeros((rows, LANES), F32))
        if n_chunks == 1:
            carry = step(k_ref[...], v_ref[...], carry)
        else:
            carry = lax.fori_loop(0, n_chunks, body, carry)
        if has_ctx:
            carry = step(ck_ref[0], cv_ref[0], carry)
        _, l, acc = carry
        res.append(acc / l)
    for b in range(ATTN_HEADS // 2):
        g = (2 * b) // ATTN_GROUP
        j0 = (2 * b) % ATTN_GROUP
        even = res[g][j0 * tq:(j0 + 1) * tq]
        odd = res[g][(j0 + 1) * tq:(j0 + 2) * tq]
        if g == 0:
            odd = pltpu.roll(odd, HEAD_DIM, 1)
        else:
            even = pltpu.roll(even, HEAD_DIM, 1)
        o_ref[:, b * LANES:(b + 1) * LANES] = jnp.where(low, even, odd)


def _attention(attn, ctx_k, ctx_v, *, batch, seq_len, tq=128):
    rows = attn.shape[0]
    kv_chunk = min(seq_len, 512)
    qt = seq_len // tq
    has_ctx = ctx_k is not None
    in_specs = [pl.BlockSpec((tq, 512), lambda b, i: (b * qt + i, 0)),
                pl.BlockSpec((seq_len, LANES), lambda b, i: (b, 4)),
                pl.BlockSpec((seq_len, LANES), lambda b, i: (b, 5))]
    args = [attn, attn, attn]
    if has_ctx:
        past = ctx_k.shape[1]
        in_specs += [pl.BlockSpec((1, past, LANES), lambda b, i: (b, 0, 0))] * 2
        args += [ctx_k, ctx_v]
    return pl.pallas_call(
        functools.partial(_attn_kernel, tq=tq, kv_chunk=kv_chunk, n_chunks=seq_len // kv_chunk,
                          has_ctx=has_ctx),
        grid=(batch, qt),
        in_specs=in_specs,
        out_specs=pl.BlockSpec((tq, 512), lambda b, i: (b * qt + i, 0)),
        out_shape=jax.ShapeDtypeStruct((rows, 512), F32),
        compiler_params=_cparams(2),
        name="attention",
    )(*args)


def _retention_tables():
    C = RET_CHUNK
    h = np.arange(RET_HEADS, dtype=np.float64)
    pos = np.arange(C, dtype=np.float64)
    diff = pos[:, None] - pos[None, :]
    inner, qd, kd, cd = [], [], [], []
    for direction, expo in enumerate((RET_DECAY_EXP_FWD, RET_DECAY_EXP_BWD)):
        lg = np.log1p(-np.exp2(-expo - h))[:, None, None]
        if direction == 0:
            inner.append(np.where(diff >= 0, np.exp(lg * np.maximum(diff, 0.0)), 0.0))
            qd.append(np.exp(lg[:, :, 0] * (pos + 1.0)))
            kd.append(np.exp(lg[:, :, 0] * (C - 1.0 - pos)))
        else:
            inner.append(np.where(diff <= 0, np.exp(lg * np.maximum(-diff, 0.0)), 0.0))
            qd.append(np.exp(lg[:, :, 0] * (C - pos)))
            kd.append(np.exp(lg[:, :, 0] * pos))
        cd.append(np.exp(lg[:, 0, 0] * C))
    inner = np.stack(inner, axis=1)
    rowdec = np.stack([np.stack(qd, 1), np.stack(kd, 1)], axis=2)
    rowdec = np.broadcast_to(rowdec[..., None], rowdec.shape + (RET_DK,))
    cd = np.stack(cd, axis=1)
    cd = np.broadcast_to(cd[:, :, None, None], cd.shape + (RET_DK, RET_DK))
    return (jnp.asarray(inner, F32), jnp.asarray(np.ascontiguousarray(rowdec), F32),
            jnp.asarray(np.ascontiguousarray(cd), F32))


def _ret_kernel(q_ref, k_ref, v_ref, g_ref, inner_ref, dec_ref, cd_ref, s0_ref, nw_ref,
                o_ref, st_ref, *, n_chunks):
    C = RET_CHUNK
    for hh in range(2):
        sl = slice(hh * RET_DK, (hh + 1) * RET_DK)

        def chunk(r0, S, direction, hh=hh, sl=sl):
            q = q_ref[pl.ds(r0, C), sl]
            k = k_ref[pl.ds(r0, C), sl]
            v = v_ref[pl.ds(r0, C), sl]
            att = _dot_nt(q, k) * inner_ref[hh, direction]
            o = _dot(att, v) + _dot(q * dec_ref[hh, direction, 0], S)
            S = S * cd_ref[hh, direction] + _dot_tn(k * dec_ref[hh, direction, 1], v)
            return o, S

        def fwd(c, S, chunk=chunk, sl=sl):
            r0 = pl.multiple_of(c * C, C)
            o, S = chunk(r0, S, 0)
            o_ref[pl.ds(r0, C), sl] = o
            return S

        def bwd(i, S, chunk=chunk, sl=sl):
            r0 = pl.multiple_of((n_chunks - 1 - i) * C, C)
            o, S = chunk(r0, S, 1)
            o = o + o_ref[pl.ds(r0, C), sl]
            mu = jnp.mean(o, axis=-1, keepdims=True)
            d = o - mu
            var = jnp.mean(d * d, axis=-1, keepdims=True)
            y = d * lax.rsqrt(var + EPS) * nw_ref[:, sl]
            o_ref[pl.ds(r0, C), sl] = _silu(g_ref[pl.ds(r0, C), sl]) * y
            return S

        st_ref[0, 0, hh] = lax.fori_loop(0, n_chunks, fwd, s0_ref[0, 0, hh])
        st_ref[0, 1, hh] = lax.fori_loop(0, n_chunks, bwd, s0_ref[0, 1, hh])


def _retention(ret, s0, nw, tables, *, batch, seq_len):
    rows = ret.shape[0]
    inner, rowdec, cd = tables
    C = RET_CHUNK
    col = lambda j: pl.BlockSpec((seq_len, LANES), lambda b, hp, j=j: (b, 4 * j + hp))
    st_spec = pl.BlockSpec((1, 2, 2, RET_DK, RET_DK), lambda b, hp: (b, 0, hp, 0, 0))
    return pl.pallas_call(
        functools.partial(_ret_kernel, n_chunks=seq_len // C),
        grid=(batch, RET_HEADS // 2),
        in_specs=[col(0), col(1), col(2), col(3),
                  pl.BlockSpec((2, 2, C, C), lambda b, hp: (hp, 0, 0, 0)),
                  pl.BlockSpec((2, 2, 2, C, RET_DK), lambda b, hp: (hp, 0, 0, 0, 0)),
                  pl.BlockSpec((2, 2, RET_DK, RET_DK), lambda b, hp: (hp, 0, 0, 0)),
                  st_spec,
                  pl.BlockSpec((1, LANES), lambda b, hp: (0, hp))],
        out_specs=[pl.BlockSpec((seq_len, LANES), lambda b, hp: (b, hp)), st_spec],
        out_shape=[jax.ShapeDtypeStruct((rows, 512), F32),
                   jax.ShapeDtypeStruct((batch, 2, RET_HEADS, RET_DK, RET_DK), F32)],
        compiler_params=_cparams(2),
        name="retention",
    )(ret, ret, ret, ret, inner, rowdec, cd, s0, nw)


SOLVE_BASE = 8
CONV_BLOCK = 256
PAD = SUBLANES


def _dn_kernel(q_ref, k_ref, v_ref, g_ref, ab_ref, cw_ref, alog_ref, dtb_ref, nw_ref, s0_ref,
               o_ref, st_ref, pad_ref, qs_ref, ks_ref, vs_ref, of_ref, ob_ref, *, seq_len):
    C = DN_CHUNK
    T = seq_len
    n_chunks = T // C
    head = pl.program_id(1)

    zero_rows = jnp.zeros((PAD, LANES), F32)
    blk = min(CONV_BLOCK, T)
    for t, (src, dst) in enumerate(((q_ref, qs_ref), (k_ref, ks_ref), (v_ref, vs_ref))):
        pad_ref[0:PAD, :] = zero_rows
        pad_ref[PAD + T:PAD + T + PAD, :] = zero_rows
        pad_ref[PAD:PAD + T, :] = src[...]
        w = cw_ref[t, 0]

        def conv(i, carry, dst=dst, w=w, t=t):
            r0 = pl.multiple_of(i * blk, blk)
            xb = pad_ref[pl.ds(r0, blk + 2 * PAD), :]
            prev = pltpu.roll(xb, 1, 0)[PAD:PAD + blk]
            nxt = pltpu.roll(xb, blk + 2 * PAD - 1, 0)[PAD:PAD + blk]
            y = _silu(w[0:1] * prev + w[1:2] * xb[PAD:PAD + blk] + w[2:3] * nxt)
            if t < 2:
                y = y * lax.rsqrt(jnp.sum(y * y, axis=-1, keepdims=True) + EPS)
            if t == 0:
                y = y * (DN_DK ** -0.5)
            dst[pl.ds(r0, blk), :] = y
            return carry

        lax.fori_loop(0, T // blk, conv, 0)

    ri = lax.broadcasted_iota(jnp.int32, (C, C), 0)
    ci = lax.broadcasted_iota(jnp.int32, (C, C), 1)
    eye = ri == ci
    eye_f = eye.astype(F32)
    lane = lax.broadcasted_iota(jnp.int32, (1, LANES), 1)
    masks = ((ri >= ci, ri > ci), (ri <= ci, ri < ci))

    def chunk(r0, S, direction):
        incl, strict = masks[direction]
        sel_a = (lane == direction * DN_HEADS + head).astype(F32)
        sel_b = (lane == 2 * DN_HEADS + direction * DN_HEADS + head).astype(F32)
        ab = ab_ref[pl.ds(r0, C), :]
        da = jnp.sum(ab * sel_a, axis=-1, keepdims=True)
        db = jnp.sum(ab * sel_b, axis=-1, keepdims=True)
        z = da + dtb_ref[direction, 0]
        softplus = jnp.maximum(z, 0.0) + jnp.log1p(jnp.exp(-jnp.abs(z)))
        g = -jnp.exp(alog_ref[direction, 0]) * softplus
        beta = _sigmoid(db)
        G = jnp.dot(incl.astype(F32), g, precision=lax.Precision.HIGHEST,
                    preferred_element_type=F32)
        Gc = G[:, :C]
        Grow = jnp.sum(jnp.where(eye, Gc, 0.0), axis=0, keepdims=True)
        L = jnp.where(incl, jnp.exp(jnp.where(incl, Gc - Grow, 0.0)), 0.0)
        q = qs_ref[pl.ds(r0, C), :]
        k = ks_ref[pl.ds(r0, C), :]
        v = vs_ref[pl.ds(r0, C), :]
        kb = k * beta
        N = jnp.where(strict, -(_dot_nt(kb, k) * L), 0.0)
        same = (ri // SOLVE_BASE) == (ci // SOLVE_BASE)
        P = jnp.where(same, N, 0.0)
        Tm = eye_f + P
        for _ in range(int(math.log2(SOLVE_BASE)) - 1):
            P = _dot(P, P)
            Tm = Tm + _dot(Tm, P)
        size = SOLVE_BASE
        while size < C:
            size *= 2
            wider = (ri // size) == (ci // size)
            X = jnp.where(jnp.logical_and(wider, jnp.logical_not(same)), N, 0.0)
            Tm = Tm + _dot(_dot(Tm, X), Tm)
            same = wider
        eG = jnp.exp(G)
        u = _dot(Tm, v * beta)
        w = _dot(Tm, kb * eG)
        v_new = u - _dot(w, S)
        att = _dot_nt(q, k) * L
        o = _dot(q * eG, S) + _dot(att, v_new)
        g_last = G[C - 1:C] if direction == 0 else G[0:1]
        S = S * jnp.exp(g_last) + _dot_tn(k * jnp.exp(g_last - G), v_new)
        return o, S

    def body(i, carry):
        Sf, Sb = carry
        rf = pl.multiple_of(i * C, C)
        rb = pl.multiple_of((n_chunks - 1 - i) * C, C)
        o_f, Sf = chunk(rf, Sf, 0)
        o_b, Sb = chunk(rb, Sb, 1)
        of_ref[pl.ds(rf, C), :] = o_f
        ob_ref[pl.ds(rb, C), :] = o_b
        return Sf, Sb

    Sf, Sb = lax.fori_loop(0, n_chunks, body, (s0_ref[0, 0, 0], s0_ref[0, 1, 0]))
    st_ref[0, 0, 0] = Sf
    st_ref[0, 1, 0] = Sb

    def fin(i, carry):
        r0 = pl.multiple_of(i * blk, blk)
        o = of_ref[pl.ds(r0, blk), :] + ob_ref[pl.ds(r0, blk), :]
        y = o * lax.rsqrt(jnp.mean(o * o, axis=-1, keepdims=True) + EPS) * nw_ref[...]
        o_ref[pl.ds(r0, blk), :] = y * _silu(g_ref[pl.ds(r0, blk), :])
        return carry

    lax.fori_loop(0, T // blk, fin, 0)


def _deltanet(dn, ab, conv_w, alog, dtb, nw, s0, *, batch, seq_len):
    rows = dn.shape[0]
    T = seq_len
    col = lambda j: pl.BlockSpec((T, LANES), lambda b, h, j=j: (b, 4 * j + h))
    st_spec = pl.BlockSpec((1, 2, 1, DN_DK, DN_DK), lambda b, h: (b, 0, h, 0, 0))
    gate_spec = pl.BlockSpec((2, 1, 1, LANES), lambda b, h: (0, h, 0, 0))
    return pl.pallas_call(
        functools.partial(_dn_kernel, seq_len=T),
        grid=(batch, DN_HEADS),
        in_specs=[col(0), col(1), col(2), col(3),
                  pl.BlockSpec((T, LANES), lambda b, h: (b, 0)),
                  pl.BlockSpec((3, 1, 3, LANES), lambda b, h: (0, h, 0, 0)),
                  gate_spec, gate_spec,
                  pl.BlockSpec((1, LANES), lambda b, h: (0, 0)),
                  st_spec],
        out_specs=[pl.BlockSpec((T, LANES), lambda b, h: (b, h)), st_spec],
        out_shape=[jax.ShapeDtypeStruct((rows, 512), F32),
                   jax.ShapeDtypeStruct((batch, 2, DN_HEADS, DN_DK, DN_DK), F32)],
        scratch_shapes=[pltpu.VMEM((T + 2 * PAD, LANES), F32)] + [pltpu.VMEM((T, LANES), F32)] * 5,
        compiler_params=_cparams(2),
        name="deltanet",
    )(dn, dn, dn, dn, ab, conv_w, alog, dtb, nw, s0)


def _merge_kernel(x_ref, mod_ref, a_ref, r_ref, d_ref, mg_ref, wbr_ref, wout_ref, o_ref):
    d = x_ref.shape[-1]
    merged = jnp.zeros(x_ref.shape, F32)
    for i, br in enumerate((a_ref, r_ref, d_ref)):
        merged = merged + _sigmoid(mg_ref[:, i * d:(i + 1) * d]) * _dot(br[...], wbr_ref[i])
    out = _dot(merged, wout_ref[...])
    o_ref[...] = x_ref[...] + mod_ref[0][5:6] * out


def _merge(x, mod, a, r, dn, mg, w_br, w_out, *, rows_per_cond, tm=512):
    rows, d = x.shape
    tiles_per_cond = rows_per_cond // tm
    row_spec = lambda wd: pl.BlockSpec((tm, wd), lambda i: (i, 0))
    return pl.pallas_call(
        _merge_kernel,
        grid=(rows // tm,),
        in_specs=[row_spec(d),
                  pl.BlockSpec((1, N_MOD, d), lambda i: (i // tiles_per_cond, 0, 0)),
                  row_spec(512), row_spec(512), row_spec(512), row_spec(MG_W),
                  _resident(w_br.shape), _resident(w_out.shape)],
        out_specs=row_spec(d),
        out_shape=jax.ShapeDtypeStruct((rows, d), F32),
        compiler_params=_cparams(1),
        name="merge",
    )(x, mod, a, r, dn, mg, w_br, w_out)


def _rope_tables(seq_len):
    n_freq = HEAD_DIM // 4
    inv = ROPE_THETA ** (-jnp.arange(n_freq, dtype=F32) / n_freq)
    t = jnp.arange(seq_len)
    row = (t // GRID_W).astype(F32)
    colp = (t % GRID_W).astype(F32)
    ang = jnp.concatenate([row[:, None] * inv, colp[:, None] * inv], axis=-1)
    c, s = jnp.cos(ang), jnp.sin(ang)
    cos = jnp.concatenate([c, c, c, c], axis=-1)
    sin = jnp.concatenate([-s, s, -s, s], axis=-1)
    return cos, sin


def _reorder_w_in(w):
    d = w.shape[0]
    o_da = 768 + 2048 + 1536
    o_dg = o_da + 4 * DN_HEADS
    o_mg = o_dg + 512
    return jnp.concatenate([w[:, :o_da], w[:, o_dg:o_mg], w[:, o_da:o_dg],
                            jnp.zeros((d, AB_W - 4 * DN_HEADS), w.dtype), w[:, o_mg:]], axis=1)


def kernel(x_prompt, x_sample, cache_k, cache_v, state_ret, state_delta, c, c_ctx,
           w_mod, b_mod, norm_ffn1, ffn1_w_in, ffn1_w_out, norm_mix, w_in,
           attn_q_norm, attn_k_norm, ret_norm, dn_conv, dn_a_log, dn_dt_bias, dn_norm,
           w_br_attn, w_br_ret, w_br_dn, w_out, norm_ffn2, ffn2_w_in, ffn2_w_out, norm_final):
    bp, tp, d = x_prompt.shape
    bs, ts, _ = x_sample.shape
    depth = w_mod.shape[0]
    past = cache_k.shape[2]

    conds = jnp.concatenate([c_ctx[None, :], c], axis=0)
    mod = _modulation(conds, w_mod, b_mod).reshape(depth, 1 + bs, N_MOD, d)

    ones_bd = jnp.asarray(np.kron(np.eye(ATTN_HEADS), np.ones((HEAD_DIM, HEAD_DIM))), BF16)
    ret_tabs = _retention_tables()
    rope_tabs = _rope_tables(ts)
    ret_zero = jnp.zeros((bp, 2, RET_HEADS, RET_DK, RET_DK), F32)
    dn_zero = jnp.zeros((bp, 2, DN_HEADS, DN_DK, DN_DK), F32)
    fnw = norm_final.reshape(1, d)

    groups = {
        "prompt": dict(x=x_prompt.reshape(bp * tp, d), batch=bp, seq=tp, rows_per_cond=bp * tp, rope=None),
        "sample": dict(x=x_sample.reshape(bs * ts, d), batch=bs, seq=ts, rows_per_cond=ts, rope=rope_tabs),
    }
    new_k, new_v, new_rs, new_ds = [], [], [], []
    for l in range(depth):
        w1_in, w1_out = ffn1_w_in[l].astype(BF16), ffn1_w_out[l].astype(BF16)
        w2_in, w2_out = ffn2_w_in[l].astype(BF16), ffn2_w_out[l].astype(BF16)
        w_proj = _reorder_w_in(w_in[l]).astype(BF16)
        w_br = jnp.stack([w_br_attn[l], w_br_ret[l], w_br_dn[l]]).astype(BF16)
        w_o = w_out[l].astype(BF16)
        gq = jnp.tile(attn_q_norm[l], ATTN_HEADS).reshape(1, 512)
        gk = jnp.tile(attn_k_norm[l], ATTN_KV_HEADS).reshape(1, LANES)
        conv_w = dn_conv[l].reshape(3, 3, DN_HEADS, LANES).transpose(1, 2, 0, 3)
        alog = jnp.broadcast_to(dn_a_log[l][:, :, None, None], (2, DN_HEADS, 1, LANES))
        dtb = jnp.broadcast_to(dn_dt_bias[l][:, :, None, None], (2, DN_HEADS, 1, LANES))
        for name, grp in groups.items():
            is_prompt = name == "prompt"
            x = grp["x"]
            gmod = mod[l, :1] if is_prompt else mod[l, 1:]
            rpc = grp["rows_per_cond"]
            x = _ffn(x, gmod, norm_ffn1[l].reshape(1, d), w1_in, w1_out, fnw,
                     mod_base=0, rows_per_cond=rpc, final=False)
            attn, ret, dn, ab, mg = _inproj(x, gmod, norm_mix[l].reshape(1, d), w_proj, ones_bd, gq, gk,
                                            grp["rope"], rows_per_cond=rpc, seq_len=grp["seq"])
            if is_prompt:
                a_out = _attention(attn, None, None, batch=bp, seq_len=tp)
                rs0, ds0 = ret_zero, dn_zero
            else:
                a_out = _attention(attn, cache_k[:, l].reshape(bs, past, LANES),
                                   cache_v[:, l].reshape(bs, past, LANES), batch=bs, seq_len=ts)
                rs0, ds0 = state_ret[:, l], state_delta[:, l]
            r_out, rs = _retention(ret, rs0, ret_norm[l].reshape(1, 512), ret_tabs,
                                   batch=grp["batch"], seq_len=grp["seq"])
            d_out, ds = _deltanet(dn, ab, conv_w, alog, dtb, dn_norm[l].reshape(1, LANES), ds0,
                                  batch=grp["batch"], seq_len=grp["seq"])
            x = _merge(x, gmod, a_out, r_out, d_out, mg, w_br, w_o, rows_per_cond=rpc)
            x = _ffn(x, gmod, norm_ffn2[l].reshape(1, d), w2_in, w2_out, fnw,
                     mod_base=6, rows_per_cond=rpc, final=(l == depth - 1))
            grp["x"] = x
            if is_prompt:
                new_k.append(attn[:, 512:640].reshape(bp, tp, ATTN_KV_HEADS, HEAD_DIM))
                new_v.append(attn[:, 640:768].reshape(bp, tp, ATTN_KV_HEADS, HEAD_DIM))
                new_rs.append(rs)
                new_ds.append(ds)

    y_prompt = groups["prompt"]["x"].reshape(bp, tp, d)
    y_sample = groups["sample"]["x"].reshape(bs, ts, d)
    return (y_prompt, y_sample, jnp.stack(new_k, axis=1), jnp.stack(new_v, axis=1),
            jnp.stack(new_rs, axis=1), jnp.stack(new_ds, axis=1))
```

```python
import functools
import math

import numpy as np
import jax
import jax.numpy as jnp
from jax import lax
from jax.experimental import pallas as pl
from jax.experimental.pallas import tpu as pltpu

F32 = jnp.float32
BF16 = jnp.bfloat16

EPS = 1e-6
ROPE_THETA = 10000.0
GRID_W = 64
N_MOD = 9

ATTN_HEADS = 8
ATTN_KV_HEADS = 2
HEAD_DIM = 64
RET_HEADS = 8
RET_DK = 64
RET_CHUNK = 128
RET_DECAY_EXP_FWD = 5.0
RET_DECAY_EXP_BWD = 5.5
DN_HEADS = 4
DN_DK = 128
DN_CHUNK = 64
N_BRANCH = 3

LANES = 128
SUBLANES = 8
VMEM_LIMIT = 56 * 1024 * 1024

ATTN_W = 768
RET_W = 2048
DN_W = 2048
AB_W = 128
MG_W = 3072
IN_W = ATTN_W + RET_W + DN_W + AB_W + MG_W


def _cparams(n_axes):
    return pltpu.CompilerParams(dimension_semantics=("parallel",) * n_axes,
                                vmem_limit_bytes=VMEM_LIMIT)


def _resident(shape):
    zeros = (0,) * len(shape)
    return pl.BlockSpec(shape, lambda *_: zeros, pipeline_mode=pl.Buffered(1))


def _dot(a, b):
    return jnp.dot(a.astype(BF16), b.astype(BF16), preferred_element_type=F32)


def _dot_nt(a, b):
    return lax.dot_general(a.astype(BF16), b.astype(BF16), (((1,), (1,)), ((), ())),
                           preferred_element_type=F32)


def _dot_tn(a, b):
    return lax.dot_general(a.astype(BF16), b.astype(BF16), (((0,), (0,)), ((), ())),
                           preferred_element_type=F32)


def _sigmoid(x):
    return 1.0 / (1.0 + jnp.exp(-x))


def _silu(x):
    return x * _sigmoid(x)


def _norm_mod(x, nw, shift, scale):
    y = x * lax.rsqrt(jnp.mean(x * x, axis=-1, keepdims=True) + EPS) * nw
    return y * (1.0 + scale) + shift


def _mod_kernel(c_ref, w_ref, b_ref, o_ref):
    o_ref[0] = _dot(_silu(c_ref[...]), w_ref[0]) + b_ref[0]


def _modulation(conds, w_mod, b_mod):
    depth, d, n = w_mod.shape
    nc = conds.shape[0]
    tn = n // N_MOD
    return pl.pallas_call(
        _mod_kernel,
        grid=(depth, n // tn),
        in_specs=[pl.BlockSpec((nc, d), lambda l, j: (0, 0)),
                  pl.BlockSpec((1, d, tn), lambda l, j: (l, 0, j)),
                  pl.BlockSpec((1, 1, tn), lambda l, j: (l, 0, j))],
        out_specs=pl.BlockSpec((1, nc, tn), lambda l, j: (l, 0, j)),
        out_shape=jax.ShapeDtypeStruct((depth, nc, n), F32),
        compiler_params=_cparams(2),
        name="modulation",
    )(conds, w_mod, b_mod.reshape(depth, 1, n))


FFN_CHUNK = 256


def _ffn_kernel(x_ref, mod_ref, nw_ref, win_ref, wout_ref, fnw_ref, o_ref, *, mod_base, dff, final):
    x = x_ref[...]
    m = mod_ref[0]
    shift, scale, gate = (m[mod_base + i:mod_base + i + 1] for i in range(3))
    h = _norm_mod(x, nw_ref[...], shift, scale).astype(BF16)
    acc = jnp.zeros(x.shape, F32)
    for c in range(dff // FFN_CHUNK):
        lo = c * FFN_CHUNK
        hg = jnp.dot(h, win_ref[:, lo:lo + FFN_CHUNK], preferred_element_type=F32)
        hu = jnp.dot(h, win_ref[:, dff + lo:dff + lo + FFN_CHUNK], preferred_element_type=F32)
        a = (_silu(hg) * hu).astype(BF16)
        acc = acc + jnp.dot(a, wout_ref[lo:lo + FFN_CHUNK, :], preferred_element_type=F32)
    y = x + 0.5 * gate * acc
    if final:
        y = y * lax.rsqrt(jnp.mean(y * y, axis=-1, keepdims=True) + EPS) * fnw_ref[...]
    o_ref[...] = y


def _ffn(x, mod, nw, w_in, w_out, fnw, *, mod_base, rows_per_cond, final, tm=512):
    rows, d = x.shape
    dff = w_out.shape[0]
    tiles_per_cond = rows_per_cond // tm
    return pl.pallas_call(
        functools.partial(_ffn_kernel, mod_base=mod_base, dff=dff, final=final),
        grid=(rows // tm,),
        in_specs=[pl.BlockSpec((tm, d), lambda i: (i, 0)),
                  pl.BlockSpec((1, N_MOD, d), lambda i: (i // tiles_per_cond, 0, 0)),
                  _resident((1, d)),
                  _resident(w_in.shape),
                  _resident(w_out.shape),
                  _resident((1, d))],
        out_specs=pl.BlockSpec((tm, d), lambda i: (i, 0)),
        out_shape=jax.ShapeDtypeStruct((rows, d), F32),
        compiler_params=_cparams(1),
        name="ffn",
    )(x, mod, nw, w_in, w_out, fnw)


def _swap_halves(x):
    n = x.shape[-1]
    lane = lax.broadcasted_iota(jnp.int32, x.shape, 1)
    first = (lane % HEAD_DIM) < (HEAD_DIM // 2)
    return jnp.where(first, pltpu.roll(x, n - HEAD_DIM // 2, 1), pltpu.roll(x, HEAD_DIM // 2, 1))


def _rope(x, cos, sin):
    reps = x.shape[-1] // LANES
    c = jnp.concatenate([cos] * reps, axis=1) if reps > 1 else cos
    s = jnp.concatenate([sin] * reps, axis=1) if reps > 1 else sin
    return x * c + _swap_halves(x) * s


def _head_rms(x, ones_bd, gain):
    sq = x * x
    hi = sq.astype(BF16)
    lo = (sq - hi.astype(F32)).astype(BF16)
    ss = (jnp.dot(hi, ones_bd, preferred_element_type=F32)
          + jnp.dot(lo, ones_bd, preferred_element_type=F32))
    return x * lax.rsqrt(ss * (1.0 / HEAD_DIM) + EPS) * gain


def _inproj_kernel(*refs, rope):
    if rope:
        (x_ref, mod_ref, nw_ref, w_ref, ones_ref, gq_ref, gk_ref, cos_ref, sin_ref,
         attn_ref, ret_ref, dn_ref, ab_ref, mg_ref) = refs
    else:
        (x_ref, mod_ref, nw_ref, w_ref, ones_ref, gq_ref, gk_ref,
         attn_ref, ret_ref, dn_ref, ab_ref, mg_ref) = refs
    m = mod_ref[0]
    h = _norm_mod(x_ref[...], nw_ref[...], m[3:4], m[4:5]).astype(BF16)

    def proj(lo, width):
        return jnp.dot(h, w_ref[:, lo:lo + width], preferred_element_type=F32)

    def rot(v):
        return _rope(v, cos_ref[...], sin_ref[...]) if rope else v

    a = proj(0, ATTN_W)
    q = _head_rms(a[:, :512], ones_ref[...], gq_ref[...])
    k = _head_rms(a[:, 512:640], ones_ref[:LANES, :LANES], gk_ref[...])
    attn_ref[:, :512] = rot(q) * (HEAD_DIM ** -0.5)
    attn_ref[:, 512:640] = rot(k)
    attn_ref[:, 640:768] = a[:, 640:768]

    r = proj(ATTN_W, RET_W)
    ret_ref[:, :512] = rot(r[:, :512]) * (RET_DK ** -0.5)
    ret_ref[:, 512:1024] = rot(r[:, 512:1024])
    ret_ref[:, 1024:] = r[:, 1024:]

    dn_ref[...] = proj(ATTN_W + RET_W, DN_W)
    ab_ref[...] = proj(ATTN_W + RET_W + DN_W, AB_W)
    mg_ref[...] = proj(ATTN_W + RET_W + DN_W + AB_W, MG_W)


def _inproj(x, mod, nw, w, ones_bd, gq, gk, rope_tabs, *, rows_per_cond, seq_len, tm=256):
    rows, d = x.shape
    tiles_per_cond = rows_per_cond // tm
    tiles_per_seq = seq_len // tm
    rope = rope_tabs is not None
    in_specs = [pl.BlockSpec((tm, d), lambda i: (i, 0)),
                pl.BlockSpec((1, N_MOD, d), lambda i: (i // tiles_per_cond, 0, 0)),
                _resident((1, d)),
                _resident(w.shape),
                _resident(ones_bd.shape),
                _resident(gq.shape),
                _resident(gk.shape)]
    args = [x, mod, nw, w, ones_bd, gq, gk]
    if rope:
        in_specs += [pl.BlockSpec((tm, LANES), lambda i: (i % tiles_per_seq, 0))] * 2
        args += list(rope_tabs)
    widths = (ATTN_W, RET_W, DN_W, AB_W, MG_W)
    return pl.pallas_call(
        functools.partial(_inproj_kernel, rope=rope),
        grid=(rows // tm,),
        in_specs=in_specs,
        out_specs=[pl.BlockSpec((tm, wd), lambda i: (i, 0)) for wd in widths],
        out_shape=[jax.ShapeDtypeStruct((rows, wd), F32) for wd in widths],
        compiler_params=_cparams(1),
        name="inproj",
    )(*args)


ATTN_GROUP = ATTN_HEADS // ATTN_KV_HEADS
LOG2E = math.log2(math.e)


def _attn_kernel(*refs, tq, kv_chunk, n_chunks, has_ctx):
    if has_ctx:
        q_ref, k_ref, v_ref, ck_ref, cv_ref, o_ref, s_ref, sc_ref, mx_ref, ls_ref, acc_ref = refs
    else:
        q_ref, k_ref, v_ref, o_ref, s_ref, mx_ref, ls_ref, acc_ref = refs
    lane = lax.broadcasted_iota(jnp.int32, (tq, LANES), 1)
    low = lane < HEAD_DIM
    parts = []
    for hd in range(ATTN_HEADS):
        g = hd // ATTN_GROUP
        blk = q_ref[:, (hd // 2) * LANES:(hd // 2 + 1) * LANES] * LOG2E
        if hd % 2 != g:
            blk = pltpu.roll(blk, HEAD_DIM, 1)
        parts.append(jnp.where(low if g == 0 else jnp.logical_not(low), blk, 0.0))
    qs = jnp.concatenate(parts, axis=0).astype(BF16)

    def lane_fold(op, acc, x):
        for b in range(x.shape[1] // LANES):
            acc = op(acc, x[:, b * LANES:(b + 1) * LANES])
        return acc

    mx_ref[...] = jnp.full(mx_ref.shape, -jnp.inf, F32)

    def scores(c, carry):
        off = pl.multiple_of(c * kv_chunk, kv_chunk)
        s = _dot_nt(qs, k_ref[pl.ds(off, kv_chunk), :])
        s_ref[c] = s
        mx_ref[...] = lane_fold(jnp.maximum, mx_ref[...], s)
        return carry

    lax.fori_loop(0, n_chunks, scores, 0)
    if has_ctx:
        s = _dot_nt(qs, ck_ref[0])
        sc_ref[...] = s
        mx_ref[...] = lane_fold(jnp.maximum, mx_ref[...], s)
    m = jnp.broadcast_to(jnp.max(mx_ref[...], axis=-1, keepdims=True), mx_ref.shape)
    mx_ref[...] = m

    ls_ref[...] = jnp.zeros(ls_ref.shape, F32)
    acc_ref[...] = jnp.zeros(acc_ref.shape, F32)

    def weigh(s, v):
        mb = mx_ref[...]
        p = jnp.exp2(s - jnp.concatenate([mb] * (s.shape[1] // LANES), axis=1))
        ls_ref[...] = lane_fold(jnp.add, ls_ref[...], p)
        acc_ref[...] += _dot(p, v)

    def values(c, carry):
        off = pl.multiple_of(c * kv_chunk, kv_chunk)
        weigh(s_ref[c], v_ref[pl.ds(off, kv_chunk), :])
        return carry

    lax.fori_loop(0, n_chunks, values, 0)
    if has_ctx:
        weigh(sc_ref[...], cv_ref[0])
    res = acc_ref[...] / jnp.sum(ls_ref[...], axis=-1, keepdims=True)

    for b in range(ATTN_HEADS // 2):
        g = (2 * b) // ATTN_GROUP
        even = res[2 * b * tq:(2 * b + 1) * tq]
        odd = res[(2 * b + 1) * tq:(2 * b + 2) * tq]
        if g == 0:
            odd = pltpu.roll(odd, HEAD_DIM, 1)
        else:
            even = pltpu.roll(even, HEAD_DIM, 1)
        o_ref[:, b * LANES:(b + 1) * LANES] = jnp.where(low, even, odd)


def _attention(attn, ctx_k, ctx_v, *, batch, seq_len, tq=128):
    rows = attn.shape[0]
    kv_chunk = min(seq_len, 512)
    n_chunks = seq_len // kv_chunk
    qt = seq_len // tq
    stacked = ATTN_HEADS * tq
    has_ctx = ctx_k is not None
    in_specs = [pl.BlockSpec((tq, 512), lambda b, i: (b * qt + i, 0)),
                pl.BlockSpec((seq_len, LANES), lambda b, i: (b, 4)),
                pl.BlockSpec((seq_len, LANES), lambda b, i: (b, 5))]
    args = [attn, attn, attn]
    scratch = [pltpu.VMEM((n_chunks, stacked, kv_chunk), F32)]
    if has_ctx:
        past = ctx_k.shape[1]
        in_specs += [pl.BlockSpec((1, past, LANES), lambda b, i: (b, 0, 0))] * 2
        args += [ctx_k, ctx_v]
        scratch += [pltpu.VMEM((stacked, past), F32)]
    scratch += [pltpu.VMEM((stacked, LANES), F32)] * 3
    return pl.pallas_call(
        functools.partial(_attn_kernel, tq=tq, kv_chunk=kv_chunk, n_chunks=n_chunks, has_ctx=has_ctx),
        grid=(batch, qt),
        in_specs=in_specs,
        out_specs=pl.BlockSpec((tq, 512), lambda b, i: (b * qt + i, 0)),
        out_shape=jax.ShapeDtypeStruct((rows, 512), F32),
        scratch_shapes=scratch,
        compiler_params=_cparams(2),
        name="attention",
    )(*args)


def _retention_tables():
    C = RET_CHUNK
    h = np.arange(RET_HEADS, dtype=np.float64)
    pos = np.arange(C, dtype=np.float64)
    diff = pos[:, None] - pos[None, :]
    inner, qd, kd, cd = [], [], [], []
    for direction, expo in enumerate((RET_DECAY_EXP_FWD, RET_DECAY_EXP_BWD)):
        lg = np.log1p(-np.exp2(-expo - h))[:, None, None]
        if direction == 0:
            inner.append(np.where(diff >= 0, np.exp(lg * np.maximum(diff, 0.0)), 0.0))
            qd.append(np.exp(lg[:, :, 0] * (pos + 1.0)))
            kd.append(np.exp(lg[:, :, 0] * (C - 1.0 - pos)))
        else:
            inner.append(np.where(diff <= 0, np.exp(lg * np.maximum(-diff, 0.0)), 0.0))
            qd.append(np.exp(lg[:, :, 0] * (C - pos)))
            kd.append(np.exp(lg[:, :, 0] * pos))
        cd.append(np.exp(lg[:, 0, 0] * C))
    inner = np.stack(inner, axis=1)
    rowdec = np.stack([np.stack(qd, 1), np.stack(kd, 1)], axis=2)
    rowdec = np.broadcast_to(rowdec[..., None], rowdec.shape + (RET_DK,))
    cd = np.stack(cd, axis=1)
    cd = np.broadcast_to(cd[:, :, None, None], cd.shape + (RET_DK, RET_DK))
    return (jnp.asarray(inner, F32), jnp.asarray(np.ascontiguousarray(rowdec), F32),
            jnp.asarray(np.ascontiguousarray(cd), F32))


RET_BLOCK = 4


def _ret_kernel(q_ref, k_ref, v_ref, g_ref, inner_ref, dec_ref, cd_ref, s0_ref, nw_ref,
                o_ref, st_ref, *, n_chunks):
    C = RET_CHUNK
    nb = min(RET_BLOCK, n_chunks)
    heads = (slice(0, RET_DK), slice(RET_DK, 2 * RET_DK))

    def sweep(direction):
        def body(i, states):
            order = [i * nb + j for j in range(nb)]
            if direction == 1:
                order = [n_chunks - 1 - c for c in order]
            starts = [pl.multiple_of(c * C, C) for c in order]
            q = [q_ref[pl.ds(r, C), :] for r in starts]
            k = [k_ref[pl.ds(r, C), :] for r in starts]
            v = [v_ref[pl.ds(r, C), :] for r in starts]
            if direction == 1:
                prev = [o_ref[pl.ds(r, C), :] for r in starts]
                gate = [g_ref[pl.ds(r, C), :] for r in starts]
            att = [[_dot_nt(q[j][:, sl], k[j][:, sl]) * inner_ref[h, direction] for j in range(nb)]
                   for h, sl in enumerate(heads)]
            kv = [[_dot_tn(k[j][:, sl] * dec_ref[h, direction, 1], v[j][:, sl]) for j in range(nb)]
                  for h, sl in enumerate(heads)]
            seen = []
            new_states = []
            for h in range(2):
                S = states[h]
                per_chunk = []
                for j in range(nb):
                    per_chunk.append(S)
                    S = S * cd_ref[h, direction] + kv[h][j]
                seen.append(per_chunk)
                new_states.append(S)
            for j in range(nb):
                halves = []
                for h, sl in enumerate(heads):
                    o = _dot(att[h][j], v[j][:, sl]) + _dot(q[j][:, sl] * dec_ref[h, direction, 0], seen[h][j])
                    if direction == 1:
                        o = o + prev[j][:, sl]
                        mu = jnp.mean(o, axis=-1, keepdims=True)
                        d = o - mu
                        var = jnp.mean(d * d, axis=-1, keepdims=True)
                        o = d * lax.rsqrt(var + EPS)
                    halves.append(o)
                o = jnp.concatenate(halves, axis=1)
                if direction == 1:
                    o = o * nw_ref[...] * _silu(gate[j])
                o_ref[pl.ds(starts[j], C), :] = o
            return tuple(new_states)

        final = lax.fori_loop(0, n_chunks // nb, body, (s0_ref[0, direction, 0], s0_ref[0, direction, 1]))
        st_ref[0, direction, 0] = final[0]
        st_ref[0, direction, 1] = final[1]

    sweep(0)
    sweep(1)


def _retention(ret, s0, nw, tables, *, batch, seq_len):
    rows = ret.shape[0]
    inner, rowdec, cd = tables
    C = RET_CHUNK
    col = lambda j: pl.BlockSpec((seq_len, LANES), lambda b, hp, j=j: (b, 4 * j + hp))
    st_spec = pl.BlockSpec((1, 2, 2, RET_DK, RET_DK), lambda b, hp: (b, 0, hp, 0, 0))
    return pl.pallas_call(
        functools.partial(_ret_kernel, n_chunks=seq_len // C),
        grid=(batch, RET_HEADS // 2),
        in_specs=[col(0), col(1), col(2), col(3),
                  pl.BlockSpec((2, 2, C, C), lambda b, hp: (hp, 0, 0, 0)),
                  pl.BlockSpec((2, 2, 2, C, RET_DK), lambda b, hp: (hp, 0, 0, 0, 0)),
                  pl.BlockSpec((2, 2, RET_DK, RET_DK), lambda b, hp: (hp, 0, 0, 0)),
                  st_spec,
                  pl.BlockSpec((1, LANES), lambda b, hp: (0, hp))],
        out_specs=[pl.BlockSpec((seq_len, LANES), lambda b, hp: (b, hp)), st_spec],
        out_shape=[jax.ShapeDtypeStruct((rows, 512), F32),
                   jax.ShapeDtypeStruct((batch, 2, RET_HEADS, RET_DK, RET_DK), F32)],
        compiler_params=_cparams(2),
        name="retention",
    )(ret, ret, ret, ret, inner, rowdec, cd, s0, nw)


DN_BLOCK = 4
SOLVE_BASE = 8
CONV_BLOCK = 256
PAD = SUBLANES


def _dn_kernel(q_ref, k_ref, v_ref, g_ref, ab_ref, cw_ref, alog_ref, dtb_ref, nw_ref, s0_ref,
               o_ref, st_ref, pad_ref, qs_ref, ks_ref, vs_ref, of_ref, ob_ref,
               u_ref, w_ref, qe_ref, kd_ref, att_ref, dec_ref, *, seq_len):
    C = DN_CHUNK
    T = seq_len
    n_chunks = T // C
    head = pl.program_id(1)

    zero_rows = jnp.zeros((PAD, LANES), F32)
    blk = min(CONV_BLOCK, T)
    for t, (src, dst) in enumerate(((q_ref, qs_ref), (k_ref, ks_ref), (v_ref, vs_ref))):
        pad_ref[0:PAD, :] = zero_rows
        pad_ref[PAD + T:PAD + T + PAD, :] = zero_rows
        pad_ref[PAD:PAD + T, :] = src[...]
        w = cw_ref[t, 0]

        def conv(i, carry, dst=dst, w=w, t=t):
            r0 = pl.multiple_of(i * blk, blk)
            xb = pad_ref[pl.ds(r0, blk + 2 * PAD), :]
            prev = pltpu.roll(xb, 1, 0)[PAD:PAD + blk]
            nxt = pltpu.roll(xb, blk + 2 * PAD - 1, 0)[PAD:PAD + blk]
            y = _silu(w[0:1] * prev + w[1:2] * xb[PAD:PAD + blk] + w[2:3] * nxt)
            if t < 2:
                y = y * lax.rsqrt(jnp.sum(y * y, axis=-1, keepdims=True) + EPS)
            if t == 0:
                y = y * (DN_DK ** -0.5)
            dst[pl.ds(r0, blk), :] = y
            return carry

        lax.fori_loop(0, T // blk, conv, 0)

    ri = lax.broadcasted_iota(jnp.int32, (C, C), 0)
    ci = lax.broadcasted_iota(jnp.int32, (C, C), 1)
    eye = ri == ci
    eye_f = eye.astype(F32)
    lane = lax.broadcasted_iota(jnp.int32, (1, LANES), 1)
    masks = ((ri >= ci, ri > ci), (ri <= ci, ri < ci))

    def load(r0):
        return tuple(ref[pl.ds(r0, C), :] for ref in (ab_ref, qs_ref, ks_ref, vs_ref))

    def prep(operands, direction):
        ab, q, k, v = operands
        incl, strict = masks[direction]
        sel_a = (lane == direction * DN_HEADS + head).astype(F32)
        sel_b = (lane == 2 * DN_HEADS + direction * DN_HEADS + head).astype(F32)
        da = jnp.sum(ab * sel_a, axis=-1, keepdims=True)
        db = jnp.sum(ab * sel_b, axis=-1, keepdims=True)
        z = da + dtb_ref[direction, 0]
        softplus = jnp.maximum(z, 0.0) + jnp.log1p(jnp.exp(-jnp.abs(z)))
        g = -jnp.exp(alog_ref[direction, 0]) * softplus
        beta = _sigmoid(db)
        kb = k * beta
        G = jnp.dot(incl.astype(F32), g, precision=lax.Precision.HIGHEST,
                    preferred_element_type=F32)
        kk = _dot_nt(kb, k)
        qk = _dot_nt(q, k)
        yield
        Gc = G[:, :C]
        Grow = jnp.sum(jnp.where(eye, Gc, 0.0), axis=0, keepdims=True)
        L = jnp.where(incl, jnp.exp(jnp.where(incl, Gc - Grow, 0.0)), 0.0)
        N = jnp.where(strict, -(kk * L), 0.0)
        same = (ri // SOLVE_BASE) == (ci // SOLVE_BASE)
        P = jnp.where(same, N, 0.0)
        Tm = eye_f + P
        P = _dot(P, P)
        yield
        for _ in range(int(math.log2(SOLVE_BASE)) - 2):
            Tm, P = Tm + _dot(Tm, P), _dot(P, P)
            yield
        Tm = Tm + _dot(Tm, P)
        yield
        size = SOLVE_BASE
        while size < C:
            size *= 2
            wider = (ri // size) == (ci // size)
            X = jnp.where(jnp.logical_and(wider, jnp.logical_not(same)), N, 0.0)
            TX = _dot(Tm, X)
            yield
            Tm = Tm + _dot(TX, Tm)
            yield
            same = wider
        eG = jnp.exp(G)
        g_last = G[C - 1:C] if direction == 0 else G[0:1]
        return (_dot(Tm, v * beta), _dot(Tm, kb * eG).astype(BF16), (q * eG).astype(BF16),
                (k * jnp.exp(g_last - G)).astype(BF16), (qk * L).astype(BF16),
                jnp.broadcast_to(jnp.exp(g_last), (SUBLANES, LANES)))

    def run_staged(generators):
        results = [None] * len(generators)
        live = list(enumerate(generators))
        while live:
            still = []
            for idx, gen in live:
                try:
                    next(gen)
                    still.append((idx, gen))
                except StopIteration as done:
                    results[idx] = done.value
            live = still
        return results

    slots = (u_ref, w_ref, qe_ref, kd_ref, att_ref, dec_ref)

    def step(slot, S):
        Sb16 = S.astype(BF16)
        wS = _dot(w_ref[slot], Sb16)
        qS = _dot(qe_ref[slot], Sb16)
        yield
        v_new = (u_ref[slot] - wS).astype(BF16)
        o = qS + _dot(att_ref[slot], v_new)
        S = S * dec_ref[slot, 0:1] + _dot_tn(kd_ref[slot], v_new)
        return o, S

    nb = min(DN_BLOCK, n_chunks)

    def body(i, carry):
        Sf, Sb = carry
        rows_f = [pl.multiple_of((i * nb + j) * C, C) for j in range(nb)]
        rows_b = [pl.multiple_of((n_chunks - 1 - (i * nb + j)) * C, C) for j in range(nb)]
        operands = [load(r) for r in rows_f + rows_b]
        prepared = run_staged([prep(operands[s], s // nb) for s in range(2 * nb)])
        for s in range(2 * nb):
            for ref, val in zip(slots, prepared[s]):
                ref[s] = val
        for j in range(nb):
            (o_f, Sf), (o_b, Sb) = run_staged([step(j, Sf), step(nb + j, Sb)])
            of_ref[pl.ds(rows_f[j], C), :] = o_f
            ob_ref[pl.ds(rows_b[j], C), :] = o_b
        return Sf, Sb

    Sf, Sb = lax.fori_loop(0, n_chunks // nb, body, (s0_ref[0, 0, 0], s0_ref[0, 1, 0]))
    st_ref[0, 0, 0] = Sf
    st_ref[0, 1, 0] = Sb

    def fin(i, carry):
        r0 = pl.multiple_of(i * blk, blk)
        o = of_ref[pl.ds(r0, blk), :] + ob_ref[pl.ds(r0, blk), :]
        y = o * lax.rsqrt(jnp.mean(o * o, axis=-1, keepdims=True) + EPS) * nw_ref[...]
        o_ref[pl.ds(r0, blk), :] = y * _silu(g_ref[pl.ds(r0, blk), :])
        return carry

    lax.fori_loop(0, T // blk, fin, 0)


def _deltanet(dn, ab, conv_w, alog, dtb, nw, s0, *, batch, seq_len):
    rows = dn.shape[0]
    T = seq_len
    C = DN_CHUNK
    nb = min(DN_BLOCK, T // C)
    col = lambda j: pl.BlockSpec((T, LANES), lambda b, h, j=j: (b, 4 * j + h))
    st_spec = pl.BlockSpec((1, 2, 1, DN_DK, DN_DK), lambda b, h: (b, 0, h, 0, 0))
    gate_spec = pl.BlockSpec((2, 1, 1, LANES), lambda b, h: (0, h, 0, 0))
    return pl.pallas_call(
        functools.partial(_dn_kernel, seq_len=T),
        grid=(batch, DN_HEADS),
        in_specs=[col(0), col(1), col(2), col(3),
                  pl.BlockSpec((T, LANES), lambda b, h: (b, 0)),
                  pl.BlockSpec((3, 1, 3, LANES), lambda b, h: (0, h, 0, 0)),
                  gate_spec, gate_spec,
                  pl.BlockSpec((1, LANES), lambda b, h: (0, 0)),
                  st_spec],
        out_specs=[pl.BlockSpec((T, LANES), lambda b, h: (b, h)), st_spec],
        out_shape=[jax.ShapeDtypeStruct((rows, 512), F32),
                   jax.ShapeDtypeStruct((batch, 2, DN_HEADS, DN_DK, DN_DK), F32)],
        scratch_shapes=([pltpu.VMEM((T + 2 * PAD, LANES), F32)] + [pltpu.VMEM((T, LANES), F32)] * 5
                        + [pltpu.VMEM((2 * nb, C, LANES), F32)] + [pltpu.VMEM((2 * nb, C, LANES), BF16)] * 3
                        + [pltpu.VMEM((2 * nb, C, C), BF16), pltpu.VMEM((2 * nb, SUBLANES, LANES), F32)]),
        compiler_params=_cparams(2),
        name="deltanet",
    )(dn, dn, dn, dn, ab, conv_w, alog, dtb, nw, s0)


def _merge_kernel(x_ref, mod_ref, a_ref, r_ref, d_ref, mg_ref, wbr_ref, wout_ref, o_ref):
    d = x_ref.shape[-1]
    merged = jnp.zeros(x_ref.shape, F32)
    for i, br in enumerate((a_ref, r_ref, d_ref)):
        merged = merged + _sigmoid(mg_ref[:, i * d:(i + 1) * d]) * _dot(br[...], wbr_ref[i])
    out = _dot(merged, wout_ref[...])
    o_ref[...] = x_ref[...] + mod_ref[0][5:6] * out


def _merge(x, mod, a, r, dn, mg, w_br, w_out, *, rows_per_cond, tm=512):
    rows, d = x.shape
    tiles_per_cond = rows_per_cond // tm
    row_spec = lambda wd: pl.BlockSpec((tm, wd), lambda i: (i, 0))
    return pl.pallas_call(
        _merge_kernel,
        grid=(rows // tm,),
        in_specs=[row_spec(d),
                  pl.BlockSpec((1, N_MOD, d), lambda i: (i // tiles_per_cond, 0, 0)),
                  row_spec(512), row_spec(512), row_spec(512), row_spec(MG_W),
                  _resident(w_br.shape), _resident(w_out.shape)],
        out_specs=row_spec(d),
        out_shape=jax.ShapeDtypeStruct((rows, d), F32),
        compiler_params=_cparams(1),
        name="merge",
    )(x, mod, a, r, dn, mg, w_br, w_out)


def _rope_tables(seq_len):
    n_freq = HEAD_DIM // 4
    inv = ROPE_THETA ** (-jnp.arange(n_freq, dtype=F32) / n_freq)
    t = jnp.arange(seq_len)
    row = (t // GRID_W).astype(F32)
    colp = (t % GRID_W).astype(F32)
    ang = jnp.concatenate([row[:, None] * inv, colp[:, None] * inv], axis=-1)
    c, s = jnp.cos(ang), jnp.sin(ang)
    cos = jnp.concatenate([c, c, c, c], axis=-1)
    sin = jnp.concatenate([-s, s, -s, s], axis=-1)
    return cos, sin


def _reorder_w_in(w):
    d = w.shape[0]
    o_da = 768 + 2048 + 1536
    o_dg = o_da + 4 * DN_HEADS
    o_mg = o_dg + 512
    return jnp.concatenate([w[:, :o_da], w[:, o_dg:o_mg], w[:, o_da:o_dg],
                            jnp.zeros((d, AB_W - 4 * DN_HEADS), w.dtype), w[:, o_mg:]], axis=1)


def kernel(x_prompt, x_sample, cache_k, cache_v, state_ret, state_delta, c, c_ctx,
           w_mod, b_mod, norm_ffn1, ffn1_w_in, ffn1_w_out, norm_mix, w_in,
           attn_q_norm, attn_k_norm, ret_norm, dn_conv, dn_a_log, dn_dt_bias, dn_norm,
           w_br_attn, w_br_ret, w_br_dn, w_out, norm_ffn2, ffn2_w_in, ffn2_w_out, norm_final):
    bp, tp, d = x_prompt.shape
    bs, ts, _ = x_sample.shape
    depth = w_mod.shape[0]
    past = cache_k.shape[2]

    conds = jnp.concatenate([c_ctx[None, :], c], axis=0)
    mod = _modulation(conds, w_mod, b_mod).reshape(depth, 1 + bs, N_MOD, d)

    ones_bd = jnp.asarray(np.kron(np.eye(ATTN_HEADS), np.ones((HEAD_DIM, HEAD_DIM))), BF16)
    ret_tabs = _retention_tables()
    rope_tabs = _rope_tables(ts)
    ret_zero = jnp.zeros((bp, 2, RET_HEADS, RET_DK, RET_DK), F32)
    dn_zero = jnp.zeros((bp, 2, DN_HEADS, DN_DK, DN_DK), F32)
    fnw = norm_final.reshape(1, d)

    groups = {
        "prompt": dict(x=x_prompt.reshape(bp * tp, d), batch=bp, seq=tp, rows_per_cond=bp * tp, rope=None),
        "sample": dict(x=x_sample.reshape(bs * ts, d), batch=bs, seq=ts, rows_per_cond=ts, rope=rope_tabs),
    }
    new_k, new_v, new_rs, new_ds = [], [], [], []
    for l in range(depth):
        w1_in, w1_out = ffn1_w_in[l].astype(BF16), ffn1_w_out[l].astype(BF16)
        w2_in, w2_out = ffn2_w_in[l].astype(BF16), ffn2_w_out[l].astype(BF16)
        w_proj = _reorder_w_in(w_in[l]).astype(BF16)
        w_br = jnp.stack([w_br_attn[l], w_br_ret[l], w_br_dn[l]]).astype(BF16)
        w_o = w_out[l].astype(BF16)
        gq = jnp.tile(attn_q_norm[l], ATTN_HEADS).reshape(1, 512)
        gk = jnp.tile(attn_k_norm[l], ATTN_KV_HEADS).reshape(1, LANES)
        conv_w = dn_conv[l].reshape(3, 3, DN_HEADS, LANES).transpose(1, 2, 0, 3)
        alog = jnp.broadcast_to(dn_a_log[l][:, :, None, None], (2, DN_HEADS, 1, LANES))
        dtb = jnp.broadcast_to(dn_dt_bias[l][:, :, None, None], (2, DN_HEADS, 1, LANES))
        for name, grp in groups.items():
            is_prompt = name == "prompt"
            x = grp["x"]
            gmod = mod[l, :1] if is_prompt else mod[l, 1:]
            rpc = grp["rows_per_cond"]
            x = _ffn(x, gmod, norm_ffn1[l].reshape(1, d), w1_in, w1_out, fnw,
                     mod_base=0, rows_per_cond=rpc, final=False)
            attn, ret, dn, ab, mg = _inproj(x, gmod, norm_mix[l].reshape(1, d), w_proj, ones_bd, gq, gk,
                                            grp["rope"], rows_per_cond=rpc, seq_len=grp["seq"])
            if is_prompt:
                a_out = _attention(attn, None, None, batch=bp, seq_len=tp)
                rs0, ds0 = ret_zero, dn_zero
            else:
                a_out = _attention(attn, cache_k[:, l].reshape(bs, past, LANES),
                                   cache_v[:, l].reshape(bs, past, LANES), batch=bs, seq_len=ts)
                rs0, ds0 = state_ret[:, l], state_delta[:, l]
            r_out, rs = _retention(ret, rs0, ret_norm[l].reshape(1, 512), ret_tabs,
                                   batch=grp["batch"], seq_len=grp["seq"])
            d_out, ds = _deltanet(dn, ab, conv_w, alog, dtb, dn_norm[l].reshape(1, LANES), ds0,
                                  batch=grp["batch"], seq_len=grp["seq"])
            x = _merge(x, gmod, a_out, r_out, d_out, mg, w_br, w_o, rows_per_cond=rpc)
            x = _ffn(x, gmod, norm_ffn2[l].reshape(1, d), w2_in, w2_out, fnw,
                     mod_base=6, rows_per_cond=rpc, final=(l == depth - 1))
            grp["x"] = x
            if is_prompt:
                new_k.append(attn[:, 512:640].reshape(bp, tp, ATTN_KV_HEADS, HEAD_DIM))
                new_v.append(attn[:, 640:768].reshape(bp, tp, ATTN_KV_HEADS, HEAD_DIM))
                new_rs.append(rs)
                new_ds.append(ds)

    y_prompt = groups["prompt"]["x"].reshape(bp, tp, d)
    y_sample = groups["sample"]["x"].reshape(bs, ts, d)
    return (y_prompt, y_sample, jnp.stack(new_k, axis=1), jnp.stack(new_v, axis=1),
            jnp.stack(new_rs, axis=1), jnp.stack(new_ds, axis=1))
```

```python
import functools
import math

import numpy as np
import jax
import jax.numpy as jnp
from jax import lax
from jax.experimental import pallas as pl
from jax.experimental.pallas import tpu as pltpu

F32 = jnp.float32
BF16 = jnp.bfloat16

EPS = 1e-6
ROPE_THETA = 10000.0
GRID_W = 64
N_MOD = 9

ATTN_HEADS = 8
ATTN_KV_HEADS = 2
HEAD_DIM = 64
RET_HEADS = 8
RET_DK = 64
RET_CHUNK = 128
RET_DECAY_EXP_FWD = 5.0
RET_DECAY_EXP_BWD = 5.5
DN_HEADS = 4
DN_DK = 128
DN_CHUNK = 64
N_BRANCH = 3

LANES = 128
SUBLANES = 8
VMEM_LIMIT = 56 * 1024 * 1024

ATTN_W = 768
RET_W = 2048
DN_W = 2048
AB_W = 128
MG_W = 3072
IN_W = ATTN_W + RET_W + DN_W + AB_W + MG_W


def _cparams(n_axes):
    return pltpu.CompilerParams(dimension_semantics=("parallel",) * n_axes,
                                vmem_limit_bytes=VMEM_LIMIT)


def _resident(shape):
    zeros = (0,) * len(shape)
    return pl.BlockSpec(shape, lambda *_: zeros, pipeline_mode=pl.Buffered(1))


def _dot(a, b):
    return jnp.dot(a.astype(BF16), b.astype(BF16), preferred_element_type=F32)


def _dot_nt(a, b):
    return lax.dot_general(a.astype(BF16), b.astype(BF16), (((1,), (1,)), ((), ())),
                           preferred_element_type=F32)


def _dot_tn(a, b):
    return lax.dot_general(a.astype(BF16), b.astype(BF16), (((0,), (0,)), ((), ())),
                           preferred_element_type=F32)


def _sigmoid(x):
    return 1.0 / (1.0 + jnp.exp(-x))


def _silu(x):
    return x * _sigmoid(x)


def _norm_mod(x, nw, shift, scale):
    y = x * lax.rsqrt(jnp.mean(x * x, axis=-1, keepdims=True) + EPS) * nw
    return y * (1.0 + scale) + shift


def _mod_kernel(c_ref, w_ref, b_ref, o_ref):
    o_ref[0] = _dot(_silu(c_ref[...]), w_ref[0]) + b_ref[0]


def _modulation(conds, w_mod, b_mod):
    depth, d, n = w_mod.shape
    nc = conds.shape[0]
    tn = n // N_MOD
    return pl.pallas_call(
        _mod_kernel,
        grid=(depth, n // tn),
        in_specs=[pl.BlockSpec((nc, d), lambda l, j: (0, 0)),
                  pl.BlockSpec((1, d, tn), lambda l, j: (l, 0, j)),
                  pl.BlockSpec((1, 1, tn), lambda l, j: (l, 0, j))],
        out_specs=pl.BlockSpec((1, nc, tn), lambda l, j: (l, 0, j)),
        out_shape=jax.ShapeDtypeStruct((depth, nc, n), F32),
        compiler_params=_cparams(2),
        name="modulation",
    )(conds, w_mod, b_mod.reshape(depth, 1, n))


FFN_CHUNK = 256


def _ffn_kernel(x_ref, mod_ref, nw_ref, win_ref, wout_ref, fnw_ref, o_ref, *, mod_base, dff, final):
    x = x_ref[...]
    m = mod_ref[0]
    shift, scale, gate = (m[mod_base + i:mod_base + i + 1] for i in range(3))
    h = _norm_mod(x, nw_ref[...], shift, scale).astype(BF16)
    acc = jnp.zeros(x.shape, F32)
    for c in range(dff // FFN_CHUNK):
        lo = c * FFN_CHUNK
        hg = jnp.dot(h, win_ref[:, lo:lo + FFN_CHUNK], preferred_element_type=F32)
        hu = jnp.dot(h, win_ref[:, dff + lo:dff + lo + FFN_CHUNK], preferred_element_type=F32)
        a = (_silu(hg) * hu).astype(BF16)
        acc = acc + jnp.dot(a, wout_ref[lo:lo + FFN_CHUNK, :], preferred_element_type=F32)
    y = x + 0.5 * gate * acc
    if final:
        y = y * lax.rsqrt(jnp.mean(y * y, axis=-1, keepdims=True) + EPS) * fnw_ref[...]
    o_ref[...] = y


def _ffn(x, mod, nw, w_in, w_out, fnw, *, mod_base, rows_per_cond, final, tm=512):
    rows, d = x.shape
    dff = w_out.shape[0]
    tiles_per_cond = rows_per_cond // tm
    return pl.pallas_call(
        functools.partial(_ffn_kernel, mod_base=mod_base, dff=dff, final=final),
        grid=(rows // tm,),
        in_specs=[pl.BlockSpec((tm, d), lambda i: (i, 0)),
                  pl.BlockSpec((1, N_MOD, d), lambda i: (i // tiles_per_cond, 0, 0)),
                  _resident((1, d)),
                  _resident(w_in.shape),
                  _resident(w_out.shape),
                  _resident((1, d))],
        out_specs=pl.BlockSpec((tm, d), lambda i: (i, 0)),
        out_shape=jax.ShapeDtypeStruct((rows, d), F32),
        compiler_params=_cparams(1),
        name="ffn",
    )(x, mod, nw, w_in, w_out, fnw)


def _swap_halves(x):
    n = x.shape[-1]
    lane = lax.broadcasted_iota(jnp.int32, x.shape, 1)
    first = (lane % HEAD_DIM) < (HEAD_DIM // 2)
    return jnp.where(first, pltpu.roll(x, n - HEAD_DIM // 2, 1), pltpu.roll(x, HEAD_DIM // 2, 1))


def _rope(x, cos, sin):
    reps = x.shape[-1] // LANES
    c = jnp.concatenate([cos] * reps, axis=1) if reps > 1 else cos
    s = jnp.concatenate([sin] * reps, axis=1) if reps > 1 else sin
    return x * c + _swap_halves(x) * s


def _head_rms(x, ones_bd, gain):
    sq = x * x
    hi = sq.astype(BF16)
    lo = (sq - hi.astype(F32)).astype(BF16)
    ss = (jnp.dot(hi, ones_bd, preferred_element_type=F32)
          + jnp.dot(lo, ones_bd, preferred_element_type=F32))
    return x * lax.rsqrt(ss * (1.0 / HEAD_DIM) + EPS) * gain


def _inproj_kernel(*refs, rope):
    if rope:
        (x_ref, mod_ref, nw_ref, w_ref, ones_ref, gq_ref, gk_ref, cos_ref, sin_ref,
         attn_ref, ret_ref, dn_ref, ab_ref, mg_ref) = refs
    else:
        (x_ref, mod_ref, nw_ref, w_ref, ones_ref, gq_ref, gk_ref,
         attn_ref, ret_ref, dn_ref, ab_ref, mg_ref) = refs
    m = mod_ref[0]
    h = _norm_mod(x_ref[...], nw_ref[...], m[3:4], m[4:5]).astype(BF16)

    def proj(lo, width):
        return jnp.dot(h, w_ref[:, lo:lo + width], preferred_element_type=F32)

    def rot(v):
        return _rope(v, cos_ref[...], sin_ref[...]) if rope else v

    a = proj(0, ATTN_W)
    q = _head_rms(a[:, :512], ones_ref[...], gq_ref[...])
    k = _head_rms(a[:, 512:640], ones_ref[:LANES, :LANES], gk_ref[...])
    attn_ref[:, :512] = rot(q) * (HEAD_DIM ** -0.5)
    attn_ref[:, 512:640] = rot(k)
    attn_ref[:, 640:768] = a[:, 640:768]

    r = proj(ATTN_W, RET_W)
    ret_ref[:, :512] = rot(r[:, :512]) * (RET_DK ** -0.5)
    ret_ref[:, 512:1024] = rot(r[:, 512:1024])
    ret_ref[:, 1024:] = r[:, 1024:]

    dn_ref[...] = proj(ATTN_W + RET_W, DN_W)
    ab_ref[...] = proj(ATTN_W + RET_W + DN_W, AB_W)
    mg_ref[...] = proj(ATTN_W + RET_W + DN_W + AB_W, MG_W)


def _inproj(x, mod, nw, w, ones_bd, gq, gk, rope_tabs, *, rows_per_cond, seq_len, tm=256):
    rows, d = x.shape
    tiles_per_cond = rows_per_cond // tm
    tiles_per_seq = seq_len // tm
    rope = rope_tabs is not None
    in_specs = [pl.BlockSpec((tm, d), lambda i: (i, 0)),
                pl.BlockSpec((1, N_MOD, d), lambda i: (i // tiles_per_cond, 0, 0)),
                _resident((1, d)),
                _resident(w.shape),
                _resident(ones_bd.shape),
                _resident(gq.shape),
                _resident(gk.shape)]
    args = [x, mod, nw, w, ones_bd, gq, gk]
    if rope:
        in_specs += [pl.BlockSpec((tm, LANES), lambda i: (i % tiles_per_seq, 0))] * 2
        args += list(rope_tabs)
    widths = (ATTN_W, RET_W, DN_W, AB_W, MG_W)
    return pl.pallas_call(
        functools.partial(_inproj_kernel, rope=rope),
        grid=(rows // tm,),
        in_specs=in_specs,
        out_specs=[pl.BlockSpec((tm, wd), lambda i: (i, 0)) for wd in widths],
        out_shape=[jax.ShapeDtypeStruct((rows, wd), F32) for wd in widths],
        compiler_params=_cparams(1),
        name="inproj",
    )(*args)


ATTN_GROUP = ATTN_HEADS // ATTN_KV_HEADS
LOG2E = math.log2(math.e)


def _attn_kernel(*refs, tq, kv_chunk, n_chunks, has_ctx):
    if has_ctx:
        q_ref, k_ref, v_ref, ck_ref, cv_ref, o_ref, mx_ref, ls_ref, acc_ref = refs
    else:
        q_ref, k_ref, v_ref, o_ref, mx_ref, ls_ref, acc_ref = refs
    lane = lax.broadcasted_iota(jnp.int32, (tq, LANES), 1)
    low = lane < HEAD_DIM
    parts = []
    for hd in range(ATTN_HEADS):
        g = hd // ATTN_GROUP
        blk = q_ref[:, (hd // 2) * LANES:(hd // 2 + 1) * LANES] * LOG2E
        if hd % 2 != g:
            blk = pltpu.roll(blk, HEAD_DIM, 1)
        parts.append(jnp.where(low if g == 0 else jnp.logical_not(low), blk, 0.0))
    qs = jnp.concatenate(parts, axis=0).astype(BF16)

    def lane_fold(op, acc, x):
        for b in range(x.shape[1] // LANES):
            acc = op(acc, x[:, b * LANES:(b + 1) * LANES])
        return acc

    mx_ref[...] = jnp.full(mx_ref.shape, -jnp.inf, F32)
    ls_ref[...] = jnp.zeros(ls_ref.shape, F32)
    acc_ref[...] = jnp.zeros(acc_ref.shape, F32)

    def step(k, v):
        s = _dot_nt(qs, k)
        reps = s.shape[1] // LANES
        m_old = mx_ref[...]
        cmax = lane_fold(jnp.maximum, s[:, :LANES], s[:, LANES:]) if reps > 1 else s
        m_new = jnp.maximum(m_old, jnp.max(cmax, axis=-1, keepdims=True))
        alpha = jnp.exp2(m_old - m_new)
        p = jnp.exp2(s - jnp.concatenate([m_new] * reps, axis=1))
        mx_ref[...] = m_new
        ls_ref[...] = alpha * ls_ref[...] + lane_fold(jnp.add, p[:, :LANES], p[:, LANES:])
        acc_ref[...] = alpha * acc_ref[...] + _dot(p, v)

    def body(c, carry):
        off = pl.multiple_of(c * kv_chunk, kv_chunk)
        step(k_ref[pl.ds(off, kv_chunk), :], v_ref[pl.ds(off, kv_chunk), :])
        return carry

    lax.fori_loop(0, n_chunks, body, 0)
    if has_ctx:
        step(ck_ref[0], cv_ref[0])
    res = acc_ref[...] / jnp.sum(ls_ref[...], axis=-1, keepdims=True)

    for b in range(ATTN_HEADS // 2):
        g = (2 * b) // ATTN_GROUP
        even = res[2 * b * tq:(2 * b + 1) * tq]
        odd = res[(2 * b + 1) * tq:(2 * b + 2) * tq]
        if g == 0:
            odd = pltpu.roll(odd, HEAD_DIM, 1)
        else:
            even = pltpu.roll(even, HEAD_DIM, 1)
        o_ref[:, b * LANES:(b + 1) * LANES] = jnp.where(low, even, odd)


def _attention(attn, ctx_k, ctx_v, *, batch, seq_len, tq=128):
    rows = attn.shape[0]
    kv_chunk = min(seq_len, 512)
    n_chunks = seq_len // kv_chunk
    qt = seq_len // tq
    stacked = ATTN_HEADS * tq
    has_ctx = ctx_k is not None
    in_specs = [pl.BlockSpec((tq, 512), lambda b, i: (b * qt + i, 0)),
                pl.BlockSpec((seq_len, LANES), lambda b, i: (b, 4)),
                pl.BlockSpec((seq_len, LANES), lambda b, i: (b, 5))]
    args = [attn, attn, attn]
    if has_ctx:
        past = ctx_k.shape[1]
        in_specs += [pl.BlockSpec((1, past, LANES), lambda b, i: (b, 0, 0))] * 2
        args += [ctx_k, ctx_v]
    scratch = [pltpu.VMEM((stacked, LANES), F32)] * 3
    return pl.pallas_call(
        functools.partial(_attn_kernel, tq=tq, kv_chunk=kv_chunk, n_chunks=n_chunks, has_ctx=has_ctx),
        grid=(batch, qt),
        in_specs=in_specs,
        out_specs=pl.BlockSpec((tq, 512), lambda b, i: (b * qt + i, 0)),
        out_shape=jax.ShapeDtypeStruct((rows, 512), F32),
        scratch_shapes=scratch,
        compiler_params=_cparams(2),
        name="attention",
    )(*args)


def _retention_tables():
    C = RET_CHUNK
    h = np.arange(RET_HEADS, dtype=np.float64)
    pos = np.arange(C, dtype=np.float64)
    diff = pos[:, None] - pos[None, :]
    inner, qd, kd, cd = [], [], [], []
    for direction, expo in enumerate((RET_DECAY_EXP_FWD, RET_DECAY_EXP_BWD)):
        lg = np.log1p(-np.exp2(-expo - h))[:, None, None]
        if direction == 0:
            inner.append(np.where(diff >= 0, np.exp(lg * np.maximum(diff, 0.0)), 0.0))
            qd.append(np.exp(lg[:, :, 0] * (pos + 1.0)))
            kd.append(np.exp(lg[:, :, 0] * (C - 1.0 - pos)))
        else:
            inner.append(np.where(diff <= 0, np.exp(lg * np.maximum(-diff, 0.0)), 0.0))
            qd.append(np.exp(lg[:, :, 0] * (C - pos)))
            kd.append(np.exp(lg[:, :, 0] * pos))
        cd.append(np.exp(lg[:, 0, 0] * C))
    inner = np.stack(inner, axis=1)
    rowdec = np.stack([np.stack(qd, 1), np.stack(kd, 1)], axis=2)
    rowdec = np.repeat(rowdec[..., None], RET_DK, axis=-1)
    rowdec = np.concatenate([rowdec[0::2], rowdec[1::2]], axis=-1)
    cd = np.stack(cd, axis=1)
    block = np.kron(np.eye(2), np.ones((RET_DK, RET_DK)))
    per_row = np.repeat(np.stack([cd[0::2], cd[1::2]], axis=-1), RET_DK, axis=-1)
    cd = per_row[:, :, :, None] * block
    return (jnp.asarray(inner, F32), jnp.asarray(rowdec, F32), jnp.asarray(cd, F32))


RET_BLOCK = 8


def _split_bf16(x):
    hi = x.astype(BF16)
    return hi, (x - hi.astype(F32)).astype(BF16)


def _ret_kernel(q_ref, k_ref, v_ref, g_ref, inner_ref, dec_ref, cd_ref, ones_ref, s0_ref, nw_ref,
                o_ref, st_ref, *, n_chunks):
    C = RET_CHUNK
    D = RET_DK
    nb = min(RET_BLOCK, n_chunks)
    lane = lax.broadcasted_iota(jnp.int32, (C, LANES), 1)
    first = lane < D
    ones = ones_ref[...]

    def head_mean(x):
        hi, lo = _split_bf16(x)
        return (jnp.dot(hi, ones, preferred_element_type=F32)
                + jnp.dot(lo, ones, preferred_element_type=F32)) * (1.0 / D)

    def sweep(direction):
        def body(i, S):
            order = [i * nb + j for j in range(nb)]
            if direction == 1:
                order = [n_chunks - 1 - c for c in order]
            starts = [pl.multiple_of(c * C, C) for c in order]
            q = [q_ref[pl.ds(r, C), :] for r in starts]
            k = [k_ref[pl.ds(r, C), :] for r in starts]
            v = [v_ref[pl.ds(r, C), :] for r in starts]
            if direction == 1:
                prev = [o_ref[pl.ds(r, C), :] for r in starts]
                gate = [g_ref[pl.ds(r, C), :] for r in starts]
            att = [jnp.concatenate([_dot_nt(jnp.where(first, q[j], 0.0), k[j]) * inner_ref[0, direction],
                                    _dot_nt(jnp.where(first, 0.0, q[j]), k[j]) * inner_ref[1, direction]],
                                   axis=1) for j in range(nb)]
            kv = [_dot_tn(k[j] * dec_ref[0, direction, 1], v[j]) for j in range(nb)]
            seen = []
            keep = cd_ref[0, direction] != 0.0
            for j in range(nb):
                seen.append(S)
                S = S * cd_ref[0, direction] + jnp.where(keep, kv[j], 0.0)
            v2 = [jnp.concatenate([jnp.where(first, v[j], 0.0), jnp.where(first, 0.0, v[j])], axis=0)
                  for j in range(nb)]
            o = [_dot(att[j], v2[j]) + _dot(q[j] * dec_ref[0, direction, 0], seen[j]) for j in range(nb)]
            if direction == 1:
                o = [o[j] + prev[j] for j in range(nb)]
                mean = [head_mean(o[j]) for j in range(nb)]
                d = [o[j] - mean[j] for j in range(nb)]
                var = [head_mean(d[j] * d[j]) for j in range(nb)]
                o = [d[j] * lax.rsqrt(var[j] + EPS) * nw_ref[...] * _silu(gate[j]) for j in range(nb)]
            for j in range(nb):
                o_ref[pl.ds(starts[j], C), :] = o[j]
            return S

        s0 = s0_ref[0, direction]
        zero = jnp.zeros((D, D), F32)
        S = jnp.concatenate([jnp.concatenate([s0[0], zero], axis=1),
                             jnp.concatenate([zero, s0[1]], axis=1)], axis=0)
        S = lax.fori_loop(0, n_chunks // nb, body, S)
        st_ref[0, direction, 0] = S[:D, :D]
        st_ref[0, direction, 1] = S[D:, D:]

    sweep(0)
    sweep(1)


def _retention(ret, s0, nw, tables, ones_bd, *, batch, seq_len):
    rows = ret.shape[0]
    inner, rowdec, cd = tables
    C = RET_CHUNK
    col = lambda j: pl.BlockSpec((seq_len, LANES), lambda b, hp, j=j: (b, 4 * j + hp))
    st_spec = pl.BlockSpec((1, 2, 2, RET_DK, RET_DK), lambda b, hp: (b, 0, hp, 0, 0))
    return pl.pallas_call(
        functools.partial(_ret_kernel, n_chunks=seq_len // C),
        grid=(batch, RET_HEADS // 2),
        in_specs=[col(0), col(1), col(2), col(3),
                  pl.BlockSpec((2, 2, C, C), lambda b, hp: (hp, 0, 0, 0)),
                  pl.BlockSpec((1, 2, 2, C, LANES), lambda b, hp: (hp, 0, 0, 0, 0)),
                  pl.BlockSpec((1, 2, LANES, LANES), lambda b, hp: (hp, 0, 0, 0)),
                  pl.BlockSpec((LANES, LANES), lambda b, hp: (0, 0)),
                  st_spec,
                  pl.BlockSpec((1, LANES), lambda b, hp: (0, hp))],
        out_specs=[pl.BlockSpec((seq_len, LANES), lambda b, hp: (b, hp)), st_spec],
        out_shape=[jax.ShapeDtypeStruct((rows, 512), F32),
                   jax.ShapeDtypeStruct((batch, 2, RET_HEADS, RET_DK, RET_DK), F32)],
        compiler_params=_cparams(2),
        name="retention",
    )(ret, ret, ret, ret, inner, rowdec, cd, ones_bd, s0, nw)


DN_BLOCK = 8
SOLVE_BASE = 8
CONV_BLOCK = 256
PAD = SUBLANES


def _dn_kernel(q_ref, k_ref, v_ref, g_ref, ab_ref, cw_ref, alog_ref, dtb_ref, nw_ref, s0_ref,
               o_ref, st_ref, pad_ref, qs_ref, ks_ref, vs_ref, of_ref, ob_ref,
               u_ref, w_ref, qe_ref, kd_ref, att_ref, dec_ref, *, seq_len):
    C = DN_CHUNK
    T = seq_len
    n_chunks = T // C
    head = pl.program_id(1)

    zero_rows = jnp.zeros((PAD, LANES), F32)
    blk = min(CONV_BLOCK, T)
    for t, (src, dst) in enumerate(((q_ref, qs_ref), (k_ref, ks_ref), (v_ref, vs_ref))):
        pad_ref[0:PAD, :] = zero_rows
        pad_ref[PAD + T:PAD + T + PAD, :] = zero_rows
        pad_ref[PAD:PAD + T, :] = src[...]
        w = cw_ref[t, 0]

        def conv(i, carry, dst=dst, w=w, t=t):
            r0 = pl.multiple_of(i * blk, blk)
            xb = pad_ref[pl.ds(r0, blk + 2 * PAD), :]
            prev = pltpu.roll(xb, 1, 0)[PAD:PAD + blk]
            nxt = pltpu.roll(xb, blk + 2 * PAD - 1, 0)[PAD:PAD + blk]
            y = _silu(w[0:1] * prev + w[1:2] * xb[PAD:PAD + blk] + w[2:3] * nxt)
            if t < 2:
                y = y * lax.rsqrt(jnp.sum(y * y, axis=-1, keepdims=True) + EPS)
            if t == 0:
                y = y * (DN_DK ** -0.5)
            dst[pl.ds(r0, blk), :] = y
            return carry

        lax.fori_loop(0, T // blk, conv, 0)

    ri = lax.broadcasted_iota(jnp.int32, (C, C), 0)
    ci = lax.broadcasted_iota(jnp.int32, (C, C), 1)
    eye = ri == ci
    eye_f = eye.astype(F32)
    lane = lax.broadcasted_iota(jnp.int32, (1, LANES), 1)
    masks = ((ri >= ci, ri > ci), (ri <= ci, ri < ci))

    def load(r0):
        return tuple(ref[pl.ds(r0, C), :] for ref in (ab_ref, qs_ref, ks_ref, vs_ref))

    def prep(operands, direction):
        ab, q, k, v = operands
        incl, strict = masks[direction]
        sel_a = (lane == direction * DN_HEADS + head).astype(F32)
        sel_b = (lane == 2 * DN_HEADS + direction * DN_HEADS + head).astype(F32)
        da = jnp.sum(ab * sel_a, axis=-1, keepdims=True)
        db = jnp.sum(ab * sel_b, axis=-1, keepdims=True)
        z = da + dtb_ref[direction, 0]
        softplus = jnp.maximum(z, 0.0) + jnp.log1p(jnp.exp(-jnp.abs(z)))
        g = -jnp.exp(alog_ref[direction, 0]) * softplus
        beta = _sigmoid(db)
        kb = k * beta
        G = jnp.dot(incl.astype(F32), g, precision=lax.Precision.HIGHEST,
                    preferred_element_type=F32)
        kk = _dot_nt(kb, k)
        qk = _dot_nt(q, k)
        yield
        Gc = G[:, :C]
        Grow = jnp.sum(jnp.where(eye, Gc, 0.0), axis=0, keepdims=True)
        L = jnp.where(incl, jnp.exp(jnp.where(incl, Gc - Grow, 0.0)), 0.0)
        N = jnp.where(strict, -(kk * L), 0.0)
        same = (ri // SOLVE_BASE) == (ci // SOLVE_BASE)
        P = jnp.where(same, N, 0.0)
        Tm = eye_f + P
        P = _dot(P, P)
        yield
        for _ in range(int(math.log2(SOLVE_BASE)) - 2):
            Tm, P = Tm + _dot(Tm, P), _dot(P, P)
            yield
        Tm = Tm + _dot(Tm, P)
        yield
        size = SOLVE_BASE
        while size < C:
            size *= 2
            wider = (ri // size) == (ci // size)
            X = jnp.where(jnp.logical_and(wider, jnp.logical_not(same)), N, 0.0)
            TX = _dot(Tm, X)
            yield
            Tm = Tm + _dot(TX, Tm)
            yield
            same = wider
        eG = jnp.exp(G)
        g_last = G[C - 1:C] if direction == 0 else G[0:1]
        return (_dot(Tm, v * beta), _dot(Tm, kb * eG).astype(BF16), (q * eG).astype(BF16),
                (k * jnp.exp(g_last - G)).astype(BF16), (qk * L).astype(BF16),
                jnp.broadcast_to(jnp.exp(g_last), (SUBLANES, LANES)))

    def run_staged(generators):
        results = [None] * len(generators)
        live = list(enumerate(generators))
        while live:
            still = []
            for idx, gen in live:
                try:
                    next(gen)
                    still.append((idx, gen))
                except StopIteration as done:
                    results[idx] = done.value
            live = still
        return results

    slots = (u_ref, w_ref, qe_ref, kd_ref, att_ref, dec_ref)

    nb = min(DN_BLOCK, n_chunks)
    n_blocks = n_chunks // nb

    def block_rows(i):
        return ([pl.multiple_of((i * nb + j) * C, C) for j in range(nb)]
                + [pl.multiple_of((n_chunks - 1 - (i * nb + j)) * C, C) for j in range(nb)])

    def prep_block(i):
        return [prep(load(r), s // nb) for s, r in enumerate(block_rows(i))]

    def store_block(prepared, base):
        for s in range(2 * nb):
            for ref, val in zip(slots, prepared[s]):
                ref[base + s] = val

    def recurrence(Sf, Sb, base):
        outs = []
        for j in range(nb):
            sf, sb = base + j, base + nb + j
            Sf16, Sb16 = Sf.astype(BF16), Sb.astype(BF16)
            wf, qf = _dot(w_ref[sf], Sf16), _dot(qe_ref[sf], Sf16)
            wb, qb = _dot(w_ref[sb], Sb16), _dot(qe_ref[sb], Sb16)
            yield
            vf = (u_ref[sf] - wf).astype(BF16)
            vb = (u_ref[sb] - wb).astype(BF16)
            outs.append((qf + _dot(att_ref[sf], vf), qb + _dot(att_ref[sb], vb)))
            Sf = Sf * dec_ref[sf, 0:1] + _dot_tn(kd_ref[sf], vf)
            Sb = Sb * dec_ref[sb, 0:1] + _dot_tn(kd_ref[sb], vb)
            yield
        return outs, Sf, Sb

    def store_outputs(i, outs):
        rows = block_rows(i)
        for j in range(nb):
            of_ref[pl.ds(rows[j], C), :] = outs[j][0]
            ob_ref[pl.ds(rows[nb + j], C), :] = outs[j][1]

    state = (s0_ref[0, 0, 0], s0_ref[0, 1, 0])
    if n_blocks == 1:
        store_block(run_staged(prep_block(0)), 0)
        (outs, Sf, Sb), = run_staged([recurrence(*state, 0)])
        store_outputs(0, outs)
    else:
        store_block(run_staged(prep_block(0)), 0)

        def body(i, carry):
            cur = (i % 2) * (2 * nb)
            nxt = 2 * nb - cur
            ahead = jnp.minimum(i + 1, n_blocks - 1)
            *prepared, (outs, Sf, Sb) = run_staged(prep_block(ahead) + [recurrence(*carry, cur)])
            store_outputs(i, outs)
            store_block(prepared, nxt)
            return Sf, Sb

        Sf, Sb = lax.fori_loop(0, n_blocks, body, state)
    st_ref[0, 0, 0] = Sf
    st_ref[0, 1, 0] = Sb

    def fin(i, carry):
        r0 = pl.multiple_of(i * blk, blk)
        o = of_ref[pl.ds(r0, blk), :] + ob_ref[pl.ds(r0, blk), :]
        y = o * lax.rsqrt(jnp.mean(o * o, axis=-1, keepdims=True) + EPS) * nw_ref[...]
        o_ref[pl.ds(r0, blk), :] = y * _silu(g_ref[pl.ds(r0, blk), :])
        return carry

    lax.fori_loop(0, T // blk, fin, 0)


def _deltanet(dn, ab, conv_w, alog, dtb, nw, s0, *, batch, seq_len):
    rows = dn.shape[0]
    T = seq_len
    C = DN_CHUNK
    nb = min(DN_BLOCK, T // C)
    col = lambda j: pl.BlockSpec((T, LANES), lambda b, h, j=j: (b, 4 * j + h))
    st_spec = pl.BlockSpec((1, 2, 1, DN_DK, DN_DK), lambda b, h: (b, 0, h, 0, 0))
    gate_spec = pl.BlockSpec((2, 1, 1, LANES), lambda b, h: (0, h, 0, 0))
    return pl.pallas_call(
        functools.partial(_dn_kernel, seq_len=T),
        grid=(batch, DN_HEADS),
        in_specs=[col(0), col(1), col(2), col(3),
                  pl.BlockSpec((T, LANES), lambda b, h: (b, 0)),
                  pl.BlockSpec((3, 1, 3, LANES), lambda b, h: (0, h, 0, 0)),
                  gate_spec, gate_spec,
                  pl.BlockSpec((1, LANES), lambda b, h: (0, 0)),
                  st_spec],
        out_specs=[pl.BlockSpec((T, LANES), lambda b, h: (b, h)), st_spec],
        out_shape=[jax.ShapeDtypeStruct((rows, 512), F32),
                   jax.ShapeDtypeStruct((batch, 2, DN_HEADS, DN_DK, DN_DK), F32)],
        scratch_shapes=([pltpu.VMEM((T + 2 * PAD, LANES), F32)] + [pltpu.VMEM((T, LANES), F32)] * 5
                        + [pltpu.VMEM((4 * nb, C, LANES), F32)] + [pltpu.VMEM((4 * nb, C, LANES), BF16)] * 3
                        + [pltpu.VMEM((4 * nb, C, C), BF16), pltpu.VMEM((4 * nb, SUBLANES, LANES), F32)]),
        compiler_params=_cparams(2),
        name="deltanet",
    )(dn, dn, dn, dn, ab, conv_w, alog, dtb, nw, s0)


def _merge_kernel(x_ref, mod_ref, a_ref, r_ref, d_ref, mg_ref, wbr_ref, wout_ref, o_ref):
    d = x_ref.shape[-1]
    merged = jnp.zeros(x_ref.shape, F32)
    for i, br in enumerate((a_ref, r_ref, d_ref)):
        merged = merged + _sigmoid(mg_ref[:, i * d:(i + 1) * d]) * _dot(br[...], wbr_ref[i])
    out = _dot(merged, wout_ref[...])
    o_ref[...] = x_ref[...] + mod_ref[0][5:6] * out


def _merge(x, mod, a, r, dn, mg, w_br, w_out, *, rows_per_cond, tm=512):
    rows, d = x.shape
    tiles_per_cond = rows_per_cond // tm
    row_spec = lambda wd: pl.BlockSpec((tm, wd), lambda i: (i, 0))
    return pl.pallas_call(
        _merge_kernel,
        grid=(rows // tm,),
        in_specs=[row_spec(d),
                  pl.BlockSpec((1, N_MOD, d), lambda i: (i // tiles_per_cond, 0, 0)),
                  row_spec(512), row_spec(512), row_spec(512), row_spec(MG_W),
                  _resident(w_br.shape), _resident(w_out.shape)],
        out_specs=row_spec(d),
        out_shape=jax.ShapeDtypeStruct((rows, d), F32),
        compiler_params=_cparams(1),
        name="merge",
    )(x, mod, a, r, dn, mg, w_br, w_out)


def _rope_tables(seq_len):
    n_freq = HEAD_DIM // 4
    inv = ROPE_THETA ** (-jnp.arange(n_freq, dtype=F32) / n_freq)
    t = jnp.arange(seq_len)
    row = (t // GRID_W).astype(F32)
    colp = (t % GRID_W).astype(F32)
    ang = jnp.concatenate([row[:, None] * inv, colp[:, None] * inv], axis=-1)
    c, s = jnp.cos(ang), jnp.sin(ang)
    cos = jnp.concatenate([c, c, c, c], axis=-1)
    sin = jnp.concatenate([-s, s, -s, s], axis=-1)
    return cos, sin


def _reorder_w_in(w):
    d = w.shape[0]
    o_da = 768 + 2048 + 1536
    o_dg = o_da + 4 * DN_HEADS
    o_mg = o_dg + 512
    return jnp.concatenate([w[:, :o_da], w[:, o_dg:o_mg], w[:, o_da:o_dg],
                            jnp.zeros((d, AB_W - 4 * DN_HEADS), w.dtype), w[:, o_mg:]], axis=1)


def kernel(x_prompt, x_sample, cache_k, cache_v, state_ret, state_delta, c, c_ctx,
           w_mod, b_mod, norm_ffn1, ffn1_w_in, ffn1_w_out, norm_mix, w_in,
           attn_q_norm, attn_k_norm, ret_norm, dn_conv, dn_a_log, dn_dt_bias, dn_norm,
           w_br_attn, w_br_ret, w_br_dn, w_out, norm_ffn2, ffn2_w_in, ffn2_w_out, norm_final):
    bp, tp, d = x_prompt.shape
    bs, ts, _ = x_sample.shape
    depth = w_mod.shape[0]
    past = cache_k.shape[2]

    conds = jnp.concatenate([c_ctx[None, :], c], axis=0)
    mod = _modulation(conds, w_mod, b_mod).reshape(depth, 1 + bs, N_MOD, d)

    ones_bd = jnp.asarray(np.kron(np.eye(ATTN_HEADS), np.ones((HEAD_DIM, HEAD_DIM))), BF16)
    ret_tabs = _retention_tables()
    rope_tabs = _rope_tables(ts)
    ret_zero = jnp.zeros((bp, 2, RET_HEADS, RET_DK, RET_DK), F32)
    dn_zero = jnp.zeros((bp, 2, DN_HEADS, DN_DK, DN_DK), F32)
    fnw = norm_final.reshape(1, d)

    groups = {
        "prompt": dict(x=x_prompt.reshape(bp * tp, d), batch=bp, seq=tp, rows_per_cond=bp * tp, rope=None),
        "sample": dict(x=x_sample.reshape(bs * ts, d), batch=bs, seq=ts, rows_per_cond=ts, rope=rope_tabs),
    }
    new_k, new_v, new_rs, new_ds = [], [], [], []
    for l in range(depth):
        w1_in, w1_out = ffn1_w_in[l].astype(BF16), ffn1_w_out[l].astype(BF16)
        w2_in, w2_out = ffn2_w_in[l].astype(BF16), ffn2_w_out[l].astype(BF16)
        w_proj = _reorder_w_in(w_in[l]).astype(BF16)
        w_br = jnp.stack([w_br_attn[l], w_br_ret[l], w_br_dn[l]]).astype(BF16)
        w_o = w_out[l].astype(BF16)
        gq = jnp.tile(attn_q_norm[l], ATTN_HEADS).reshape(1, 512)
        gk = jnp.tile(attn_k_norm[l], ATTN_KV_HEADS).reshape(1, LANES)
        conv_w = dn_conv[l].reshape(3, 3, DN_HEADS, LANES).transpose(1, 2, 0, 3)
        alog = jnp.broadcast_to(dn_a_log[l][:, :, None, None], (2, DN_HEADS, 1, LANES))
        dtb = jnp.broadcast_to(dn_dt_bias[l][:, :, None, None], (2, DN_HEADS, 1, LANES))
        for name, grp in groups.items():
            is_prompt = name == "prompt"
            x = grp["x"]
            gmod = mod[l, :1] if is_prompt else mod[l, 1:]
            rpc = grp["rows_per_cond"]
            x = _ffn(x, gmod, norm_ffn1[l].reshape(1, d), w1_in, w1_out, fnw,
                     mod_base=0, rows_per_cond=rpc, final=False)
            attn, ret, dn, ab, mg = _inproj(x, gmod, norm_mix[l].reshape(1, d), w_proj, ones_bd, gq, gk,
                                            grp["rope"], rows_per_cond=rpc, seq_len=grp["seq"])
            if is_prompt:
                a_out = _attention(attn, None, None, batch=bp, seq_len=tp)
                rs0, ds0 = ret_zero, dn_zero
            else:
                a_out = _attention(attn, cache_k[:, l].reshape(bs, past, LANES),
                                   cache_v[:, l].reshape(bs, past, LANES), batch=bs, seq_len=ts)
                rs0, ds0 = state_ret[:, l], state_delta[:, l]
            r_out, rs = _retention(ret, rs0, ret_norm[l].reshape(1, 512), ret_tabs, ones_bd,
                                   batch=grp["batch"], seq_len=grp["seq"])
            d_out, ds = _deltanet(dn, ab, conv_w, alog, dtb, dn_norm[l].reshape(1, LANES), ds0,
                                  batch=grp["batch"], seq_len=grp["seq"])
            x = _merge(x, gmod, a_out, r_out, d_out, mg, w_br, w_o, rows_per_cond=rpc)
            x = _ffn(x, gmod, norm_ffn2[l].reshape(1, d), w2_in, w2_out, fnw,
                     mod_base=6, rows_per_cond=rpc, final=(l == depth - 1))
            grp["x"] = x
            if is_prompt:
                new_k.append(attn[:, 512:640].reshape(bp, tp, ATTN_KV_HEADS, HEAD_DIM))
                new_v.append(attn[:, 640:768].reshape(bp, tp, ATTN_KV_HEADS, HEAD_DIM))
                new_rs.append(rs)
                new_ds.append(ds)

    y_prompt = groups["prompt"]["x"].reshape(bp, tp, d)
    y_sample = groups["sample"]["x"].reshape(bs, ts, d)
    return (y_prompt, y_sample, jnp.stack(new_k, axis=1), jnp.stack(new_v, axis=1),
            jnp.stack(new_rs, axis=1), jnp.stack(new_ds, axis=1))
```

```python
import functools
import math

import numpy as np
import jax
import jax.numpy as jnp
from jax import lax
from jax.experimental import pallas as pl
from jax.experimental.pallas import tpu as pltpu

F32 = jnp.float32
BF16 = jnp.bfloat16

EPS = 1e-6
ROPE_THETA = 10000.0
GRID_W = 64
N_MOD = 9

ATTN_HEADS = 8
ATTN_KV_HEADS = 2
HEAD_DIM = 64
RET_HEADS = 8
RET_DK = 64
RET_CHUNK = 128
RET_DECAY_EXP_FWD = 5.0
RET_DECAY_EXP_BWD = 5.5
DN_HEADS = 4
DN_DK = 128
DN_CHUNK = 64
N_BRANCH = 3

LANES = 128
SUBLANES = 8
VMEM_LIMIT = 56 * 1024 * 1024

ATTN_W = 768
RET_W = 2048
DN_W = 2048
AB_W = 128
MG_W = 3072
IN_W = ATTN_W + RET_W + DN_W + AB_W + MG_W


def _cparams(n_axes):
    return pltpu.CompilerParams(dimension_semantics=("parallel",) * n_axes,
                                vmem_limit_bytes=VMEM_LIMIT)


def _resident(shape):
    zeros = (0,) * len(shape)
    return pl.BlockSpec(shape, lambda *_: zeros, pipeline_mode=pl.Buffered(1))


def _dot(a, b):
    return jnp.dot(a.astype(BF16), b.astype(BF16), preferred_element_type=F32)


def _dot_nt(a, b):
    return lax.dot_general(a.astype(BF16), b.astype(BF16), (((1,), (1,)), ((), ())),
                           preferred_element_type=F32)


def _dot_tn(a, b):
    return lax.dot_general(a.astype(BF16), b.astype(BF16), (((0,), (0,)), ((), ())),
                           preferred_element_type=F32)


def _sigmoid(x):
    return 1.0 / (1.0 + jnp.exp(-x))


def _silu(x):
    return x * _sigmoid(x)


def _norm_mod(x, nw, shift, scale):
    y = x * lax.rsqrt(jnp.mean(x * x, axis=-1, keepdims=True) + EPS) * nw
    return y * (1.0 + scale) + shift


def _mod_kernel(c_ref, w_ref, b_ref, o_ref):
    o_ref[0] = _dot(_silu(c_ref[...]), w_ref[0]) + b_ref[0]


def _modulation(conds, w_mod, b_mod):
    depth, d, n = w_mod.shape
    nc = conds.shape[0]
    tn = n // N_MOD
    return pl.pallas_call(
        _mod_kernel,
        grid=(depth, n // tn),
        in_specs=[pl.BlockSpec((nc, d), lambda l, j: (0, 0)),
                  pl.BlockSpec((1, d, tn), lambda l, j: (l, 0, j)),
                  pl.BlockSpec((1, 1, tn), lambda l, j: (l, 0, j))],
        out_specs=pl.BlockSpec((1, nc, tn), lambda l, j: (l, 0, j)),
        out_shape=jax.ShapeDtypeStruct((depth, nc, n), F32),
        compiler_params=_cparams(2),
        name="modulation",
    )(conds, w_mod, b_mod.reshape(depth, 1, n))


FFN_CHUNK = 256


def _ffn_kernel(x_ref, mod_ref, nw_ref, win_ref, wout_ref, fnw_ref, o_ref, *, mod_base, dff, final):
    x = x_ref[...]
    m = mod_ref[0]
    shift, scale, gate = (m[mod_base + i:mod_base + i + 1] for i in range(3))
    h = _norm_mod(x, nw_ref[...], shift, scale).astype(BF16)
    acc = jnp.zeros(x.shape, F32)
    for c in range(dff // FFN_CHUNK):
        lo = c * FFN_CHUNK
        hg = jnp.dot(h, win_ref[:, lo:lo + FFN_CHUNK], preferred_element_type=F32)
        hu = jnp.dot(h, win_ref[:, dff + lo:dff + lo + FFN_CHUNK], preferred_element_type=F32)
        a = (_silu(hg) * hu).astype(BF16)
        acc = acc + jnp.dot(a, wout_ref[lo:lo + FFN_CHUNK, :], preferred_element_type=F32)
    y = x + 0.5 * gate * acc
    if final:
        y = y * lax.rsqrt(jnp.mean(y * y, axis=-1, keepdims=True) + EPS) * fnw_ref[...]
    o_ref[...] = y


def _ffn(x, mod, nw, w_in, w_out, fnw, *, mod_base, rows_per_cond, final, tm=512):
    rows, d = x.shape
    dff = w_out.shape[0]
    tiles_per_cond = rows_per_cond // tm
    return pl.pallas_call(
        functools.partial(_ffn_kernel, mod_base=mod_base, dff=dff, final=final),
        grid=(rows // tm,),
        in_specs=[pl.BlockSpec((tm, d), lambda i: (i, 0)),
                  pl.BlockSpec((1, N_MOD, d), lambda i: (i // tiles_per_cond, 0, 0)),
                  _resident((1, d)),
                  _resident(w_in.shape),
                  _resident(w_out.shape),
                  _resident((1, d))],
        out_specs=pl.BlockSpec((tm, d), lambda i: (i, 0)),
        out_shape=jax.ShapeDtypeStruct((rows, d), F32),
        compiler_params=_cparams(1),
        name="ffn",
    )(x, mod, nw, w_in, w_out, fnw)


def _swap_halves(x):
    n = x.shape[-1]
    lane = lax.broadcasted_iota(jnp.int32, x.shape, 1)
    first = (lane % HEAD_DIM) < (HEAD_DIM // 2)
    return jnp.where(first, pltpu.roll(x, n - HEAD_DIM // 2, 1), pltpu.roll(x, HEAD_DIM // 2, 1))


def _rope(x, cos, sin):
    reps = x.shape[-1] // LANES
    c = jnp.concatenate([cos] * reps, axis=1) if reps > 1 else cos
    s = jnp.concatenate([sin] * reps, axis=1) if reps > 1 else sin
    return x * c + _swap_halves(x) * s


def _head_rms(x, ones_bd, gain):
    sq = x * x
    hi = sq.astype(BF16)
    lo = (sq - hi.astype(F32)).astype(BF16)
    ss = (jnp.dot(hi, ones_bd, preferred_element_type=F32)
          + jnp.dot(lo, ones_bd, preferred_element_type=F32))
    return x * lax.rsqrt(ss * (1.0 / HEAD_DIM) + EPS) * gain


def _inproj_kernel(*refs, rope):
    if rope:
        (x_ref, mod_ref, nw_ref, w_ref, ones_ref, gq_ref, gk_ref, cos_ref, sin_ref,
         attn_ref, ret_ref, dn_ref, ab_ref, mg_ref) = refs
    else:
        (x_ref, mod_ref, nw_ref, w_ref, ones_ref, gq_ref, gk_ref,
         attn_ref, ret_ref, dn_ref, ab_ref, mg_ref) = refs
    m = mod_ref[0]
    h = _norm_mod(x_ref[...], nw_ref[...], m[3:4], m[4:5]).astype(BF16)

    def proj(lo, width):
        return jnp.dot(h, w_ref[:, lo:lo + width], preferred_element_type=F32)

    def rot(v):
        return _rope(v, cos_ref[...], sin_ref[...]) if rope else v

    a = proj(0, ATTN_W)
    q = _head_rms(a[:, :512], ones_ref[...], gq_ref[...])
    k = _head_rms(a[:, 512:640], ones_ref[:LANES, :LANES], gk_ref[...])
    attn_ref[:, :512] = rot(q) * (HEAD_DIM ** -0.5)
    attn_ref[:, 512:640] = rot(k)
    attn_ref[:, 640:768] = a[:, 640:768]

    r = proj(ATTN_W, RET_W)
    ret_ref[:, :512] = rot(r[:, :512]) * (RET_DK ** -0.5)
    ret_ref[:, 512:1024] = rot(r[:, 512:1024])
    ret_ref[:, 1024:] = r[:, 1024:]

    dn_ref[...] = proj(ATTN_W + RET_W, DN_W)
    ab_ref[...] = proj(ATTN_W + RET_W + DN_W, AB_W)
    mg_ref[...] = proj(ATTN_W + RET_W + DN_W + AB_W, MG_W)


def _inproj(x, mod, nw, w, ones_bd, gq, gk, rope_tabs, *, rows_per_cond, seq_len, tm=256):
    rows, d = x.shape
    tiles_per_cond = rows_per_cond // tm
    tiles_per_seq = seq_len // tm
    rope = rope_tabs is not None
    in_specs = [pl.BlockSpec((tm, d), lambda i: (i, 0)),
                pl.BlockSpec((1, N_MOD, d), lambda i: (i // tiles_per_cond, 0, 0)),
                _resident((1, d)),
                _resident(w.shape),
                _resident(ones_bd.shape),
                _resident(gq.shape),
                _resident(gk.shape)]
    args = [x, mod, nw, w, ones_bd, gq, gk]
    if rope:
        in_specs += [pl.BlockSpec((tm, LANES), lambda i: (i % tiles_per_seq, 0))] * 2
        args += list(rope_tabs)
    widths = (ATTN_W, RET_W, DN_W, AB_W, MG_W)
    return pl.pallas_call(
        functools.partial(_inproj_kernel, rope=rope),
        grid=(rows // tm,),
        in_specs=in_specs,
        out_specs=[pl.BlockSpec((tm, wd), lambda i: (i, 0)) for wd in widths],
        out_shape=[jax.ShapeDtypeStruct((rows, wd), F32) for wd in widths],
        compiler_params=_cparams(1),
        name="inproj",
    )(*args)


ATTN_GROUP = ATTN_HEADS // ATTN_KV_HEADS
LOG2E = math.log2(math.e)
ATTN_CHUNK_GROUP = 8
ATTN_ROW_BLOCKS = 2


def _attn_kernel(*refs, tq, kv_chunk, n_chunks, has_ctx):
    if has_ctx:
        q_ref, k_ref, v_ref, ck_ref, cv_ref, o_ref, mx_ref, ls_ref, acc_ref, qs_ref = refs
    else:
        q_ref, k_ref, v_ref, o_ref, mx_ref, ls_ref, acc_ref, qs_ref = refs
    lane = lax.broadcasted_iota(jnp.int32, (tq, LANES), 1)
    low = lane < HEAD_DIM
    parts = []
    for hd in range(ATTN_HEADS):
        g = hd // ATTN_GROUP
        blk = q_ref[:, (hd // 2) * LANES:(hd // 2 + 1) * LANES] * LOG2E
        if hd % 2 != g:
            blk = pltpu.roll(blk, HEAD_DIM, 1)
        parts.append(jnp.where(low if g == 0 else jnp.logical_not(low), blk, 0.0))
    qs_ref[...] = jnp.concatenate(parts, axis=0).astype(BF16)

    def lane_fold(op, acc, x):
        for b in range(x.shape[1] // LANES):
            acc = op(acc, x[:, b * LANES:(b + 1) * LANES])
        return acc

    mx_ref[...] = jnp.full(mx_ref.shape, -jnp.inf, F32)
    ls_ref[...] = jnp.zeros(ls_ref.shape, F32)
    acc_ref[...] = jnp.zeros(acc_ref.shape, F32)
    rbs = qs_ref.shape[0] // ATTN_ROW_BLOCKS
    blocks = [slice(r * rbs, (r + 1) * rbs) for r in range(ATTN_ROW_BLOCKS)]

    def scores(sl, k):
        return _dot_nt(qs_ref[sl], k)

    def update(sl, s, v):
        reps = s.shape[1] // LANES
        m_old = mx_ref[sl]
        cmax = lane_fold(jnp.maximum, s[:, :LANES], s[:, LANES:])
        m_new = jnp.maximum(m_old, jnp.max(cmax, axis=-1, keepdims=True))
        alpha = jnp.exp2(m_old - m_new)
        p = jnp.exp2(s - jnp.concatenate([m_new] * reps, axis=1))
        mx_ref[sl] = m_new
        ls_ref[sl] = alpha * ls_ref[sl] + lane_fold(jnp.add, p[:, :LANES], p[:, LANES:])
        acc_ref[sl] = alpha * acc_ref[sl] + _dot(p, v)

    def chunks(kvs):
        units = [(sl, k.astype(BF16), v.astype(BF16)) for k, v in kvs for sl in blocks]
        ahead = None
        for sl, k, v in units + [(None, None, None)]:
            nxt = scores(sl, k) if sl is not None else None
            if ahead is not None:
                update(*ahead)
            ahead = (sl, nxt, v)

    group = min(ATTN_CHUNK_GROUP, n_chunks)

    def body(i, carry):
        kvs = []
        for j in range(group):
            off = pl.multiple_of((i * group + j) * kv_chunk, kv_chunk)
            kvs.append((k_ref[pl.ds(off, kv_chunk), :], v_ref[pl.ds(off, kv_chunk), :]))
        chunks(kvs)
        return carry

    tail = [(ck_ref[0], cv_ref[0])] if has_ctx else []
    if group == n_chunks:
        chunks([(k_ref[c * kv_chunk:(c + 1) * kv_chunk, :], v_ref[c * kv_chunk:(c + 1) * kv_chunk, :])
                for c in range(n_chunks)] + tail)
    else:
        lax.fori_loop(0, n_chunks // group, body, 0)
        if tail:
            chunks(tail)
    res = acc_ref[...] / jnp.sum(ls_ref[...], axis=-1, keepdims=True)

    for b in range(ATTN_HEADS // 2):
        g = (2 * b) // ATTN_GROUP
        even = res[2 * b * tq:(2 * b + 1) * tq]
        odd = res[(2 * b + 1) * tq:(2 * b + 2) * tq]
        if g == 0:
            odd = pltpu.roll(odd, HEAD_DIM, 1)
        else:
            even = pltpu.roll(even, HEAD_DIM, 1)
        o_ref[:, b * LANES:(b + 1) * LANES] = jnp.where(low, even, odd)


def _attention(attn, ctx_k, ctx_v, *, batch, seq_len, tq=128):
    rows = attn.shape[0]
    kv_chunk = min(seq_len, 512)
    n_chunks = seq_len // kv_chunk
    qt = seq_len // tq
    stacked = ATTN_HEADS * tq
    has_ctx = ctx_k is not None
    in_specs = [pl.BlockSpec((tq, 512), lambda b, i: (b * qt + i, 0)),
                pl.BlockSpec((seq_len, LANES), lambda b, i: (b, 4)),
                pl.BlockSpec((seq_len, LANES), lambda b, i: (b, 5))]
    args = [attn, attn, attn]
    if has_ctx:
        past = ctx_k.shape[1]
        in_specs += [pl.BlockSpec((1, past, LANES), lambda b, i: (b, 0, 0))] * 2
        args += [ctx_k, ctx_v]
    scratch = [pltpu.VMEM((stacked, LANES), F32)] * 3 + [pltpu.VMEM((stacked, LANES), BF16)]
    return pl.pallas_call(
        functools.partial(_attn_kernel, tq=tq, kv_chunk=kv_chunk, n_chunks=n_chunks, has_ctx=has_ctx),
        grid=(batch, qt),
        in_specs=in_specs,
        out_specs=pl.BlockSpec((tq, 512), lambda b, i: (b * qt + i, 0)),
        out_shape=jax.ShapeDtypeStruct((rows, 512), F32),
        scratch_shapes=scratch,
        compiler_params=_cparams(2),
        name="attention",
    )(*args)


def _retention_tables():
    C = RET_CHUNK
    h = np.arange(RET_HEADS, dtype=np.float64)
    pos = np.arange(C, dtype=np.float64)
    diff = pos[:, None] - pos[None, :]
    inner, qd, kd, cd = [], [], [], []
    for direction, expo in enumerate((RET_DECAY_EXP_FWD, RET_DECAY_EXP_BWD)):
        lg = np.log1p(-np.exp2(-expo - h))[:, None, None]
        if direction == 0:
            inner.append(np.where(diff >= 0, np.exp(lg * np.maximum(diff, 0.0)), 0.0))
            qd.append(np.exp(lg[:, :, 0] * (pos + 1.0)))
            kd.append(np.exp(lg[:, :, 0] * (C - 1.0 - pos)))
        else:
            inner.append(np.where(diff <= 0, np.exp(lg * np.maximum(-diff, 0.0)), 0.0))
            qd.append(np.exp(lg[:, :, 0] * (C - pos)))
            kd.append(np.exp(lg[:, :, 0] * pos))
        cd.append(np.exp(lg[:, 0, 0] * C))
    inner = np.stack(inner, axis=1)
    rowdec = np.stack([np.stack(qd, 1), np.stack(kd, 1)], axis=2)
    rowdec = np.repeat(rowdec[..., None], RET_DK, axis=-1)
    rowdec = np.concatenate([rowdec[0::2], rowdec[1::2]], axis=-1)
    cd = np.stack(cd, axis=1)
    block = np.kron(np.eye(2), np.ones((RET_DK, RET_DK)))
    per_row = np.repeat(np.stack([cd[0::2], cd[1::2]], axis=-1), RET_DK, axis=-1)
    cd = per_row[:, :, :, None] * block
    return (jnp.asarray(inner, F32), jnp.asarray(rowdec, F32), jnp.asarray(cd, F32))


RET_BLOCK = 8


def _split_bf16(x):
    hi = x.astype(BF16)
    return hi, (x - hi.astype(F32)).astype(BF16)


def _ret_kernel(q_ref, k_ref, v_ref, g_ref, inner_ref, dec_ref, cd_ref, ones_ref, s0_ref, nw_ref,
                o_ref, st_ref, *, n_chunks):
    C = RET_CHUNK
    D = RET_DK
    nb = min(RET_BLOCK, n_chunks)
    lane = lax.broadcasted_iota(jnp.int32, (C, LANES), 1)
    first = lane < D
    ones = ones_ref[...]

    def head_mean(x):
        hi, lo = _split_bf16(x)
        return (jnp.dot(hi, ones, preferred_element_type=F32)
                + jnp.dot(lo, ones, preferred_element_type=F32)) * (1.0 / D)

    def sweep(direction):
        def body(i, S):
            order = [i * nb + j for j in range(nb)]
            if direction == 1:
                order = [n_chunks - 1 - c for c in order]
            starts = [pl.multiple_of(c * C, C) for c in order]
            q = [q_ref[pl.ds(r, C), :] for r in starts]
            k = [k_ref[pl.ds(r, C), :] for r in starts]
            v = [v_ref[pl.ds(r, C), :] for r in starts]
            if direction == 1:
                prev = [o_ref[pl.ds(r, C), :] for r in starts]
                gate = [g_ref[pl.ds(r, C), :] for r in starts]
            att = [jnp.concatenate([_dot_nt(jnp.where(first, q[j], 0.0), k[j]) * inner_ref[0, direction],
                                    _dot_nt(jnp.where(first, 0.0, q[j]), k[j]) * inner_ref[1, direction]],
                                   axis=1) for j in range(nb)]
            kv = [_dot_tn(k[j] * dec_ref[0, direction, 1], v[j]) for j in range(nb)]
            seen = []
            keep = cd_ref[0, direction] != 0.0
            for j in range(nb):
                seen.append(S)
                S = S * cd_ref[0, direction] + jnp.where(keep, kv[j], 0.0)
            v2 = [jnp.concatenate([jnp.where(first, v[j], 0.0), jnp.where(first, 0.0, v[j])], axis=0)
                  for j in range(nb)]
            o = [_dot(att[j], v2[j]) + _dot(q[j] * dec_ref[0, direction, 0], seen[j]) for j in range(nb)]
            if direction == 1:
                o = [o[j] + prev[j] for j in range(nb)]
                mean = [head_mean(o[j]) for j in range(nb)]
                d = [o[j] - mean[j] for j in range(nb)]
                var = [head_mean(d[j] * d[j]) for j in range(nb)]
                o = [d[j] * lax.rsqrt(var[j] + EPS) * nw_ref[...] * _silu(gate[j]) for j in range(nb)]
            for j in range(nb):
                o_ref[pl.ds(starts[j], C), :] = o[j]
            return S

        s0 = s0_ref[0, direction]
        zero = jnp.zeros((D, D), F32)
        S = jnp.concatenate([jnp.concatenate([s0[0], zero], axis=1),
                             jnp.concatenate([zero, s0[1]], axis=1)], axis=0)
        S = lax.fori_loop(0, n_chunks // nb, body, S)
        st_ref[0, direction, 0] = S[:D, :D]
        st_ref[0, direction, 1] = S[D:, D:]

    sweep(0)
    sweep(1)


def _retention(ret, s0, nw, tables, ones_bd, *, batch, seq_len):
    rows = ret.shape[0]
    inner, rowdec, cd = tables
    C = RET_CHUNK
    col = lambda j: pl.BlockSpec((seq_len, LANES), lambda b, hp, j=j: (b, 4 * j + hp))
    st_spec = pl.BlockSpec((1, 2, 2, RET_DK, RET_DK), lambda b, hp: (b, 0, hp, 0, 0))
    return pl.pallas_call(
        functools.partial(_ret_kernel, n_chunks=seq_len // C),
        grid=(batch, RET_HEADS // 2),
        in_specs=[col(0), col(1), col(2), col(3),
                  pl.BlockSpec((2, 2, C, C), lambda b, hp: (hp, 0, 0, 0)),
                  pl.BlockSpec((1, 2, 2, C, LANES), lambda b, hp: (hp, 0, 0, 0, 0)),
                  pl.BlockSpec((1, 2, LANES, LANES), lambda b, hp: (hp, 0, 0, 0)),
                  pl.BlockSpec((LANES, LANES), lambda b, hp: (0, 0)),
                  st_spec,
                  pl.BlockSpec((1, LANES), lambda b, hp: (0, hp))],
        out_specs=[pl.BlockSpec((seq_len, LANES), lambda b, hp: (b, hp)), st_spec],
        out_shape=[jax.ShapeDtypeStruct((rows, 512), F32),
                   jax.ShapeDtypeStruct((batch, 2, RET_HEADS, RET_DK, RET_DK), F32)],
        compiler_params=_cparams(2),
        name="retention",
    )(ret, ret, ret, ret, inner, rowdec, cd, ones_bd, s0, nw)


DN_BLOCK = 8
SOLVE_BASE = 8
CONV_BLOCK = 256
PAD = SUBLANES


def _dn_kernel(q_ref, k_ref, v_ref, g_ref, ab_ref, cw_ref, alog_ref, dtb_ref, nw_ref, s0_ref,
               o_ref, st_ref, qs_ref, ks_ref, vs_ref, of_ref, ob_ref,
               wp_ref, bm_ref, qp_ref, op_ref, dec_ref, *, seq_len):
    C = DN_CHUNK
    T = seq_len
    n_chunks = T // C
    head = pl.program_id(1)

    blk = min(CONV_BLOCK, T)
    row = lax.broadcasted_iota(jnp.int32, (blk, LANES), 0)
    srcs = (q_ref, k_ref, v_ref)
    dsts = (qs_ref, ks_ref, vs_ref)

    def conv(i, carry):
        r0 = pl.multiple_of(i * blk, blk)
        above = pl.multiple_of(jnp.maximum(r0 - SUBLANES, 0), SUBLANES)
        below = pl.multiple_of(jnp.minimum(r0 + blk, T - SUBLANES), SUBLANES)
        x = [src[pl.ds(r0, blk), :] for src in srcs]
        up = [jnp.where(r0 > 0, src[pl.ds(above, SUBLANES), :][SUBLANES - 1:], 0.0) for src in srcs]
        dn = [jnp.where(r0 + blk < T, src[pl.ds(below, SUBLANES), :][:1], 0.0) for src in srcs]
        prev = [jnp.where(row == 0, up[t], pltpu.roll(x[t], 1, 0)) for t in range(3)]
        nxt = [jnp.where(row == blk - 1, dn[t], pltpu.roll(x[t], blk - 1, 0)) for t in range(3)]
        w = [cw_ref[t, 0] for t in range(3)]
        y = [_silu(w[t][0:1] * prev[t] + w[t][1:2] * x[t] + w[t][2:3] * nxt[t]) for t in range(3)]
        ss = [jnp.sum(y[t] * y[t], axis=-1, keepdims=True) for t in range(2)]
        y[0] = y[0] * (lax.rsqrt(ss[0] + EPS) * (DN_DK ** -0.5))
        y[1] = y[1] * lax.rsqrt(ss[1] + EPS)
        for t in range(3):
            dsts[t][pl.ds(r0, blk), :] = y[t]
        return carry

    lax.fori_loop(0, T // blk, conv, 0)

    ri = lax.broadcasted_iota(jnp.int32, (C, C), 0)
    ci = lax.broadcasted_iota(jnp.int32, (C, C), 1)
    eye = ri == ci
    eye_f = eye.astype(F32)
    lane = lax.broadcasted_iota(jnp.int32, (1, LANES), 1)
    masks = ((ri >= ci, ri > ci), (ri <= ci, ri < ci))

    def load(r0):
        return tuple(ref[pl.ds(r0, C), :] for ref in (ab_ref, qs_ref, ks_ref, vs_ref))

    def prep(operands, direction):
        ab, q, k, v = operands
        incl, strict = masks[direction]
        sel_a = (lane == direction * DN_HEADS + head).astype(F32)
        sel_b = (lane == 2 * DN_HEADS + direction * DN_HEADS + head).astype(F32)
        da = jnp.sum(ab * sel_a, axis=-1, keepdims=True)
        db = jnp.sum(ab * sel_b, axis=-1, keepdims=True)
        z = da + dtb_ref[direction, 0]
        softplus = jnp.maximum(z, 0.0) + jnp.log1p(jnp.exp(-jnp.abs(z)))
        g = -jnp.exp(alog_ref[direction, 0]) * softplus
        beta = _sigmoid(db)
        kb = k * beta
        G = jnp.dot(incl.astype(F32), g, precision=lax.Precision.HIGHEST,
                    preferred_element_type=F32)
        kk = _dot_nt(kb, k)
        qk = _dot_nt(q, k)
        yield
        Gc = G[:, :C]
        Grow = jnp.sum(jnp.where(eye, Gc, 0.0), axis=0, keepdims=True)
        L = jnp.where(incl, jnp.exp(jnp.where(incl, Gc - Grow, 0.0)), 0.0)
        N = jnp.where(strict, -(kk * L), 0.0)
        same = (ri // SOLVE_BASE) == (ci // SOLVE_BASE)
        P = jnp.where(same, N, 0.0)
        Tm = eye_f + P
        P = _dot(P, P)
        yield
        for _ in range(int(math.log2(SOLVE_BASE)) - 2):
            Tm, P = Tm + _dot(Tm, P), _dot(P, P)
            yield
        Tm = Tm + _dot(Tm, P)
        yield
        size = SOLVE_BASE
        while size < C:
            size *= 2
            wider = (ri // size) == (ci // size)
            X = jnp.where(jnp.logical_and(wider, jnp.logical_not(same)), N, 0.0)
            TX = _dot(Tm, X)
            yield
            Tm = Tm + _dot(TX, Tm)
            yield
            same = wider
        eG = jnp.exp(G)
        g_last = G[C - 1:C] if direction == 0 else G[0:1]
        u = _dot(Tm, v * beta)
        w = _dot(Tm, kb * eG)
        yield
        kd = k * jnp.exp(g_last - G)
        att = qk * L
        return ((-_dot_tn(kd, w)).astype(BF16), _dot_tn(kd, u), (q * eG - _dot(att, w)).astype(BF16),
                _dot(att, u), jnp.broadcast_to(jnp.exp(g_last), (SUBLANES, LANES)))

    def run_staged(generators):
        results = [None] * len(generators)
        live = list(enumerate(generators))
        while live:
            still = []
            for idx, gen in live:
                try:
                    next(gen)
                    still.append((idx, gen))
                except StopIteration as done:
                    results[idx] = done.value
            live = still
        return results

    slots = (wp_ref, bm_ref, qp_ref, op_ref, dec_ref)

    nb = min(DN_BLOCK, n_chunks)
    n_blocks = n_chunks // nb

    def block_rows(i):
        return ([pl.multiple_of((i * nb + j) * C, C) for j in range(nb)]
                + [pl.multiple_of((n_chunks - 1 - (i * nb + j)) * C, C) for j in range(nb)])

    def prep_block(i):
        return [prep(load(r), s // nb) for s, r in enumerate(block_rows(i))]

    def store_block(prepared, base):
        for s in range(2 * nb):
            for ref, val in zip(slots, prepared[s]):
                ref[base + s] = val

    def recurrence(Sf, Sb, base):
        outs = []
        for j in range(nb):
            sf, sb = base + j, base + nb + j
            Sf16, Sb16 = Sf.astype(BF16), Sb.astype(BF16)
            outs.append((_dot(qp_ref[sf], Sf16) + op_ref[sf], _dot(qp_ref[sb], Sb16) + op_ref[sb]))
            Sf = Sf * dec_ref[sf, 0:1] + _dot(wp_ref[sf], Sf16) + bm_ref[sf]
            Sb = Sb * dec_ref[sb, 0:1] + _dot(wp_ref[sb], Sb16) + bm_ref[sb]
            yield
        return outs, Sf, Sb

    def store_outputs(i, outs):
        rows = block_rows(i)
        for j in range(nb):
            of_ref[pl.ds(rows[j], C), :] = outs[j][0]
            ob_ref[pl.ds(rows[nb + j], C), :] = outs[j][1]

    state = (s0_ref[0, 0, 0], s0_ref[0, 1, 0])
    if n_blocks == 1:
        store_block(run_staged(prep_block(0)), 0)
        (outs, Sf, Sb), = run_staged([recurrence(*state, 0)])
        store_outputs(0, outs)
    else:
        store_block(run_staged(prep_block(0)), 0)

        def body(i, carry):
            cur = (i % 2) * (2 * nb)
            nxt = 2 * nb - cur
            ahead = jnp.minimum(i + 1, n_blocks - 1)
            *prepared, (outs, Sf, Sb) = run_staged(prep_block(ahead) + [recurrence(*carry, cur)])
            store_outputs(i, outs)
            store_block(prepared, nxt)
            return Sf, Sb

        Sf, Sb = lax.fori_loop(0, n_blocks, body, state)
    st_ref[0, 0, 0] = Sf
    st_ref[0, 1, 0] = Sb

    def fin(i, carry):
        r0 = pl.multiple_of(i * blk, blk)
        o = of_ref[pl.ds(r0, blk), :] + ob_ref[pl.ds(r0, blk), :]
        y = o * lax.rsqrt(jnp.mean(o * o, axis=-1, keepdims=True) + EPS) * nw_ref[...]
        o_ref[pl.ds(r0, blk), :] = y * _silu(g_ref[pl.ds(r0, blk), :])
        return carry

    lax.fori_loop(0, T // blk, fin, 0)


def _deltanet(dn, ab, conv_w, alog, dtb, nw, s0, *, batch, seq_len):
    rows = dn.shape[0]
    T = seq_len
    C = DN_CHUNK
    nb = min(DN_BLOCK, T // C)
    col = lambda j: pl.BlockSpec((T, LANES), lambda b, h, j=j: (b, 4 * j + h))
    st_spec = pl.BlockSpec((1, 2, 1, DN_DK, DN_DK), lambda b, h: (b, 0, h, 0, 0))
    gate_spec = pl.BlockSpec((2, 1, 1, LANES), lambda b, h: (0, h, 0, 0))
    return pl.pallas_call(
        functools.partial(_dn_kernel, seq_len=T),
        grid=(batch, DN_HEADS),
        in_specs=[col(0), col(1), col(2), col(3),
                  pl.BlockSpec((T, LANES), lambda b, h: (b, 0)),
                  pl.BlockSpec((3, 1, 3, LANES), lambda b, h: (0, h, 0, 0)),
                  gate_spec, gate_spec,
                  pl.BlockSpec((1, LANES), lambda b, h: (0, 0)),
                  st_spec],
        out_specs=[pl.BlockSpec((T, LANES), lambda b, h: (b, h)), st_spec],
        out_shape=[jax.ShapeDtypeStruct((rows, 512), F32),
                   jax.ShapeDtypeStruct((batch, 2, DN_HEADS, DN_DK, DN_DK), F32)],
        scratch_shapes=([pltpu.VMEM((T, LANES), F32)] * 5
                        + [pltpu.VMEM((4 * nb, DN_DK, LANES), BF16), pltpu.VMEM((4 * nb, DN_DK, LANES), F32),
                           pltpu.VMEM((4 * nb, C, LANES), BF16), pltpu.VMEM((4 * nb, C, LANES), F32),
                           pltpu.VMEM((4 * nb, SUBLANES, LANES), F32)]),
        compiler_params=_cparams(2),
        name="deltanet",
    )(dn, dn, dn, dn, ab, conv_w, alog, dtb, nw, s0)


def _merge_kernel(x_ref, mod_ref, a_ref, r_ref, d_ref, mg_ref, wbr_ref, wout_ref, o_ref):
    d = x_ref.shape[-1]
    merged = jnp.zeros(x_ref.shape, F32)
    for i, br in enumerate((a_ref, r_ref, d_ref)):
        merged = merged + _sigmoid(mg_ref[:, i * d:(i + 1) * d]) * _dot(br[...], wbr_ref[i])
    out = _dot(merged, wout_ref[...])
    o_ref[...] = x_ref[...] + mod_ref[0][5:6] * out


def _merge(x, mod, a, r, dn, mg, w_br, w_out, *, rows_per_cond, tm=512):
    rows, d = x.shape
    tiles_per_cond = rows_per_cond // tm
    row_spec = lambda wd: pl.BlockSpec((tm, wd), lambda i: (i, 0))
    return pl.pallas_call(
        _merge_kernel,
        grid=(rows // tm,),
        in_specs=[row_spec(d),
                  pl.BlockSpec((1, N_MOD, d), lambda i: (i // tiles_per_cond, 0, 0)),
                  row_spec(512), row_spec(512), row_spec(512), row_spec(MG_W),
                  _resident(w_br.shape), _resident(w_out.shape)],
        out_specs=row_spec(d),
        out_shape=jax.ShapeDtypeStruct((rows, d), F32),
        compiler_params=_cparams(1),
        name="merge",
    )(x, mod, a, r, dn, mg, w_br, w_out)


def _rope_tables(seq_len):
    n_freq = HEAD_DIM // 4
    inv = ROPE_THETA ** (-jnp.arange(n_freq, dtype=F32) / n_freq)
    t = jnp.arange(seq_len)
    row = (t // GRID_W).astype(F32)
    colp = (t % GRID_W).astype(F32)
    ang = jnp.concatenate([row[:, None] * inv, colp[:, None] * inv], axis=-1)
    c, s = jnp.cos(ang), jnp.sin(ang)
    cos = jnp.concatenate([c, c, c, c], axis=-1)
    sin = jnp.concatenate([-s, s, -s, s], axis=-1)
    return cos, sin


def _reorder_w_in(w):
    d = w.shape[0]
    o_da = 768 + 2048 + 1536
    o_dg = o_da + 4 * DN_HEADS
    o_mg = o_dg + 512
    return jnp.concatenate([w[:, :o_da], w[:, o_dg:o_mg], w[:, o_da:o_dg],
                            jnp.zeros((d, AB_W - 4 * DN_HEADS), w.dtype), w[:, o_mg:]], axis=1)


def kernel(x_prompt, x_sample, cache_k, cache_v, state_ret, state_delta, c, c_ctx,
           w_mod, b_mod, norm_ffn1, ffn1_w_in, ffn1_w_out, norm_mix, w_in,
           attn_q_norm, attn_k_norm, ret_norm, dn_conv, dn_a_log, dn_dt_bias, dn_norm,
           w_br_attn, w_br_ret, w_br_dn, w_out, norm_ffn2, ffn2_w_in, ffn2_w_out, norm_final):
    bp, tp, d = x_prompt.shape
    bs, ts, _ = x_sample.shape
    depth = w_mod.shape[0]
    past = cache_k.shape[2]

    conds = jnp.concatenate([c_ctx[None, :], c], axis=0)
    mod = _modulation(conds, w_mod, b_mod).reshape(depth, 1 + bs, N_MOD, d)

    ones_bd = jnp.asarray(np.kron(np.eye(ATTN_HEADS), np.ones((HEAD_DIM, HEAD_DIM))), BF16)
    ret_tabs = _retention_tables()
    rope_tabs = _rope_tables(ts)
    ret_zero = jnp.zeros((bp, 2, RET_HEADS, RET_DK, RET_DK), F32)
    dn_zero = jnp.zeros((bp, 2, DN_HEADS, DN_DK, DN_DK), F32)
    fnw = norm_final.reshape(1, d)

    groups = {
        "prompt": dict(x=x_prompt.reshape(bp * tp, d), batch=bp, seq=tp, rows_per_cond=bp * tp, rope=None),
        "sample": dict(x=x_sample.reshape(bs * ts, d), batch=bs, seq=ts, rows_per_cond=ts, rope=rope_tabs),
    }
    new_k, new_v, new_rs, new_ds = [], [], [], []
    for l in range(depth):
        w1_in, w1_out = ffn1_w_in[l].astype(BF16), ffn1_w_out[l].astype(BF16)
        w2_in, w2_out = ffn2_w_in[l].astype(BF16), ffn2_w_out[l].astype(BF16)
        w_proj = _reorder_w_in(w_in[l]).astype(BF16)
        w_br = jnp.stack([w_br_attn[l], w_br_ret[l], w_br_dn[l]]).astype(BF16)
        w_o = w_out[l].astype(BF16)
        gq = jnp.tile(attn_q_norm[l], ATTN_HEADS).reshape(1, 512)
        gk = jnp.tile(attn_k_norm[l], ATTN_KV_HEADS).reshape(1, LANES)
        conv_w = dn_conv[l].reshape(3, 3, DN_HEADS, LANES).transpose(1, 2, 0, 3)
        alog = jnp.broadcast_to(dn_a_log[l][:, :, None, None], (2, DN_HEADS, 1, LANES))
        dtb = jnp.broadcast_to(dn_dt_bias[l][:, :, None, None], (2, DN_HEADS, 1, LANES))
        for name, grp in groups.items():
            is_prompt = name == "prompt"
            x = grp["x"]
            gmod = mod[l, :1] if is_prompt else mod[l, 1:]
            rpc = grp["rows_per_cond"]
            x = _ffn(x, gmod, norm_ffn1[l].reshape(1, d), w1_in, w1_out, fnw,
                     mod_base=0, rows_per_cond=rpc, final=False)
            attn, ret, dn, ab, mg = _inproj(x, gmod, norm_mix[l].reshape(1, d), w_proj, ones_bd, gq, gk,
                                            grp["rope"], rows_per_cond=rpc, seq_len=grp["seq"])
            if is_prompt:
                a_out = _attention(attn, None, None, batch=bp, seq_len=tp)
                rs0, ds0 = ret_zero, dn_zero
            else:
                a_out = _attention(attn, cache_k[:, l].reshape(bs, past, LANES),
                                   cache_v[:, l].reshape(bs, past, LANES), batch=bs, seq_len=ts)
                rs0, ds0 = state_ret[:, l], state_delta[:, l]
            r_out, rs = _retention(ret, rs0, ret_norm[l].reshape(1, 512), ret_tabs, ones_bd,
                                   batch=grp["batch"], seq_len=grp["seq"])
            d_out, ds = _deltanet(dn, ab, conv_w, alog, dtb, dn_norm[l].reshape(1, LANES), ds0,
                                  batch=grp["batch"], seq_len=grp["seq"])
            x = _merge(x, gmod, a_out, r_out, d_out, mg, w_br, w_o, rows_per_cond=rpc)
            x = _ffn(x, gmod, norm_ffn2[l].reshape(1, d), w2_in, w2_out, fnw,
                     mod_base=6, rows_per_cond=rpc, final=(l == depth - 1))
            grp["x"] = x
            if is_prompt:
                new_k.append(attn[:, 512:640].reshape(bp, tp, ATTN_KV_HEADS, HEAD_DIM))
                new_v.append(attn[:, 640:768].reshape(bp, tp, ATTN_KV_HEADS, HEAD_DIM))
                new_rs.append(rs)
                new_ds.append(ds)

    y_prompt = groups["prompt"]["x"].reshape(bp, tp, d)
    y_sample = groups["sample"]["x"].reshape(bs, ts, d)
    return (y_prompt, y_sample, jnp.stack(new_k, axis=1), jnp.stack(new_v, axis=1),
            jnp.stack(new_rs, axis=1), jnp.stack(new_ds, axis=1))
```

```python
import functools
import math

import numpy as np
import jax
import jax.numpy as jnp
from jax import lax
from jax.experimental import pallas as pl
from jax.experimental.pallas import tpu as pltpu

F32 = jnp.float32
BF16 = jnp.bfloat16

EPS = 1e-6
ROPE_THETA = 10000.0
GRID_W = 64
N_MOD = 9

ATTN_HEADS = 8
ATTN_KV_HEADS = 2
HEAD_DIM = 64
RET_HEADS = 8
RET_DK = 64
RET_CHUNK = 128
RET_DECAY_EXP_FWD = 5.0
RET_DECAY_EXP_BWD = 5.5
DN_HEADS = 4
DN_DK = 128
DN_CHUNK = 64
N_BRANCH = 3

LANES = 128
SUBLANES = 8
VMEM_LIMIT = 56 * 1024 * 1024

ATTN_W = 768
RET_W = 2048
DN_W = 2048
AB_W = 128
MG_W = 3072
IN_W = ATTN_W + RET_W + DN_W + AB_W + MG_W


def _cparams(n_axes):
    return pltpu.CompilerParams(dimension_semantics=("parallel",) * n_axes,
                                vmem_limit_bytes=VMEM_LIMIT)


def _resident(shape):
    zeros = (0,) * len(shape)
    return pl.BlockSpec(shape, lambda *_: zeros, pipeline_mode=pl.Buffered(1))


def _dot(a, b):
    return jnp.dot(a.astype(BF16), b.astype(BF16), preferred_element_type=F32)


def _dot_nt(a, b):
    return lax.dot_general(a.astype(BF16), b.astype(BF16), (((1,), (1,)), ((), ())),
                           preferred_element_type=F32)


def _dot_tn(a, b):
    return lax.dot_general(a.astype(BF16), b.astype(BF16), (((0,), (0,)), ((), ())),
                           preferred_element_type=F32)


def _sigmoid(x):
    return 1.0 / (1.0 + jnp.exp(-x))


def _silu(x):
    return x * _sigmoid(x)


def _norm_mod(x, nw, shift, scale):
    y = x * lax.rsqrt(jnp.mean(x * x, axis=-1, keepdims=True) + EPS) * nw
    return y * (1.0 + scale) + shift


def _mod_kernel(c_ref, w_ref, b_ref, o_ref):
    o_ref[0] = _dot(_silu(c_ref[...]), w_ref[0]) + b_ref[0]


def _modulation(conds, w_mod, b_mod):
    depth, d, n = w_mod.shape
    nc = conds.shape[0]
    tn = n // N_MOD
    return pl.pallas_call(
        _mod_kernel,
        grid=(depth, n // tn),
        in_specs=[pl.BlockSpec((nc, d), lambda l, j: (0, 0)),
                  pl.BlockSpec((1, d, tn), lambda l, j: (l, 0, j)),
                  pl.BlockSpec((1, 1, tn), lambda l, j: (l, 0, j))],
        out_specs=pl.BlockSpec((1, nc, tn), lambda l, j: (l, 0, j)),
        out_shape=jax.ShapeDtypeStruct((depth, nc, n), F32),
        compiler_params=_cparams(2),
        name="modulation",
    )(conds, w_mod, b_mod.reshape(depth, 1, n))


FFN_CHUNK = 256


def _ffn_kernel(x_ref, mod_ref, nw_ref, win_ref, wout_ref, fnw_ref, o_ref, *, mod_base, dff, final):
    x = x_ref[...]
    m = mod_ref[0]
    shift, scale, gate = (m[mod_base + i:mod_base + i + 1] for i in range(3))
    h = _norm_mod(x, nw_ref[...], shift, scale).astype(BF16)
    acc = jnp.zeros(x.shape, F32)
    for c in range(dff // FFN_CHUNK):
        lo = c * FFN_CHUNK
        hg = jnp.dot(h, win_ref[:, lo:lo + FFN_CHUNK], preferred_element_type=F32)
        hu = jnp.dot(h, win_ref[:, dff + lo:dff + lo + FFN_CHUNK], preferred_element_type=F32)
        a = (_silu(hg) * hu).astype(BF16)
        acc = acc + jnp.dot(a, wout_ref[lo:lo + FFN_CHUNK, :], preferred_element_type=F32)
    y = x + 0.5 * gate * acc
    if final:
        y = y * lax.rsqrt(jnp.mean(y * y, axis=-1, keepdims=True) + EPS) * fnw_ref[...]
    o_ref[...] = y


def _ffn(x, mod, nw, w_in, w_out, fnw, *, mod_base, rows_per_cond, final, tm=512):
    rows, d = x.shape
    dff = w_out.shape[0]
    tiles_per_cond = rows_per_cond // tm
    return pl.pallas_call(
        functools.partial(_ffn_kernel, mod_base=mod_base, dff=dff, final=final),
        grid=(rows // tm,),
        in_specs=[pl.BlockSpec((tm, d), lambda i: (i, 0)),
                  pl.BlockSpec((1, N_MOD, d), lambda i: (i // tiles_per_cond, 0, 0)),
                  _resident((1, d)),
                  _resident(w_in.shape),
                  _resident(w_out.shape),
                  _resident((1, d))],
        out_specs=pl.BlockSpec((tm, d), lambda i: (i, 0)),
        out_shape=jax.ShapeDtypeStruct((rows, d), F32),
        compiler_params=_cparams(1),
        name="ffn",
    )(x, mod, nw, w_in, w_out, fnw)


def _swap_halves(x):
    n = x.shape[-1]
    lane = lax.broadcasted_iota(jnp.int32, x.shape, 1)
    first = (lane % HEAD_DIM) < (HEAD_DIM // 2)
    return jnp.where(first, pltpu.roll(x, n - HEAD_DIM // 2, 1), pltpu.roll(x, HEAD_DIM // 2, 1))


def _rope(x, cos, sin):
    reps = x.shape[-1] // LANES
    c = jnp.concatenate([cos] * reps, axis=1) if reps > 1 else cos
    s = jnp.concatenate([sin] * reps, axis=1) if reps > 1 else sin
    return x * c + _swap_halves(x) * s


def _head_rms(x, ones_bd, gain):
    sq = x * x
    hi = sq.astype(BF16)
    lo = (sq - hi.astype(F32)).astype(BF16)
    ss = (jnp.dot(hi, ones_bd, preferred_element_type=F32)
          + jnp.dot(lo, ones_bd, preferred_element_type=F32))
    return x * lax.rsqrt(ss * (1.0 / HEAD_DIM) + EPS) * gain


def _inproj_kernel(*refs, rope):
    if rope:
        (x_ref, mod_ref, nw_ref, w_ref, ones_ref, gq_ref, gk_ref, cos_ref, sin_ref,
         attn_ref, ret_ref, dn_ref, ab_ref, mg_ref) = refs
    else:
        (x_ref, mod_ref, nw_ref, w_ref, ones_ref, gq_ref, gk_ref,
         attn_ref, ret_ref, dn_ref, ab_ref, mg_ref) = refs
    m = mod_ref[0]
    h = _norm_mod(x_ref[...], nw_ref[...], m[3:4], m[4:5]).astype(BF16)

    def proj(lo, width):
        return jnp.dot(h, w_ref[:, lo:lo + width], preferred_element_type=F32)

    def rot(v):
        return _rope(v, cos_ref[...], sin_ref[...]) if rope else v

    a = proj(0, ATTN_W)
    q = _head_rms(a[:, :512], ones_ref[...], gq_ref[...])
    k = _head_rms(a[:, 512:640], ones_ref[:LANES, :LANES], gk_ref[...])
    attn_ref[:, :512] = rot(q) * (HEAD_DIM ** -0.5)
    attn_ref[:, 512:640] = rot(k)
    attn_ref[:, 640:768] = a[:, 640:768]

    r = proj(ATTN_W, RET_W)
    ret_ref[:, :512] = rot(r[:, :512]) * (RET_DK ** -0.5)
    ret_ref[:, 512:1024] = rot(r[:, 512:1024])
    ret_ref[:, 1024:] = r[:, 1024:]

    dn_ref[...] = proj(ATTN_W + RET_W, DN_W)
    ab_ref[...] = proj(ATTN_W + RET_W + DN_W, AB_W)
    mg_ref[...] = proj(ATTN_W + RET_W + DN_W + AB_W, MG_W)


def _inproj(x, mod, nw, w, ones_bd, gq, gk, rope_tabs, *, rows_per_cond, seq_len, tm=256):
    rows, d = x.shape
    tiles_per_cond = rows_per_cond // tm
    tiles_per_seq = seq_len // tm
    rope = rope_tabs is not None
    in_specs = [pl.BlockSpec((tm, d), lambda i: (i, 0)),
                pl.BlockSpec((1, N_MOD, d), lambda i: (i // tiles_per_cond, 0, 0)),
                _resident((1, d)),
                _resident(w.shape),
                _resident(ones_bd.shape),
                _resident(gq.shape),
                _resident(gk.shape)]
    args = [x, mod, nw, w, ones_bd, gq, gk]
    if rope:
        in_specs += [pl.BlockSpec((tm, LANES), lambda i: (i % tiles_per_seq, 0))] * 2
        args += list(rope_tabs)
    widths = (ATTN_W, RET_W, DN_W, AB_W, MG_W)
    return pl.pallas_call(
        functools.partial(_inproj_kernel, rope=rope),
        grid=(rows // tm,),
        in_specs=in_specs,
        out_specs=[pl.BlockSpec((tm, wd), lambda i: (i, 0)) for wd in widths],
        out_shape=[jax.ShapeDtypeStruct((rows, wd), F32) for wd in widths],
        compiler_params=_cparams(1),
        name="inproj",
    )(*args)


ATTN_GROUP = ATTN_HEADS // ATTN_KV_HEADS
LOG2E = math.log2(math.e)
ATTN_CHUNK_GROUP = 8
ATTN_ROW_BLOCKS = 2


def _attn_kernel(*refs, tq, kv_chunk, n_chunks, has_ctx):
    if has_ctx:
        q_ref, k_ref, v_ref, ck_ref, cv_ref, o_ref, mx_ref, ls_ref, acc_ref, qs_ref = refs
    else:
        q_ref, k_ref, v_ref, o_ref, mx_ref, ls_ref, acc_ref, qs_ref = refs
    lane = lax.broadcasted_iota(jnp.int32, (tq, LANES), 1)
    low = lane < HEAD_DIM
    parts = []
    for hd in range(ATTN_HEADS):
        g = hd // ATTN_GROUP
        blk = q_ref[:, (hd // 2) * LANES:(hd // 2 + 1) * LANES] * LOG2E
        if hd % 2 != g:
            blk = pltpu.roll(blk, HEAD_DIM, 1)
        parts.append(jnp.where(low if g == 0 else jnp.logical_not(low), blk, 0.0))
    qs_ref[...] = jnp.concatenate(parts, axis=0).astype(BF16)

    def lane_fold(op, acc, x):
        for b in range(x.shape[1] // LANES):
            acc = op(acc, x[:, b * LANES:(b + 1) * LANES])
        return acc

    mx_ref[...] = jnp.full(mx_ref.shape, -jnp.inf, F32)
    ls_ref[...] = jnp.zeros(ls_ref.shape, F32)
    acc_ref[...] = jnp.zeros(acc_ref.shape, F32)
    rbs = qs_ref.shape[0] // ATTN_ROW_BLOCKS
    blocks = [slice(r * rbs, (r + 1) * rbs) for r in range(ATTN_ROW_BLOCKS)]

    def scores(sl, k):
        return _dot_nt(qs_ref[sl], k)

    def update(sl, s, v):
        reps = s.shape[1] // LANES
        m_old = mx_ref[sl]
        cmax = lane_fold(jnp.maximum, s[:, :LANES], s[:, LANES:])
        m_new = jnp.maximum(m_old, jnp.max(cmax, axis=-1, keepdims=True))
        alpha = jnp.exp2(m_old - m_new)
        p = jnp.exp2(s - jnp.concatenate([m_new] * reps, axis=1))
        mx_ref[sl] = m_new
        ls_ref[sl] = alpha * ls_ref[sl] + lane_fold(jnp.add, p[:, :LANES], p[:, LANES:])
        acc_ref[sl] = alpha * acc_ref[sl] + _dot(p, v)

    def chunks(kvs):
        units = [(sl, k.astype(BF16), v.astype(BF16)) for k, v in kvs for sl in blocks]
        ahead = None
        for sl, k, v in units + [(None, None, None)]:
            nxt = scores(sl, k) if sl is not None else None
            if ahead is not None:
                update(*ahead)
            ahead = (sl, nxt, v)

    group = min(ATTN_CHUNK_GROUP, n_chunks)

    def body(i, carry):
        kvs = []
        for j in range(group):
            off = pl.multiple_of((i * group + j) * kv_chunk, kv_chunk)
            kvs.append((k_ref[pl.ds(off, kv_chunk), :], v_ref[pl.ds(off, kv_chunk), :]))
        chunks(kvs)
        return carry

    tail = [(ck_ref[0], cv_ref[0])] if has_ctx else []
    if group == n_chunks:
        chunks([(k_ref[c * kv_chunk:(c + 1) * kv_chunk, :], v_ref[c * kv_chunk:(c + 1) * kv_chunk, :])
                for c in range(n_chunks)] + tail)
    else:
        lax.fori_loop(0, n_chunks // group, body, 0)
        if tail:
            chunks(tail)
    res = acc_ref[...] / jnp.sum(ls_ref[...], axis=-1, keepdims=True)

    for b in range(ATTN_HEADS // 2):
        g = (2 * b) // ATTN_GROUP
        even = res[2 * b * tq:(2 * b + 1) * tq]
        odd = res[(2 * b + 1) * tq:(2 * b + 2) * tq]
        if g == 0:
            odd = pltpu.roll(odd, HEAD_DIM, 1)
        else:
            even = pltpu.roll(even, HEAD_DIM, 1)
        o_ref[:, b * LANES:(b + 1) * LANES] = jnp.where(low, even, odd)


def _attention(attn, ctx_k, ctx_v, *, batch, seq_len, tq=128):
    rows = attn.shape[0]
    kv_chunk = min(seq_len, 512)
    n_chunks = seq_len // kv_chunk
    qt = seq_len // tq
    stacked = ATTN_HEADS * tq
    has_ctx = ctx_k is not None
    in_specs = [pl.BlockSpec((tq, 512), lambda b, i: (b * qt + i, 0)),
                pl.BlockSpec((seq_len, LANES), lambda b, i: (b, 4)),
                pl.BlockSpec((seq_len, LANES), lambda b, i: (b, 5))]
    args = [attn, attn, attn]
    if has_ctx:
        past = ctx_k.shape[1]
        in_specs += [pl.BlockSpec((1, past, LANES), lambda b, i: (b, 0, 0))] * 2
        args += [ctx_k, ctx_v]
    scratch = [pltpu.VMEM((stacked, LANES), F32)] * 3 + [pltpu.VMEM((stacked, LANES), BF16)]
    return pl.pallas_call(
        functools.partial(_attn_kernel, tq=tq, kv_chunk=kv_chunk, n_chunks=n_chunks, has_ctx=has_ctx),
        grid=(batch, qt),
        in_specs=in_specs,
        out_specs=pl.BlockSpec((tq, 512), lambda b, i: (b * qt + i, 0)),
        out_shape=jax.ShapeDtypeStruct((rows, 512), F32),
        scratch_shapes=scratch,
        compiler_params=_cparams(2),
        name="attention",
    )(*args)


def _retention_tables():
    C = RET_CHUNK
    h = np.arange(RET_HEADS, dtype=np.float64)
    pos = np.arange(C, dtype=np.float64)
    diff = pos[:, None] - pos[None, :]
    inner, qd, kd, cd = [], [], [], []
    for direction, expo in enumerate((RET_DECAY_EXP_FWD, RET_DECAY_EXP_BWD)):
        lg = np.log1p(-np.exp2(-expo - h))[:, None, None]
        if direction == 0:
            inner.append(np.where(diff >= 0, np.exp(lg * np.maximum(diff, 0.0)), 0.0))
            qd.append(np.exp(lg[:, :, 0] * (pos + 1.0)))
            kd.append(np.exp(lg[:, :, 0] * (C - 1.0 - pos)))
        else:
            inner.append(np.where(diff <= 0, np.exp(lg * np.maximum(-diff, 0.0)), 0.0))
            qd.append(np.exp(lg[:, :, 0] * (C - pos)))
            kd.append(np.exp(lg[:, :, 0] * pos))
        cd.append(np.exp(lg[:, 0, 0] * C))
    inner = np.stack(inner, axis=1)
    rowdec = np.stack([np.stack(qd, 1), np.stack(kd, 1)], axis=2)
    rowdec = np.repeat(rowdec[..., None], RET_DK, axis=-1)
    rowdec = np.concatenate([rowdec[0::2], rowdec[1::2]], axis=-1)
    cd = np.stack(cd, axis=1)
    block = np.kron(np.eye(2), np.ones((RET_DK, RET_DK)))
    per_row = np.repeat(np.stack([cd[0::2], cd[1::2]], axis=-1), RET_DK, axis=-1)
    cd = per_row[:, :, :, None] * block
    return (jnp.asarray(inner, F32), jnp.asarray(rowdec, F32), jnp.asarray(cd, F32))


RET_BLOCK = 8


def _split_bf16(x):
    hi = x.astype(BF16)
    return hi, (x - hi.astype(F32)).astype(BF16)


def _ret_kernel(q_ref, k_ref, v_ref, g_ref, inner_ref, dec_ref, cd_ref, ones_ref, s0_ref, nw_ref,
                o_ref, st_ref, *, n_chunks):
    C = RET_CHUNK
    D = RET_DK
    nb = min(RET_BLOCK, n_chunks)
    lane = lax.broadcasted_iota(jnp.int32, (C, LANES), 1)
    first = lane < D
    ones = ones_ref[...]

    def head_mean(x):
        hi, lo = _split_bf16(x)
        return (jnp.dot(hi, ones, preferred_element_type=F32)
                + jnp.dot(lo, ones, preferred_element_type=F32)) * (1.0 / D)

    def sweep(direction):
        def body(i, S):
            order = [i * nb + j for j in range(nb)]
            if direction == 1:
                order = [n_chunks - 1 - c for c in order]
            starts = [pl.multiple_of(c * C, C) for c in order]
            q = [q_ref[pl.ds(r, C), :] for r in starts]
            k = [k_ref[pl.ds(r, C), :] for r in starts]
            v = [v_ref[pl.ds(r, C), :] for r in starts]
            if direction == 1:
                prev = [o_ref[pl.ds(r, C), :] for r in starts]
                gate = [g_ref[pl.ds(r, C), :] for r in starts]
            att = [jnp.concatenate([_dot_nt(jnp.where(first, q[j], 0.0), k[j]) * inner_ref[0, direction],
                                    _dot_nt(jnp.where(first, 0.0, q[j]), k[j]) * inner_ref[1, direction]],
                                   axis=1) for j in range(nb)]
            kv = [_dot_tn(k[j] * dec_ref[0, direction, 1], v[j]) for j in range(nb)]
            seen = []
            keep = cd_ref[0, direction] != 0.0
            for j in range(nb):
                seen.append(S)
                S = S * cd_ref[0, direction] + jnp.where(keep, kv[j], 0.0)
            v2 = [jnp.concatenate([jnp.where(first, v[j], 0.0), jnp.where(first, 0.0, v[j])], axis=0)
                  for j in range(nb)]
            o = [_dot(att[j], v2[j]) + _dot(q[j] * dec_ref[0, direction, 0], seen[j]) for j in range(nb)]
            if direction == 1:
                o = [o[j] + prev[j] for j in range(nb)]
                mean = [head_mean(o[j]) for j in range(nb)]
                d = [o[j] - mean[j] for j in range(nb)]
                var = [head_mean(d[j] * d[j]) for j in range(nb)]
                o = [d[j] * lax.rsqrt(var[j] + EPS) * nw_ref[...] * _silu(gate[j]) for j in range(nb)]
            for j in range(nb):
                o_ref[pl.ds(starts[j], C), :] = o[j]
            return S

        s0 = s0_ref[0, direction]
        zero = jnp.zeros((D, D), F32)
        S = jnp.concatenate([jnp.concatenate([s0[0], zero], axis=1),
                             jnp.concatenate([zero, s0[1]], axis=1)], axis=0)
        S = lax.fori_loop(0, n_chunks // nb, body, S)
        st_ref[0, direction, 0] = S[:D, :D]
        st_ref[0, direction, 1] = S[D:, D:]

    sweep(0)
    sweep(1)


def _retention(ret, s0, nw, tables, ones_bd, *, batch, seq_len):
    rows = ret.shape[0]
    inner, rowdec, cd = tables
    C = RET_CHUNK
    col = lambda j: pl.BlockSpec((seq_len, LANES), lambda b, hp, j=j: (b, 4 * j + hp))
    st_spec = pl.BlockSpec((1, 2, 2, RET_DK, RET_DK), lambda b, hp: (b, 0, hp, 0, 0))
    return pl.pallas_call(
        functools.partial(_ret_kernel, n_chunks=seq_len // C),
        grid=(batch, RET_HEADS // 2),
        in_specs=[col(0), col(1), col(2), col(3),
                  pl.BlockSpec((2, 2, C, C), lambda b, hp: (hp, 0, 0, 0)),
                  pl.BlockSpec((1, 2, 2, C, LANES), lambda b, hp: (hp, 0, 0, 0, 0)),
                  pl.BlockSpec((1, 2, LANES, LANES), lambda b, hp: (hp, 0, 0, 0)),
                  pl.BlockSpec((LANES, LANES), lambda b, hp: (0, 0)),
                  st_spec,
                  pl.BlockSpec((1, LANES), lambda b, hp: (0, hp))],
        out_specs=[pl.BlockSpec((seq_len, LANES), lambda b, hp: (b, hp)), st_spec],
        out_shape=[jax.ShapeDtypeStruct((rows, 512), F32),
                   jax.ShapeDtypeStruct((batch, 2, RET_HEADS, RET_DK, RET_DK), F32)],
        compiler_params=_cparams(2),
        name="retention",
    )(ret, ret, ret, ret, inner, rowdec, cd, ones_bd, s0, nw)


DN_BLOCK = 8
SOLVE_BASE = 8
CONV_BLOCK = 256


def _dn_kernel(q_ref, k_ref, v_ref, g_ref, ab_ref, cw_ref, alog_ref, dtb_ref, nw_ref, s0_ref,
               o_ref, st_ref, qs_ref, ks_ref, vs_ref, of_ref, ob_ref,
               wp_ref, bm_ref, qp_ref, op_ref, dec_ref, *, seq_len, n_heads):
    C = DN_CHUNK
    T = seq_len
    U = n_heads
    n_chunks = T // C
    head0 = pl.program_id(1) * U
    lanes_of = [slice(hh * LANES, (hh + 1) * LANES) for hh in range(U)]

    blk = min(CONV_BLOCK, T)
    row = lax.broadcasted_iota(jnp.int32, (blk, LANES), 0)
    tensors = [(src, dst, t, hh) for hh in range(U)
               for t, (src, dst) in enumerate(((q_ref, qs_ref), (k_ref, ks_ref), (v_ref, vs_ref)))]

    def conv(i, carry):
        r0 = pl.multiple_of(i * blk, blk)
        above = pl.multiple_of(jnp.maximum(r0 - SUBLANES, 0), SUBLANES)
        below = pl.multiple_of(jnp.minimum(r0 + blk, T - SUBLANES), SUBLANES)
        x = [src[pl.ds(r0, blk), lanes_of[hh]] for src, _, _, hh in tensors]
        up = [jnp.where(r0 > 0, src[pl.ds(above, SUBLANES), lanes_of[hh]][SUBLANES - 1:], 0.0)
              for src, _, _, hh in tensors]
        dn = [jnp.where(r0 + blk < T, src[pl.ds(below, SUBLANES), lanes_of[hh]][:1], 0.0)
              for src, _, _, hh in tensors]
        n = len(tensors)
        prev = [jnp.where(row == 0, up[e], pltpu.roll(x[e], 1, 0)) for e in range(n)]
        nxt = [jnp.where(row == blk - 1, dn[e], pltpu.roll(x[e], blk - 1, 0)) for e in range(n)]
        w = [cw_ref[t, hh] for _, _, t, hh in tensors]
        y = [_silu(w[e][0:1] * prev[e] + w[e][1:2] * x[e] + w[e][2:3] * nxt[e]) for e in range(n)]
        for e, (_, dst, t, hh) in enumerate(tensors):
            if t < 2:
                scale = lax.rsqrt(jnp.sum(y[e] * y[e], axis=-1, keepdims=True) + EPS)
                y[e] = y[e] * (scale * (DN_DK ** -0.5) if t == 0 else scale)
        for e, (_, dst, t, hh) in enumerate(tensors):
            dst[pl.ds(r0, blk), lanes_of[hh]] = y[e]
        return carry

    lax.fori_loop(0, T // blk, conv, 0)

    ri = lax.broadcasted_iota(jnp.int32, (C, C), 0)
    ci = lax.broadcasted_iota(jnp.int32, (C, C), 1)
    eye = ri == ci
    eye_f = eye.astype(F32)
    lane = lax.broadcasted_iota(jnp.int32, (1, LANES), 1)
    masks = ((ri >= ci, ri > ci), (ri <= ci, ri < ci))
    incl_bf = tuple(m[0].astype(F32).astype(BF16) for m in masks)

    def load(r0, hh):
        return (ab_ref[pl.ds(r0, C), :],) + tuple(ref[pl.ds(r0, C), lanes_of[hh]]
                                                   for ref in (qs_ref, ks_ref, vs_ref))

    def prep(operands, direction, hh):
        ab, q, k, v = operands
        incl, strict = masks[direction]
        sel_a = (lane == direction * DN_HEADS + head0 + hh).astype(F32)
        sel_b = (lane == 2 * DN_HEADS + direction * DN_HEADS + head0 + hh).astype(F32)
        da = jnp.sum(ab * sel_a, axis=-1, keepdims=True)
        db = jnp.sum(ab * sel_b, axis=-1, keepdims=True)
        z = da + dtb_ref[direction, hh]
        softplus = jnp.maximum(z, 0.0) + jnp.log1p(jnp.exp(-jnp.abs(z)))
        g = -jnp.exp(alog_ref[direction, hh]) * softplus
        beta = _sigmoid(db)
        kb = k * beta
        g1 = g.astype(BF16)
        g2 = (g - g1.astype(F32)).astype(BF16)
        g3 = (g - g1.astype(F32) - g2.astype(F32)).astype(BF16)
        G12 = _dot(incl_bf[direction], jnp.concatenate([g1, g2], axis=1))
        G = G12[:, :LANES] + G12[:, LANES:] + _dot(incl_bf[direction], g3)
        kk = _dot_nt(kb, k)
        qk = _dot_nt(q, k)
        yield
        Gc = G[:, :C]
        Grow = jnp.sum(jnp.where(eye, Gc, 0.0), axis=0, keepdims=True)
        L = jnp.where(incl, jnp.exp(jnp.where(incl, Gc - Grow, 0.0)), 0.0)
        N = jnp.where(strict, -(kk * L), 0.0)
        same = (ri // SOLVE_BASE) == (ci // SOLVE_BASE)
        P = jnp.where(same, N, 0.0)
        Tm = eye_f + P
        P = _dot(P, P)
        yield
        for _ in range(int(math.log2(SOLVE_BASE)) - 2):
            Tm, P = Tm + _dot(Tm, P), _dot(P, P)
            yield
        Tm = Tm + _dot(Tm, P)
        yield
        size = SOLVE_BASE
        while size < C:
            size *= 2
            wider = (ri // size) == (ci // size)
            X = jnp.where(jnp.logical_and(wider, jnp.logical_not(same)), N, 0.0)
            TX = _dot(Tm, X)
            yield
            Tm = Tm + _dot(TX, Tm)
            yield
            same = wider
        eG = jnp.exp(G)
        g_last = G[C - 1:C] if direction == 0 else G[0:1]
        wu = _dot(Tm, jnp.concatenate([kb * eG, v * beta], axis=1))
        yield
        kd = k * jnp.exp(g_last - G)
        att = qk * L
        kd_wu = _dot_tn(kd, wu)
        att_wu = _dot(att, wu)
        return ((-kd_wu[:, :LANES]).astype(BF16), kd_wu[:, LANES:],
                (q * eG - att_wu[:, :LANES]).astype(BF16), att_wu[:, LANES:],
                jnp.broadcast_to(jnp.exp(g_last), (SUBLANES, LANES)))

    def run_staged(generators):
        results = [None] * len(generators)
        live = list(enumerate(generators))
        while live:
            still = []
            for idx, gen in live:
                try:
                    next(gen)
                    still.append((idx, gen))
                except StopIteration as done:
                    results[idx] = done.value
            live = still
        return results

    slots = (wp_ref, bm_ref, qp_ref, op_ref, dec_ref)
    nb = min(DN_BLOCK, n_chunks)
    n_blocks = n_chunks // nb
    per_half = 2 * nb * U

    def block_rows(i):
        return ([pl.multiple_of((i * nb + j) * C, C) for j in range(nb)]
                + [pl.multiple_of((n_chunks - 1 - (i * nb + j)) * C, C) for j in range(nb)])

    def prep_block(i):
        rows = block_rows(i)
        return [prep(load(r, hh), s // nb, hh) for hh in range(U) for s, r in enumerate(rows)]

    def store_block(prepared, base):
        for e in range(per_half):
            for ref, val in zip(slots, prepared[e]):
                ref[base + e] = val

    def recurrence(states, base):
        states = list(states)
        outs = []
        for j in range(nb):
            step_out = []
            for hh in range(U):
                for direction in range(2):
                    e = base + hh * 2 * nb + direction * nb + j
                    S = states[2 * hh + direction]
                    S16 = S.astype(BF16)
                    step_out.append(_dot(qp_ref[e], S16) + op_ref[e])
                    states[2 * hh + direction] = S * dec_ref[e, 0:1] + _dot(wp_ref[e], S16) + bm_ref[e]
            outs.append(step_out)
            yield
        return outs, tuple(states)

    def store_outputs(i, outs):
        rows = block_rows(i)
        for j in range(nb):
            for hh in range(U):
                of_ref[pl.ds(rows[j], C), lanes_of[hh]] = outs[j][2 * hh]
                ob_ref[pl.ds(rows[nb + j], C), lanes_of[hh]] = outs[j][2 * hh + 1]

    state = tuple(s0_ref[0, direction, hh] for hh in range(U) for direction in range(2))
    if n_blocks == 1:
        store_block(run_staged(prep_block(0)), 0)
        (outs, state), = run_staged([recurrence(state, 0)])
        store_outputs(0, outs)
    else:
        store_block(run_staged(prep_block(0)), 0)

        def body(i, carry):
            cur = (i % 2) * per_half
            nxt = per_half - cur
            ahead = jnp.minimum(i + 1, n_blocks - 1)
            *prepared, (outs, carry) = run_staged(prep_block(ahead) + [recurrence(carry, cur)])
            store_outputs(i, outs)
            store_block(prepared, nxt)
            return carry

        state = lax.fori_loop(0, n_blocks, body, state)
    for hh in range(U):
        for direction in range(2):
            st_ref[0, direction, hh] = state[2 * hh + direction]

    def fin(i, carry):
        r0 = pl.multiple_of(i * blk, blk)
        for hh in range(U):
            o = of_ref[pl.ds(r0, blk), lanes_of[hh]] + ob_ref[pl.ds(r0, blk), lanes_of[hh]]
            y = o * lax.rsqrt(jnp.mean(o * o, axis=-1, keepdims=True) + EPS) * nw_ref[...]
            o_ref[pl.ds(r0, blk), lanes_of[hh]] = y * _silu(g_ref[pl.ds(r0, blk), lanes_of[hh]])
        return carry

    lax.fori_loop(0, T // blk, fin, 0)


def _deltanet(dn, ab, conv_w, alog, dtb, nw, s0, *, batch, seq_len, n_heads):
    rows = dn.shape[0]
    T = seq_len
    C = DN_CHUNK
    U = n_heads
    nb = min(DN_BLOCK, T // C)
    n_slots = 4 * nb * U
    groups = DN_HEADS // U
    col = lambda j: pl.BlockSpec((T, U * LANES), lambda b, h, j=j: (b, groups * j + h))
    st_spec = pl.BlockSpec((1, 2, U, DN_DK, DN_DK), lambda b, h: (b, 0, h, 0, 0))
    gate_spec = pl.BlockSpec((2, U, 1, LANES), lambda b, h: (0, h, 0, 0))
    return pl.pallas_call(
        functools.partial(_dn_kernel, seq_len=T, n_heads=U),
        grid=(batch, groups),
        in_specs=[col(0), col(1), col(2), col(3),
                  pl.BlockSpec((T, LANES), lambda b, h: (b, 0)),
                  pl.BlockSpec((3, U, 3, LANES), lambda b, h: (0, h, 0, 0)),
                  gate_spec, gate_spec,
                  pl.BlockSpec((1, LANES), lambda b, h: (0, 0)),
                  st_spec],
        out_specs=[pl.BlockSpec((T, U * LANES), lambda b, h: (b, h)), st_spec],
        out_shape=[jax.ShapeDtypeStruct((rows, 512), F32),
                   jax.ShapeDtypeStruct((batch, 2, DN_HEADS, DN_DK, DN_DK), F32)],
        scratch_shapes=([pltpu.VMEM((T, U * LANES), F32)] * 5
                        + [pltpu.VMEM((n_slots, DN_DK, LANES), BF16), pltpu.VMEM((n_slots, DN_DK, LANES), F32),
                           pltpu.VMEM((n_slots, C, LANES), BF16), pltpu.VMEM((n_slots, C, LANES), F32),
                           pltpu.VMEM((n_slots, SUBLANES, LANES), F32)]),
        compiler_params=_cparams(2),
        name="deltanet",
    )(dn, dn, dn, dn, ab, conv_w, alog, dtb, nw, s0)


def _merge_kernel(x_ref, mod_ref, a_ref, r_ref, d_ref, mg_ref, wbr_ref, wout_ref, o_ref):
    d = x_ref.shape[-1]
    merged = jnp.zeros(x_ref.shape, F32)
    for i, br in enumerate((a_ref, r_ref, d_ref)):
        merged = merged + _sigmoid(mg_ref[:, i * d:(i + 1) * d]) * _dot(br[...], wbr_ref[i])
    out = _dot(merged, wout_ref[...])
    o_ref[...] = x_ref[...] + mod_ref[0][5:6] * out


def _merge(x, mod, a, r, dn, mg, w_br, w_out, *, rows_per_cond, tm=512):
    rows, d = x.shape
    tiles_per_cond = rows_per_cond // tm
    row_spec = lambda wd: pl.BlockSpec((tm, wd), lambda i: (i, 0))
    return pl.pallas_call(
        _merge_kernel,
        grid=(rows // tm,),
        in_specs=[row_spec(d),
                  pl.BlockSpec((1, N_MOD, d), lambda i: (i // tiles_per_cond, 0, 0)),
                  row_spec(512), row_spec(512), row_spec(512), row_spec(MG_W),
                  _resident(w_br.shape), _resident(w_out.shape)],
        out_specs=row_spec(d),
        out_shape=jax.ShapeDtypeStruct((rows, d), F32),
        compiler_params=_cparams(1),
        name="merge",
    )(x, mod, a, r, dn, mg, w_br, w_out)


def _rope_tables(seq_len):
    n_freq = HEAD_DIM // 4
    inv = ROPE_THETA ** (-jnp.arange(n_freq, dtype=F32) / n_freq)
    t = jnp.arange(seq_len)
    row = (t // GRID_W).astype(F32)
    colp = (t % GRID_W).astype(F32)
    ang = jnp.concatenate([row[:, None] * inv, colp[:, None] * inv], axis=-1)
    c, s = jnp.cos(ang), jnp.sin(ang)
    cos = jnp.concatenate([c, c, c, c], axis=-1)
    sin = jnp.concatenate([-s, s, -s, s], axis=-1)
    return cos, sin


def _reorder_w_in(w):
    d = w.shape[0]
    o_da = 768 + 2048 + 1536
    o_dg = o_da + 4 * DN_HEADS
    o_mg = o_dg + 512
    return jnp.concatenate([w[:, :o_da], w[:, o_dg:o_mg], w[:, o_da:o_dg],
                            jnp.zeros((d, AB_W - 4 * DN_HEADS), w.dtype), w[:, o_mg:]], axis=1)


def kernel(x_prompt, x_sample, cache_k, cache_v, state_ret, state_delta, c, c_ctx,
           w_mod, b_mod, norm_ffn1, ffn1_w_in, ffn1_w_out, norm_mix, w_in,
           attn_q_norm, attn_k_norm, ret_norm, dn_conv, dn_a_log, dn_dt_bias, dn_norm,
           w_br_attn, w_br_ret, w_br_dn, w_out, norm_ffn2, ffn2_w_in, ffn2_w_out, norm_final):
    bp, tp, d = x_prompt.shape
    bs, ts, _ = x_sample.shape
    depth = w_mod.shape[0]
    past = cache_k.shape[2]

    conds = jnp.concatenate([c_ctx[None, :], c], axis=0)
    mod = _modulation(conds, w_mod, b_mod).reshape(depth, 1 + bs, N_MOD, d)

    ones_bd = jnp.asarray(np.kron(np.eye(ATTN_HEADS), np.ones((HEAD_DIM, HEAD_DIM))), BF16)
    ret_tabs = _retention_tables()
    rope_tabs = _rope_tables(ts)
    ret_zero = jnp.zeros((bp, 2, RET_HEADS, RET_DK, RET_DK), F32)
    dn_zero = jnp.zeros((bp, 2, DN_HEADS, DN_DK, DN_DK), F32)
    fnw = norm_final.reshape(1, d)

    groups = {
        "prompt": dict(x=x_prompt.reshape(bp * tp, d), batch=bp, seq=tp, rows_per_cond=bp * tp, rope=None,
                       dn_heads=DN_HEADS),
        "sample": dict(x=x_sample.reshape(bs * ts, d), batch=bs, seq=ts, rows_per_cond=ts, rope=rope_tabs,
                       dn_heads=1),
    }
    new_k, new_v, new_rs, new_ds = [], [], [], []
    for l in range(depth):
        w1_in, w1_out = ffn1_w_in[l].astype(BF16), ffn1_w_out[l].astype(BF16)
        w2_in, w2_out = ffn2_w_in[l].astype(BF16), ffn2_w_out[l].astype(BF16)
        w_proj = _reorder_w_in(w_in[l]).astype(BF16)
        w_br = jnp.stack([w_br_attn[l], w_br_ret[l], w_br_dn[l]]).astype(BF16)
        w_o = w_out[l].astype(BF16)
        gq = jnp.tile(attn_q_norm[l], ATTN_HEADS).reshape(1, 512)
        gk = jnp.tile(attn_k_norm[l], ATTN_KV_HEADS).reshape(1, LANES)
        conv_w = dn_conv[l].reshape(3, 3, DN_HEADS, LANES).transpose(1, 2, 0, 3)
        alog = jnp.broadcast_to(dn_a_log[l][:, :, None, None], (2, DN_HEADS, 1, LANES))
        dtb = jnp.broadcast_to(dn_dt_bias[l][:, :, None, None], (2, DN_HEADS, 1, LANES))
        for name, grp in groups.items():
            is_prompt = name == "prompt"
            x = grp["x"]
            gmod = mod[l, :1] if is_prompt else mod[l, 1:]
            rpc = grp["rows_per_cond"]
            x = _ffn(x, gmod, norm_ffn1[l].reshape(1, d), w1_in, w1_out, fnw,
                     mod_base=0, rows_per_cond=rpc, final=False)
            attn, ret, dn, ab, mg = _inproj(x, gmod, norm_mix[l].reshape(1, d), w_proj, ones_bd, gq, gk,
                                            grp["rope"], rows_per_cond=rpc, seq_len=grp["seq"])
            if is_prompt:
                a_out = _attention(attn, None, None, batch=bp, seq_len=tp)
                rs0, ds0 = ret_zero, dn_zero
            else:
                a_out = _attention(attn, cache_k[:, l].reshape(bs, past, LANES),
                                   cache_v[:, l].reshape(bs, past, LANES), batch=bs, seq_len=ts)
                rs0, ds0 = state_ret[:, l], state_delta[:, l]
            r_out, rs = _retention(ret, rs0, ret_norm[l].reshape(1, 512), ret_tabs, ones_bd,
                                   batch=grp["batch"], seq_len=grp["seq"])
            d_out, ds = _deltanet(dn, ab, conv_w, alog, dtb, dn_norm[l].reshape(1, LANES), ds0,
                                  batch=grp["batch"], seq_len=grp["seq"], n_heads=grp["dn_heads"])
            x = _merge(x, gmod, a_out, r_out, d_out, mg, w_br, w_o, rows_per_cond=rpc)
            x = _ffn(x, gmod, norm_ffn2[l].reshape(1, d), w2_in, w2_out, fnw,
                     mod_base=6, rows_per_cond=rpc, final=(l == depth - 1))
            grp["x"] = x
            if is_prompt:
                new_k.append(attn[:, 512:640].reshape(bp, tp, ATTN_KV_HEADS, HEAD_DIM))
                new_v.append(attn[:, 640:768].reshape(bp, tp, ATTN_KV_HEADS, HEAD_DIM))
                new_rs.append(rs)
                new_ds.append(ds)

    y_prompt = groups["prompt"]["x"].reshape(bp, tp, d)
    y_sample = groups["sample"]["x"].reshape(bs, ts, d)
    return (y_prompt, y_sample, jnp.stack(new_k, axis=1), jnp.stack(new_v, axis=1),
            jnp.stack(new_rs, axis=1), jnp.stack(new_ds, axis=1))
```

```python
import functools
import math

import numpy as np
import jax
import jax.numpy as jnp
from jax import lax
from jax.experimental import pallas as pl
from jax.experimental.pallas import tpu as pltpu

F32 = jnp.float32
BF16 = jnp.bfloat16

EPS = 1e-6
ROPE_THETA = 10000.0
GRID_W = 64
N_MOD = 9

ATTN_HEADS = 8
ATTN_KV_HEADS = 2
HEAD_DIM = 64
RET_HEADS = 8
RET_DK = 64
RET_CHUNK = 128
RET_DECAY_EXP_FWD = 5.0
RET_DECAY_EXP_BWD = 5.5
DN_HEADS = 4
DN_DK = 128
DN_CHUNK = 64
N_BRANCH = 3

LANES = 128
SUBLANES = 8
VMEM_LIMIT = 56 * 1024 * 1024

ATTN_W = 768
RET_W = 2048
DN_W = 2048
AB_W = 128
MG_W = 3072
IN_W = ATTN_W + RET_W + DN_W + AB_W + MG_W


def _cparams(n_axes):
    return pltpu.CompilerParams(dimension_semantics=("parallel",) * n_axes,
                                vmem_limit_bytes=VMEM_LIMIT)


def _resident(shape):
    zeros = (0,) * len(shape)
    return pl.BlockSpec(shape, lambda *_: zeros, pipeline_mode=pl.Buffered(1))


def _dot(a, b):
    return jnp.dot(a.astype(BF16), b.astype(BF16), preferred_element_type=F32)


def _dot_nt(a, b):
    return lax.dot_general(a.astype(BF16), b.astype(BF16), (((1,), (1,)), ((), ())),
                           preferred_element_type=F32)


def _dot_tn(a, b):
    return lax.dot_general(a.astype(BF16), b.astype(BF16), (((0,), (0,)), ((), ())),
                           preferred_element_type=F32)


def _sigmoid(x):
    return 1.0 / (1.0 + jnp.exp(-x))


def _silu(x):
    return x * _sigmoid(x)


def _norm_mod(x, nw, shift, scale):
    y = x * lax.rsqrt(jnp.mean(x * x, axis=-1, keepdims=True) + EPS) * nw
    return y * (1.0 + scale) + shift


def _mod_kernel(c_ref, w_ref, b_ref, o_ref):
    o_ref[0] = _dot(_silu(c_ref[...]), w_ref[0]) + b_ref[0]


def _modulation(conds, w_mod, b_mod):
    depth, d, n = w_mod.shape
    nc = conds.shape[0]
    tn = n // N_MOD
    return pl.pallas_call(
        _mod_kernel,
        grid=(depth, n // tn),
        in_specs=[pl.BlockSpec((nc, d), lambda l, j: (0, 0)),
                  pl.BlockSpec((1, d, tn), lambda l, j: (l, 0, j)),
                  pl.BlockSpec((1, 1, tn), lambda l, j: (l, 0, j))],
        out_specs=pl.BlockSpec((1, nc, tn), lambda l, j: (l, 0, j)),
        out_shape=jax.ShapeDtypeStruct((depth, nc, n), F32),
        compiler_params=_cparams(2),
        name="modulation",
    )(conds, w_mod, b_mod.reshape(depth, 1, n))


FFN_CHUNK = 256


def _ffn_rows(x, m, nw_ref, win_ref, wout_ref, fnw_ref, *, mod_base, dff, final):
    shift, scale, gate = (m[mod_base + i:mod_base + i + 1] for i in range(3))
    h = _norm_mod(x, nw_ref[...], shift, scale).astype(BF16)
    acc = jnp.zeros(x.shape, F32)
    for c in range(dff // FFN_CHUNK):
        lo = c * FFN_CHUNK
        hg = jnp.dot(h, win_ref[:, lo:lo + FFN_CHUNK], preferred_element_type=F32)
        hu = jnp.dot(h, win_ref[:, dff + lo:dff + lo + FFN_CHUNK], preferred_element_type=F32)
        a = (_silu(hg) * hu).astype(BF16)
        acc = acc + jnp.dot(a, wout_ref[lo:lo + FFN_CHUNK, :], preferred_element_type=F32)
    y = x + 0.5 * gate * acc
    if final:
        y = y * lax.rsqrt(jnp.mean(y * y, axis=-1, keepdims=True) + EPS) * fnw_ref[...]
    return y


def _ffn_kernel(x_ref, mod_ref, nw_ref, win_ref, wout_ref, fnw_ref, o_ref, *, mod_base, dff, final):
    o_ref[...] = _ffn_rows(x_ref[...], mod_ref[0], nw_ref, win_ref, wout_ref, fnw_ref,
                           mod_base=mod_base, dff=dff, final=final)


def _ffn(x, mod, nw, w_in, w_out, fnw, *, mod_base, rows_per_cond, final, tm=512):
    rows, d = x.shape
    dff = w_out.shape[0]
    tiles_per_cond = rows_per_cond // tm
    return pl.pallas_call(
        functools.partial(_ffn_kernel, mod_base=mod_base, dff=dff, final=final),
        grid=(rows // tm,),
        in_specs=[pl.BlockSpec((tm, d), lambda i: (i, 0)),
                  pl.BlockSpec((1, N_MOD, d), lambda i: (i // tiles_per_cond, 0, 0)),
                  _resident((1, d)),
                  _resident(w_in.shape),
                  _resident(w_out.shape),
                  _resident((1, d))],
        out_specs=pl.BlockSpec((tm, d), lambda i: (i, 0)),
        out_shape=jax.ShapeDtypeStruct((rows, d), F32),
        compiler_params=_cparams(1),
        name="ffn",
    )(x, mod, nw, w_in, w_out, fnw)


def _swap_halves(x):
    n = x.shape[-1]
    lane = lax.broadcasted_iota(jnp.int32, x.shape, 1)
    first = (lane % HEAD_DIM) < (HEAD_DIM // 2)
    return jnp.where(first, pltpu.roll(x, n - HEAD_DIM // 2, 1), pltpu.roll(x, HEAD_DIM // 2, 1))


def _rope(x, cos, sin):
    reps = x.shape[-1] // LANES
    c = jnp.concatenate([cos] * reps, axis=1) if reps > 1 else cos
    s = jnp.concatenate([sin] * reps, axis=1) if reps > 1 else sin
    return x * c + _swap_halves(x) * s


def _head_rms(x, ones_bd, gain):
    sq = x * x
    hi = sq.astype(BF16)
    lo = (sq - hi.astype(F32)).astype(BF16)
    ss = (jnp.dot(hi, ones_bd, preferred_element_type=F32)
          + jnp.dot(lo, ones_bd, preferred_element_type=F32))
    return x * lax.rsqrt(ss * (1.0 / HEAD_DIM) + EPS) * gain


def _inproj_kernel(*refs, rope):
    if rope:
        (x_ref, mod_ref, nw_ref, w_ref, ones_ref, gq_ref, gk_ref, cos_ref, sin_ref,
         attn_ref, ret_ref, dn_ref, ab_ref, mg_ref) = refs
    else:
        (x_ref, mod_ref, nw_ref, w_ref, ones_ref, gq_ref, gk_ref,
         attn_ref, ret_ref, dn_ref, ab_ref, mg_ref) = refs
    m = mod_ref[0]
    h = _norm_mod(x_ref[...], nw_ref[...], m[3:4], m[4:5]).astype(BF16)

    def proj(lo, width):
        return jnp.dot(h, w_ref[:, lo:lo + width], preferred_element_type=F32)

    def rot(v):
        return _rope(v, cos_ref[...], sin_ref[...]) if rope else v

    a = proj(0, ATTN_W)
    q = _head_rms(a[:, :512], ones_ref[...], gq_ref[...])
    k = _head_rms(a[:, 512:640], ones_ref[:LANES, :LANES], gk_ref[...])
    attn_ref[:, :512] = rot(q) * (HEAD_DIM ** -0.5)
    attn_ref[:, 512:640] = rot(k)
    attn_ref[:, 640:768] = a[:, 640:768]

    r = proj(ATTN_W, RET_W)
    ret_ref[:, :512] = rot(r[:, :512]) * (RET_DK ** -0.5)
    ret_ref[:, 512:1024] = rot(r[:, 512:1024])
    ret_ref[:, 1024:] = r[:, 1024:]

    dn_ref[...] = proj(ATTN_W + RET_W, DN_W)
    ab_ref[...] = proj(ATTN_W + RET_W + DN_W, AB_W)
    mg_ref[...] = proj(ATTN_W + RET_W + DN_W + AB_W, MG_W)


def _inproj(x, mod, nw, w, ones_bd, gq, gk, rope_tabs, *, rows_per_cond, seq_len, tm=256):
    rows, d = x.shape
    tiles_per_cond = rows_per_cond // tm
    tiles_per_seq = seq_len // tm
    rope = rope_tabs is not None
    in_specs = [pl.BlockSpec((tm, d), lambda i: (i, 0)),
                pl.BlockSpec((1, N_MOD, d), lambda i: (i // tiles_per_cond, 0, 0)),
                _resident((1, d)),
                _resident(w.shape),
                _resident(ones_bd.shape),
                _resident(gq.shape),
                _resident(gk.shape)]
    args = [x, mod, nw, w, ones_bd, gq, gk]
    if rope:
        in_specs += [pl.BlockSpec((tm, LANES), lambda i: (i % tiles_per_seq, 0))] * 2
        args += list(rope_tabs)
    widths = (ATTN_W, RET_W, DN_W, AB_W, MG_W)
    return pl.pallas_call(
        functools.partial(_inproj_kernel, rope=rope),
        grid=(rows // tm,),
        in_specs=in_specs,
        out_specs=[pl.BlockSpec((tm, wd), lambda i: (i, 0)) for wd in widths],
        out_shape=[jax.ShapeDtypeStruct((rows, wd), F32) for wd in widths],
        compiler_params=_cparams(1),
        name="inproj",
    )(*args)


ATTN_GROUP = ATTN_HEADS // ATTN_KV_HEADS
LOG2E = math.log2(math.e)
ATTN_CHUNK_GROUP = 8
ATTN_ROW_BLOCKS = 2


def _attn_kernel(*refs, tq, kv_chunk, n_chunks, has_ctx):
    if has_ctx:
        q_ref, k_ref, v_ref, ck_ref, cv_ref, o_ref, mx_ref, ls_ref, acc_ref, qs_ref = refs
    else:
        q_ref, k_ref, v_ref, o_ref, mx_ref, ls_ref, acc_ref, qs_ref = refs
    lane = lax.broadcasted_iota(jnp.int32, (tq, LANES), 1)
    low = lane < HEAD_DIM
    parts = []
    for hd in range(ATTN_HEADS):
        g = hd // ATTN_GROUP
        blk = q_ref[:, (hd // 2) * LANES:(hd // 2 + 1) * LANES] * LOG2E
        if hd % 2 != g:
            blk = pltpu.roll(blk, HEAD_DIM, 1)
        parts.append(jnp.where(low if g == 0 else jnp.logical_not(low), blk, 0.0))
    qs_ref[...] = jnp.concatenate(parts, axis=0).astype(BF16)

    def lane_fold(op, acc, x):
        for b in range(x.shape[1] // LANES):
            acc = op(acc, x[:, b * LANES:(b + 1) * LANES])
        return acc

    mx_ref[...] = jnp.full(mx_ref.shape, -jnp.inf, F32)
    ls_ref[...] = jnp.zeros(ls_ref.shape, F32)
    acc_ref[...] = jnp.zeros(acc_ref.shape, F32)
    rbs = qs_ref.shape[0] // ATTN_ROW_BLOCKS
    blocks = [slice(r * rbs, (r + 1) * rbs) for r in range(ATTN_ROW_BLOCKS)]

    def scores(sl, k):
        return _dot_nt(qs_ref[sl], k)

    def update(sl, s, v):
        reps = s.shape[1] // LANES
        m_old = mx_ref[sl]
        cmax = lane_fold(jnp.maximum, s[:, :LANES], s[:, LANES:])
        m_new = jnp.maximum(m_old, jnp.max(cmax, axis=-1, keepdims=True))
        alpha = jnp.exp2(m_old - m_new)
        p = jnp.exp2(s - jnp.concatenate([m_new] * reps, axis=1))
        mx_ref[sl] = m_new
        ls_ref[sl] = alpha * ls_ref[sl] + lane_fold(jnp.add, p[:, :LANES], p[:, LANES:])
        acc_ref[sl] = alpha * acc_ref[sl] + _dot(p, v)

    def chunks(kvs):
        units = [(sl, k.astype(BF16), v.astype(BF16)) for k, v in kvs for sl in blocks]
        ahead = None
        for sl, k, v in units + [(None, None, None)]:
            nxt = scores(sl, k) if sl is not None else None
            if ahead is not None:
                update(*ahead)
            ahead = (sl, nxt, v)

    group = min(ATTN_CHUNK_GROUP, n_chunks)

    def body(i, carry):
        kvs = []
        for j in range(group):
            off = pl.multiple_of((i * group + j) * kv_chunk, kv_chunk)
            kvs.append((k_ref[pl.ds(off, kv_chunk), :], v_ref[pl.ds(off, kv_chunk), :]))
        chunks(kvs)
        return carry

    tail = [(ck_ref[0], cv_ref[0])] if has_ctx else []
    if group == n_chunks:
        chunks([(k_ref[c * kv_chunk:(c + 1) * kv_chunk, :], v_ref[c * kv_chunk:(c + 1) * kv_chunk, :])
                for c in range(n_chunks)] + tail)
    else:
        lax.fori_loop(0, n_chunks // group, body, 0)
        if tail:
            chunks(tail)
    res = acc_ref[...] / jnp.sum(ls_ref[...], axis=-1, keepdims=True)

    for b in range(ATTN_HEADS // 2):
        g = (2 * b) // ATTN_GROUP
        even = res[2 * b * tq:(2 * b + 1) * tq]
        odd = res[(2 * b + 1) * tq:(2 * b + 2) * tq]
        if g == 0:
            odd = pltpu.roll(odd, HEAD_DIM, 1)
        else:
            even = pltpu.roll(even, HEAD_DIM, 1)
        o_ref[:, b * LANES:(b + 1) * LANES] = jnp.where(low, even, odd)


def _attention(attn, ctx_k, ctx_v, *, batch, seq_len, tq=128):
    rows = attn.shape[0]
    kv_chunk = min(seq_len, 512)
    n_chunks = seq_len // kv_chunk
    qt = seq_len // tq
    stacked = ATTN_HEADS * tq
    has_ctx = ctx_k is not None
    in_specs = [pl.BlockSpec((tq, 512), lambda b, i: (b * qt + i, 0)),
                pl.BlockSpec((seq_len, LANES), lambda b, i: (b, 4)),
                pl.BlockSpec((seq_len, LANES), lambda b, i: (b, 5))]
    args = [attn, attn, attn]
    if has_ctx:
        past = ctx_k.shape[1]
        in_specs += [pl.BlockSpec((1, past, LANES), lambda b, i: (b, 0, 0))] * 2
        args += [ctx_k, ctx_v]
    scratch = [pltpu.VMEM((stacked, LANES), F32)] * 3 + [pltpu.VMEM((stacked, LANES), BF16)]
    return pl.pallas_call(
        functools.partial(_attn_kernel, tq=tq, kv_chunk=kv_chunk, n_chunks=n_chunks, has_ctx=has_ctx),
        grid=(batch, qt),
        in_specs=in_specs,
        out_specs=pl.BlockSpec((tq, 512), lambda b, i: (b * qt + i, 0)),
        out_shape=jax.ShapeDtypeStruct((rows, 512), F32),
        scratch_shapes=scratch,
        compiler_params=_cparams(2),
        name="attention",
    )(*args)


def _retention_tables():
    C = RET_CHUNK
    h = np.arange(RET_HEADS, dtype=np.float64)
    pos = np.arange(C, dtype=np.float64)
    diff = pos[:, None] - pos[None, :]
    inner, qd, kd, cd = [], [], [], []
    for direction, expo in enumerate((RET_DECAY_EXP_FWD, RET_DECAY_EXP_BWD)):
        lg = np.log1p(-np.exp2(-expo - h))[:, None, None]
        if direction == 0:
            inner.append(np.where(diff >= 0, np.exp(lg * np.maximum(diff, 0.0)), 0.0))
            qd.append(np.exp(lg[:, :, 0] * (pos + 1.0)))
            kd.append(np.exp(lg[:, :, 0] * (C - 1.0 - pos)))
        else:
            inner.append(np.where(diff <= 0, np.exp(lg * np.maximum(-diff, 0.0)), 0.0))
            qd.append(np.exp(lg[:, :, 0] * (C - pos)))
            kd.append(np.exp(lg[:, :, 0] * pos))
        cd.append(np.exp(lg[:, 0, 0] * C))
    inner = np.stack(inner, axis=1)
    rowdec = np.stack([np.stack(qd, 1), np.stack(kd, 1)], axis=2)
    rowdec = np.repeat(rowdec[..., None], RET_DK, axis=-1)
    rowdec = np.concatenate([rowdec[0::2], rowdec[1::2]], axis=-1)
    cd = np.stack(cd, axis=1)
    block = np.kron(np.eye(2), np.ones((RET_DK, RET_DK)))
    per_row = np.repeat(np.stack([cd[0::2], cd[1::2]], axis=-1), RET_DK, axis=-1)
    cd = per_row[:, :, :, None] * block
    return (jnp.asarray(inner, F32), jnp.asarray(rowdec, F32), jnp.asarray(cd, F32))


RET_BLOCK = 8


def _split_bf16(x):
    hi = x.astype(BF16)
    return hi, (x - hi.astype(F32)).astype(BF16)


def _ret_kernel(q_ref, k_ref, v_ref, g_ref, inner_ref, dec_ref, cd_ref, ones_ref, s0_ref, nw_ref,
                o_ref, st_ref, *, n_chunks, n_pairs):
    C = RET_CHUNK
    D = RET_DK
    nb = min(RET_BLOCK, n_chunks)
    lanes_of = [slice(p * LANES, (p + 1) * LANES) for p in range(n_pairs)]
    lane = lax.broadcasted_iota(jnp.int32, (C, LANES), 1)
    first = lane < D
    ones = ones_ref[...]

    def head_mean(x):
        hi, lo = _split_bf16(x)
        return (jnp.dot(hi, ones, preferred_element_type=F32)
                + jnp.dot(lo, ones, preferred_element_type=F32)) * (1.0 / D)

    def sweep(direction):
        def body(i, states):
            order = [i * nb + j for j in range(nb)]
            if direction == 1:
                order = [n_chunks - 1 - c for c in order]
            units = [(p, pl.multiple_of(c * C, C)) for p in range(n_pairs) for c in order]
            n = len(units)
            q = [q_ref[pl.ds(r, C), lanes_of[p]] for p, r in units]
            k = [k_ref[pl.ds(r, C), lanes_of[p]] for p, r in units]
            v = [v_ref[pl.ds(r, C), lanes_of[p]] for p, r in units]
            if direction == 1:
                prev = [o_ref[pl.ds(r, C), lanes_of[p]] for p, r in units]
                gate = [g_ref[pl.ds(r, C), lanes_of[p]] for p, r in units]
            att = [jnp.concatenate(
                [_dot_nt(jnp.where(first, q[e], 0.0), k[e]) * inner_ref[2 * units[e][0], direction],
                 _dot_nt(jnp.where(first, 0.0, q[e]), k[e]) * inner_ref[2 * units[e][0] + 1, direction]],
                axis=1) for e in range(n)]
            kv = [_dot_tn(k[e] * dec_ref[units[e][0], direction, 1], v[e]) for e in range(n)]
            seen = []
            states = list(states)
            for e, (p, _) in enumerate(units):
                cd = cd_ref[p, direction]
                seen.append(states[p])
                states[p] = states[p] * cd + jnp.where(cd != 0.0, kv[e], 0.0)
            v2 = [jnp.concatenate([jnp.where(first, v[e], 0.0), jnp.where(first, 0.0, v[e])], axis=0)
                  for e in range(n)]
            o = [_dot(att[e], v2[e]) + _dot(q[e] * dec_ref[units[e][0], direction, 0], seen[e])
                 for e in range(n)]
            if direction == 1:
                o = [o[e] + prev[e] for e in range(n)]
                mean = [head_mean(o[e]) for e in range(n)]
                d = [o[e] - mean[e] for e in range(n)]
                var = [head_mean(d[e] * d[e]) for e in range(n)]
                o = [d[e] * lax.rsqrt(var[e] + EPS) * nw_ref[:, lanes_of[units[e][0]]] * _silu(gate[e])
                     for e in range(n)]
            for e, (p, r) in enumerate(units):
                o_ref[pl.ds(r, C), lanes_of[p]] = o[e]
            return tuple(states)

        zero = jnp.zeros((D, D), F32)
        states = tuple(
            jnp.concatenate([jnp.concatenate([s0_ref[0, direction, 2 * p], zero], axis=1),
                             jnp.concatenate([zero, s0_ref[0, direction, 2 * p + 1]], axis=1)], axis=0)
            for p in range(n_pairs))
        states = lax.fori_loop(0, n_chunks // nb, body, states)
        for p in range(n_pairs):
            st_ref[0, direction, 2 * p] = states[p][:D, :D]
            st_ref[0, direction, 2 * p + 1] = states[p][D:, D:]

    sweep(0)
    sweep(1)


def _retention(ret, s0, nw, tables, ones_bd, *, batch, seq_len, n_pairs):
    rows = ret.shape[0]
    inner, rowdec, cd = tables
    C = RET_CHUNK
    U = n_pairs
    groups = RET_HEADS // 2 // U
    col = lambda j: pl.BlockSpec((seq_len, U * LANES), lambda b, hp, j=j: (b, groups * j + hp))
    st_spec = pl.BlockSpec((1, 2, 2 * U, RET_DK, RET_DK), lambda b, hp: (b, 0, hp, 0, 0))
    return pl.pallas_call(
        functools.partial(_ret_kernel, n_chunks=seq_len // C, n_pairs=U),
        grid=(batch, groups),
        in_specs=[col(0), col(1), col(2), col(3),
                  pl.BlockSpec((2 * U, 2, C, C), lambda b, hp: (hp, 0, 0, 0)),
                  pl.BlockSpec((U, 2, 2, C, LANES), lambda b, hp: (hp, 0, 0, 0, 0)),
                  pl.BlockSpec((U, 2, LANES, LANES), lambda b, hp: (hp, 0, 0, 0)),
                  pl.BlockSpec((LANES, LANES), lambda b, hp: (0, 0)),
                  st_spec,
                  pl.BlockSpec((1, U * LANES), lambda b, hp: (0, hp))],
        out_specs=[pl.BlockSpec((seq_len, U * LANES), lambda b, hp: (b, hp)), st_spec],
        out_shape=[jax.ShapeDtypeStruct((rows, 512), F32),
                   jax.ShapeDtypeStruct((batch, 2, RET_HEADS, RET_DK, RET_DK), F32)],
        compiler_params=_cparams(2),
        name="retention",
    )(ret, ret, ret, ret, inner, rowdec, cd, ones_bd, s0, nw)


DN_BLOCK = 8
SOLVE_BASE = 8
CONV_BLOCK = 256


def _dn_kernel(q_ref, k_ref, v_ref, g_ref, ab_ref, cw_ref, alog_ref, dtb_ref, nw_ref, s0_ref,
               o_ref, st_ref, qs_ref, ks_ref, vs_ref, of_ref, ob_ref,
               wp_ref, bm_ref, qp_ref, op_ref, dec_ref, *, seq_len, n_heads):
    C = DN_CHUNK
    T = seq_len
    U = n_heads
    n_chunks = T // C
    head0 = pl.program_id(1) * U
    lanes_of = [slice(hh * LANES, (hh + 1) * LANES) for hh in range(U)]

    blk = min(CONV_BLOCK, T)
    row = lax.broadcasted_iota(jnp.int32, (blk, LANES), 0)
    tensors = [(src, dst, t, hh) for hh in range(U)
               for t, (src, dst) in enumerate(((q_ref, qs_ref), (k_ref, ks_ref), (v_ref, vs_ref)))]

    def conv(i, carry):
        r0 = pl.multiple_of(i * blk, blk)
        above = pl.multiple_of(jnp.maximum(r0 - SUBLANES, 0), SUBLANES)
        below = pl.multiple_of(jnp.minimum(r0 + blk, T - SUBLANES), SUBLANES)
        x = [src[pl.ds(r0, blk), lanes_of[hh]] for src, _, _, hh in tensors]
        up = [jnp.where(r0 > 0, src[pl.ds(above, SUBLANES), lanes_of[hh]][SUBLANES - 1:], 0.0)
              for src, _, _, hh in tensors]
        dn = [jnp.where(r0 + blk < T, src[pl.ds(below, SUBLANES), lanes_of[hh]][:1], 0.0)
              for src, _, _, hh in tensors]
        n = len(tensors)
        prev = [jnp.where(row == 0, up[e], pltpu.roll(x[e], 1, 0)) for e in range(n)]
        nxt = [jnp.where(row == blk - 1, dn[e], pltpu.roll(x[e], blk - 1, 0)) for e in range(n)]
        w = [cw_ref[t, hh] for _, _, t, hh in tensors]
        y = [_silu(w[e][0:1] * prev[e] + w[e][1:2] * x[e] + w[e][2:3] * nxt[e]) for e in range(n)]
        for e, (_, dst, t, hh) in enumerate(tensors):
            if t < 2:
                scale = lax.rsqrt(jnp.sum(y[e] * y[e], axis=-1, keepdims=True) + EPS)
                y[e] = y[e] * (scale * (DN_DK ** -0.5) if t == 0 else scale)
        for e, (_, dst, t, hh) in enumerate(tensors):
            dst[pl.ds(r0, blk), lanes_of[hh]] = y[e]
        return carry

    lax.fori_loop(0, T // blk, conv, 0)

    ri = lax.broadcasted_iota(jnp.int32, (C, C), 0)
    ci = lax.broadcasted_iota(jnp.int32, (C, C), 1)
    eye = ri == ci
    eye_f = eye.astype(F32)
    lane = lax.broadcasted_iota(jnp.int32, (1, LANES), 1)
    masks = ((ri >= ci, ri > ci), (ri <= ci, ri < ci))
    incl_bf = tuple(m[0].astype(F32).astype(BF16) for m in masks)
    base_blocks = (ri // SOLVE_BASE) == (ci // SOLVE_BASE)
    level_masks = []
    size = SOLVE_BASE
    while size < C:
        level_masks.append(jnp.logical_and((ri // (2 * size)) == (ci // (2 * size)),
                                           (ri // size) != (ci // size)))
        size *= 2

    def load(r0, hh):
        return (ab_ref[pl.ds(r0, C), :],) + tuple(ref[pl.ds(r0, C), lanes_of[hh]]
                                                   for ref in (qs_ref, ks_ref, vs_ref))

    def prep(operands, direction, hh):
        ab, q, k, v = operands
        incl, strict = masks[direction]
        sel_a = (lane == direction * DN_HEADS + head0 + hh).astype(F32)
        sel_b = (lane == 2 * DN_HEADS + direction * DN_HEADS + head0 + hh).astype(F32)
        da = jnp.sum(ab * sel_a, axis=-1, keepdims=True)
        db = jnp.sum(ab * sel_b, axis=-1, keepdims=True)
        z = da + dtb_ref[direction, hh]
        softplus = jnp.maximum(z, 0.0) + jnp.log1p(jnp.exp(-jnp.abs(z)))
        g = -jnp.exp(alog_ref[direction, hh]) * softplus
        beta = _sigmoid(db)
        kb = k * beta
        g1 = g.astype(BF16)
        g2 = (g - g1.astype(F32)).astype(BF16)
        g3 = (g - g1.astype(F32) - g2.astype(F32)).astype(BF16)
        G12 = _dot(incl_bf[direction], jnp.concatenate([g1, g2], axis=1))
        G = G12[:, :LANES] + G12[:, LANES:] + _dot(incl_bf[direction], g3)
        kk = _dot_nt(kb, k)
        qk = _dot_nt(q, k)
        yield
        Gc = G[:, :C]
        Grow = jnp.sum(jnp.where(eye, Gc, 0.0), axis=0, keepdims=True)
        L = jnp.where(incl, jnp.exp(jnp.where(incl, Gc - Grow, 0.0)), 0.0)
        N = jnp.where(strict, -(kk * L), 0.0)
        P = jnp.where(base_blocks, N, 0.0)
        Tm = eye_f + P
        P = _dot(P, P)
        yield
        for _ in range(int(math.log2(SOLVE_BASE)) - 2):
            Tm, P = Tm + _dot(Tm, P), _dot(P, P)
            yield
        Tm = Tm + _dot(Tm, P)
        yield
        for off_blocks in level_masks:
            TX = _dot(Tm, jnp.where(off_blocks, N, 0.0))
            yield
            Tm = Tm + _dot(TX, Tm)
            yield
        eG = jnp.exp(G)
        g_last = G[C - 1:C] if direction == 0 else G[0:1]
        wu = _dot(Tm, jnp.concatenate([kb * eG, v * beta], axis=1))
        yield
        kd = k * jnp.exp(g_last - G)
        att = qk * L
        kd_wu = _dot_tn(kd, wu)
        att_wu = _dot(att, wu)
        return ((-kd_wu[:, :LANES]).astype(BF16), kd_wu[:, LANES:],
                (q * eG - att_wu[:, :LANES]).astype(BF16), att_wu[:, LANES:],
                jnp.broadcast_to(jnp.exp(g_last), (SUBLANES, LANES)))

    def run_staged(generators):
        results = [None] * len(generators)
        live = list(enumerate(generators))
        while live:
            still = []
            for idx, gen in live:
                try:
                    next(gen)
                    still.append((idx, gen))
                except StopIteration as done:
                    results[idx] = done.value
            live = still
        return results

    slots = (wp_ref, bm_ref, qp_ref, op_ref, dec_ref)
    nb = min(DN_BLOCK, n_chunks)
    n_blocks = n_chunks // nb
    per_half = 2 * nb * U

    def block_rows(i):
        return ([pl.multiple_of((i * nb + j) * C, C) for j in range(nb)]
                + [pl.multiple_of((n_chunks - 1 - (i * nb + j)) * C, C) for j in range(nb)])

    def prep_block(i):
        rows = block_rows(i)
        return [prep(load(r, hh), s // nb, hh) for hh in range(U) for s, r in enumerate(rows)]

    def store_block(prepared, base):
        for e in range(per_half):
            for ref, val in zip(slots, prepared[e]):
                ref[base + e] = val

    def recurrence(states, base):
        states = list(states)
        outs = []
        for j in range(nb):
            step_out = []
            for hh in range(U):
                for direction in range(2):
                    e = base + hh * 2 * nb + direction * nb + j
                    S = states[2 * hh + direction]
                    S16 = S.astype(BF16)
                    step_out.append(_dot(qp_ref[e], S16) + op_ref[e])
                    states[2 * hh + direction] = S * dec_ref[e, 0:1] + _dot(wp_ref[e], S16) + bm_ref[e]
            outs.append(step_out)
            yield
        return outs, tuple(states)

    def store_outputs(i, outs):
        rows = block_rows(i)
        for j in range(nb):
            for hh in range(U):
                of_ref[pl.ds(rows[j], C), lanes_of[hh]] = outs[j][2 * hh]
                ob_ref[pl.ds(rows[nb + j], C), lanes_of[hh]] = outs[j][2 * hh + 1]

    state = tuple(s0_ref[0, direction, hh] for hh in range(U) for direction in range(2))
    if n_blocks == 1:
        store_block(run_staged(prep_block(0)), 0)
        (outs, state), = run_staged([recurrence(state, 0)])
        store_outputs(0, outs)
    else:
        store_block(run_staged(prep_block(0)), 0)

        def body(i, carry):
            cur = (i % 2) * per_half
            nxt = per_half - cur
            ahead = jnp.minimum(i + 1, n_blocks - 1)
            *prepared, (outs, carry) = run_staged(prep_block(ahead) + [recurrence(carry, cur)])
            store_outputs(i, outs)
            store_block(prepared, nxt)
            return carry

        state = lax.fori_loop(0, n_blocks, body, state)
    for hh in range(U):
        for direction in range(2):
            st_ref[0, direction, hh] = state[2 * hh + direction]

    def fin(i, carry):
        r0 = pl.multiple_of(i * blk, blk)
        for hh in range(U):
            o = of_ref[pl.ds(r0, blk), lanes_of[hh]] + ob_ref[pl.ds(r0, blk), lanes_of[hh]]
            y = o * lax.rsqrt(jnp.mean(o * o, axis=-1, keepdims=True) + EPS) * nw_ref[...]
            o_ref[pl.ds(r0, blk), lanes_of[hh]] = y * _silu(g_ref[pl.ds(r0, blk), lanes_of[hh]])
        return carry

    lax.fori_loop(0, T // blk, fin, 0)


def _deltanet(dn, ab, conv_w, alog, dtb, nw, s0, *, batch, seq_len, n_heads):
    rows = dn.shape[0]
    T = seq_len
    C = DN_CHUNK
    U = n_heads
    nb = min(DN_BLOCK, T // C)
    n_slots = 4 * nb * U
    groups = DN_HEADS // U
    col = lambda j: pl.BlockSpec((T, U * LANES), lambda b, h, j=j: (b, groups * j + h))
    st_spec = pl.BlockSpec((1, 2, U, DN_DK, DN_DK), lambda b, h: (b, 0, h, 0, 0))
    gate_spec = pl.BlockSpec((2, U, 1, LANES), lambda b, h: (0, h, 0, 0))
    return pl.pallas_call(
        functools.partial(_dn_kernel, seq_len=T, n_heads=U),
        grid=(batch, groups),
        in_specs=[col(0), col(1), col(2), col(3),
                  pl.BlockSpec((T, LANES), lambda b, h: (b, 0)),
                  pl.BlockSpec((3, U, 3, LANES), lambda b, h: (0, h, 0, 0)),
                  gate_spec, gate_spec,
                  pl.BlockSpec((1, LANES), lambda b, h: (0, 0)),
                  st_spec],
        out_specs=[pl.BlockSpec((T, U * LANES), lambda b, h: (b, h)), st_spec],
        out_shape=[jax.ShapeDtypeStruct((rows, 512), F32),
                   jax.ShapeDtypeStruct((batch, 2, DN_HEADS, DN_DK, DN_DK), F32)],
        scratch_shapes=([pltpu.VMEM((T, U * LANES), F32)] * 5
                        + [pltpu.VMEM((n_slots, DN_DK, LANES), BF16), pltpu.VMEM((n_slots, DN_DK, LANES), F32),
                           pltpu.VMEM((n_slots, C, LANES), BF16), pltpu.VMEM((n_slots, C, LANES), F32),
                           pltpu.VMEM((n_slots, SUBLANES, LANES), F32)]),
        compiler_params=_cparams(2),
        name="deltanet",
    )(dn, dn, dn, dn, ab, conv_w, alog, dtb, nw, s0)


def _merge_ffn_kernel(x_ref, mod_ref, a_ref, r_ref, d_ref, mg_ref, wbr_ref, wo_ref,
                      nw_ref, win_ref, wout_ref, fnw_ref, o_ref, *, dff, final):
    d = x_ref.shape[-1]
    m = mod_ref[0]
    merged = jnp.zeros(x_ref.shape, F32)
    for i, br in enumerate((a_ref, r_ref, d_ref)):
        merged = merged + _sigmoid(mg_ref[:, i * d:(i + 1) * d]) * _dot(br[...], wbr_ref[i])
    x = x_ref[...] + m[5:6] * _dot(merged, wo_ref[...])
    o_ref[...] = _ffn_rows(x, m, nw_ref, win_ref, wout_ref, fnw_ref, mod_base=6, dff=dff, final=final)


def _merge_ffn(x, mod, a, r, dn, mg, w_br, w_o, nw, w_in, w_out, fnw, *, rows_per_cond, final, tm=512):
    rows, d = x.shape
    dff = w_out.shape[0]
    tiles_per_cond = rows_per_cond // tm
    row_spec = lambda wd: pl.BlockSpec((tm, wd), lambda i: (i, 0))
    return pl.pallas_call(
        functools.partial(_merge_ffn_kernel, dff=dff, final=final),
        grid=(rows // tm,),
        in_specs=[row_spec(d),
                  pl.BlockSpec((1, N_MOD, d), lambda i: (i // tiles_per_cond, 0, 0)),
                  row_spec(512), row_spec(512), row_spec(512), row_spec(MG_W),
                  _resident(w_br.shape), _resident(w_o.shape),
                  _resident((1, d)), _resident(w_in.shape), _resident(w_out.shape), _resident((1, d))],
        out_specs=row_spec(d),
        out_shape=jax.ShapeDtypeStruct((rows, d), F32),
        compiler_params=_cparams(1),
        name="merge_ffn",
    )(x, mod, a, r, dn, mg, w_br, w_o, nw, w_in, w_out, fnw)


def _rope_tables(seq_len):
    n_freq = HEAD_DIM // 4
    inv = ROPE_THETA ** (-jnp.arange(n_freq, dtype=F32) / n_freq)
    t = jnp.arange(seq_len)
    row = (t // GRID_W).astype(F32)
    colp = (t % GRID_W).astype(F32)
    ang = jnp.concatenate([row[:, None] * inv, colp[:, None] * inv], axis=-1)
    c, s = jnp.cos(ang), jnp.sin(ang)
    cos = jnp.concatenate([c, c, c, c], axis=-1)
    sin = jnp.concatenate([-s, s, -s, s], axis=-1)
    return cos, sin


def _reorder_w_in(w):
    d = w.shape[0]
    o_da = 768 + 2048 + 1536
    o_dg = o_da + 4 * DN_HEADS
    o_mg = o_dg + 512
    parts = [w[:, :o_da], w[:, o_dg:o_mg], w[:, o_da:o_dg],
             jnp.zeros((d, AB_W - 4 * DN_HEADS), w.dtype), w[:, o_mg:]]
    return jnp.concatenate([p.astype(BF16) for p in parts], axis=1)


def kernel(x_prompt, x_sample, cache_k, cache_v, state_ret, state_delta, c, c_ctx,
           w_mod, b_mod, norm_ffn1, ffn1_w_in, ffn1_w_out, norm_mix, w_in,
           attn_q_norm, attn_k_norm, ret_norm, dn_conv, dn_a_log, dn_dt_bias, dn_norm,
           w_br_attn, w_br_ret, w_br_dn, w_out, norm_ffn2, ffn2_w_in, ffn2_w_out, norm_final):
    bp, tp, d = x_prompt.shape
    bs, ts, _ = x_sample.shape
    depth = w_mod.shape[0]
    past = cache_k.shape[2]

    conds = jnp.concatenate([c_ctx[None, :], c], axis=0)
    mod = _modulation(conds, w_mod, b_mod).reshape(depth, 1 + bs, N_MOD, d)

    ones_bd = jnp.asarray(np.kron(np.eye(ATTN_HEADS), np.ones((HEAD_DIM, HEAD_DIM))), BF16)
    ret_tabs = _retention_tables()
    rope_tabs = _rope_tables(ts)
    ret_zero = jnp.zeros((bp, 2, RET_HEADS, RET_DK, RET_DK), F32)
    dn_zero = jnp.zeros((bp, 2, DN_HEADS, DN_DK, DN_DK), F32)
    fnw = norm_final.reshape(1, d)

    groups = {
        "prompt": dict(x=x_prompt.reshape(bp * tp, d), batch=bp, seq=tp, rows_per_cond=bp * tp, rope=None,
                       dn_heads=DN_HEADS, ret_pairs=RET_HEADS // 2),
        "sample": dict(x=x_sample.reshape(bs * ts, d), batch=bs, seq=ts, rows_per_cond=ts, rope=rope_tabs,
                       dn_heads=1, ret_pairs=1),
    }
    new_k, new_v, new_rs, new_ds = [], [], [], []
    for l in range(depth):
        w1_in, w1_out = ffn1_w_in[l].astype(BF16), ffn1_w_out[l].astype(BF16)
        w2_in, w2_out = ffn2_w_in[l].astype(BF16), ffn2_w_out[l].astype(BF16)
        w_proj = _reorder_w_in(w_in[l])
        w_br = jnp.stack([w_br_attn[l], w_br_ret[l], w_br_dn[l]]).astype(BF16)
        w_o = w_out[l].astype(BF16)
        gq = jnp.tile(attn_q_norm[l], ATTN_HEADS).reshape(1, 512)
        gk = jnp.tile(attn_k_norm[l], ATTN_KV_HEADS).reshape(1, LANES)
        conv_w = dn_conv[l].reshape(3, 3, DN_HEADS, LANES).transpose(1, 2, 0, 3)
        alog = jnp.broadcast_to(dn_a_log[l][:, :, None, None], (2, DN_HEADS, 1, LANES))
        dtb = jnp.broadcast_to(dn_dt_bias[l][:, :, None, None], (2, DN_HEADS, 1, LANES))
        for name, grp in groups.items():
            is_prompt = name == "prompt"
            x = grp["x"]
            gmod = mod[l, :1] if is_prompt else mod[l, 1:]
            rpc = grp["rows_per_cond"]
            x = _ffn(x, gmod, norm_ffn1[l].reshape(1, d), w1_in, w1_out, fnw,
                     mod_base=0, rows_per_cond=rpc, final=False)
            attn, ret, dn, ab, mg = _inproj(x, gmod, norm_mix[l].reshape(1, d), w_proj, ones_bd, gq, gk,
                                            grp["rope"], rows_per_cond=rpc, seq_len=grp["seq"])
            if is_prompt:
                a_out = _attention(attn, None, None, batch=bp, seq_len=tp)
                rs0, ds0 = ret_zero, dn_zero
            else:
                a_out = _attention(attn, cache_k[:, l].reshape(bs, past, LANES),
                                   cache_v[:, l].reshape(bs, past, LANES), batch=bs, seq_len=ts)
                rs0, ds0 = state_ret[:, l], state_delta[:, l]
            r_out, rs = _retention(ret, rs0, ret_norm[l].reshape(1, 512), ret_tabs, ones_bd,
                                   batch=grp["batch"], seq_len=grp["seq"], n_pairs=grp["ret_pairs"])
            d_out, ds = _deltanet(dn, ab, conv_w, alog, dtb, dn_norm[l].reshape(1, LANES), ds0,
                                  batch=grp["batch"], seq_len=grp["seq"], n_heads=grp["dn_heads"])
            x = _merge_ffn(x, gmod, a_out, r_out, d_out, mg, w_br, w_o, norm_ffn2[l].reshape(1, d),
                           w2_in, w2_out, fnw, rows_per_cond=rpc, final=(l == depth - 1))
            grp["x"] = x
            if is_prompt:
                new_k.append(attn[:, 512:640].reshape(bp, tp, ATTN_KV_HEADS, HEAD_DIM))
                new_v.append(attn[:, 640:768].reshape(bp, tp, ATTN_KV_HEADS, HEAD_DIM))
                new_rs.append(rs)
                new_ds.append(ds)

    y_prompt = groups["prompt"]["x"].reshape(bp, tp, d)
    y_sample = groups["sample"]["x"].reshape(bs, ts, d)
    return (y_prompt, y_sample, jnp.stack(new_k, axis=1), jnp.stack(new_v, axis=1),
            jnp.stack(new_rs, axis=1), jnp.stack(new_ds, axis=1))
```

```python
import functools
import math

import numpy as np
import jax
import jax.numpy as jnp
from jax import lax
from jax.experimental import pallas as pl
from jax.experimental.pallas import tpu as pltpu

F32 = jnp.float32
BF16 = jnp.bfloat16

EPS = 1e-6
ROPE_THETA = 10000.0
GRID_W = 64
N_MOD = 9

ATTN_HEADS = 8
ATTN_KV_HEADS = 2
HEAD_DIM = 64
RET_HEADS = 8
RET_DK = 64
RET_CHUNK = 128
RET_DECAY_EXP_FWD = 5.0
RET_DECAY_EXP_BWD = 5.5
DN_HEADS = 4
DN_DK = 128
DN_CHUNK = 64
N_BRANCH = 3

LANES = 128
SUBLANES = 8
VMEM_LIMIT = 56 * 1024 * 1024

ATTN_W = 768
RET_W = 2048
DN_W = 2048
AB_W = 128
MG_W = 3072
IN_W = ATTN_W + RET_W + DN_W + AB_W + MG_W


def _cparams(n_axes):
    return pltpu.CompilerParams(dimension_semantics=("parallel",) * n_axes,
                                vmem_limit_bytes=VMEM_LIMIT)


def _resident(shape):
    zeros = (0,) * len(shape)
    return pl.BlockSpec(shape, lambda *_: zeros, pipeline_mode=pl.Buffered(1))


def _dot(a, b):
    return jnp.dot(a.astype(BF16), b.astype(BF16), preferred_element_type=F32)


def _dot_nt(a, b):
    return lax.dot_general(a.astype(BF16), b.astype(BF16), (((1,), (1,)), ((), ())),
                           preferred_element_type=F32)


def _dot_tn(a, b):
    return lax.dot_general(a.astype(BF16), b.astype(BF16), (((0,), (0,)), ((), ())),
                           preferred_element_type=F32)


def _sigmoid(x):
    return 1.0 / (1.0 + jnp.exp(-x))


def _silu(x):
    return x * _sigmoid(x)


def _norm_mod(x, nw, shift, scale):
    y = x * lax.rsqrt(jnp.mean(x * x, axis=-1, keepdims=True) + EPS) * nw
    return y * (1.0 + scale) + shift


def _mod_kernel(c_ref, w_ref, b_ref, o_ref):
    o_ref[0] = _dot(_silu(c_ref[...]), w_ref[0]) + b_ref[0]


def _modulation(conds, w_mod, b_mod):
    depth, d, n = w_mod.shape
    nc = conds.shape[0]
    tn = n // N_MOD
    return pl.pallas_call(
        _mod_kernel,
        grid=(depth, n // tn),
        in_specs=[pl.BlockSpec((nc, d), lambda l, j: (0, 0)),
                  pl.BlockSpec((1, d, tn), lambda l, j: (l, 0, j)),
                  pl.BlockSpec((1, 1, tn), lambda l, j: (l, 0, j))],
        out_specs=pl.BlockSpec((1, nc, tn), lambda l, j: (l, 0, j)),
        out_shape=jax.ShapeDtypeStruct((depth, nc, n), F32),
        compiler_params=_cparams(2),
        name="modulation",
    )(conds, w_mod, b_mod.reshape(depth, 1, n))


FFN_CHUNK = 256


def _ffn_rows(x, m, nw_ref, win_ref, wout_ref, fnw_ref, *, mod_base, dff, final):
    shift, scale, gate = (m[mod_base + i:mod_base + i + 1] for i in range(3))
    h = _norm_mod(x, nw_ref[...], shift, scale).astype(BF16)
    acc = jnp.zeros(x.shape, F32)
    for c in range(dff // FFN_CHUNK):
        lo = c * FFN_CHUNK
        hg = jnp.dot(h, win_ref[:, lo:lo + FFN_CHUNK], preferred_element_type=F32)
        hu = jnp.dot(h, win_ref[:, dff + lo:dff + lo + FFN_CHUNK], preferred_element_type=F32)
        a = (_silu(hg) * hu).astype(BF16)
        acc = acc + jnp.dot(a, wout_ref[lo:lo + FFN_CHUNK, :], preferred_element_type=F32)
    y = x + 0.5 * gate * acc
    if final:
        y = y * lax.rsqrt(jnp.mean(y * y, axis=-1, keepdims=True) + EPS) * fnw_ref[...]
    return y


def _ffn_kernel(x_ref, mod_ref, nw_ref, win_ref, wout_ref, fnw_ref, o_ref, *, mod_base, dff, final):
    o_ref[...] = _ffn_rows(x_ref[...], mod_ref[0], nw_ref, win_ref, wout_ref, fnw_ref,
                           mod_base=mod_base, dff=dff, final=final)


def _ffn(x, mod, nw, w_in, w_out, fnw, *, mod_base, rows_per_cond, final, tm=512):
    rows, d = x.shape
    dff = w_out.shape[0]
    tiles_per_cond = rows_per_cond // tm
    return pl.pallas_call(
        functools.partial(_ffn_kernel, mod_base=mod_base, dff=dff, final=final),
        grid=(rows // tm,),
        in_specs=[pl.BlockSpec((tm, d), lambda i: (i, 0)),
                  pl.BlockSpec((1, N_MOD, d), lambda i: (i // tiles_per_cond, 0, 0)),
                  _resident((1, d)),
                  _resident(w_in.shape),
                  _resident(w_out.shape),
                  _resident((1, d))],
        out_specs=pl.BlockSpec((tm, d), lambda i: (i, 0)),
        out_shape=jax.ShapeDtypeStruct((rows, d), F32),
        compiler_params=_cparams(1),
        name="ffn",
    )(x, mod, nw, w_in, w_out, fnw)


def _swap_halves(x):
    n = x.shape[-1]
    lane = lax.broadcasted_iota(jnp.int32, x.shape, 1)
    first = (lane % HEAD_DIM) < (HEAD_DIM // 2)
    return jnp.where(first, pltpu.roll(x, n - HEAD_DIM // 2, 1), pltpu.roll(x, HEAD_DIM // 2, 1))


def _rope(x, cos, sin):
    reps = x.shape[-1] // LANES
    c = jnp.concatenate([cos] * reps, axis=1) if reps > 1 else cos
    s = jnp.concatenate([sin] * reps, axis=1) if reps > 1 else sin
    return x * c + _swap_halves(x) * s


def _head_rms(x, ones_bd, gain):
    sq = x * x
    hi = sq.astype(BF16)
    lo = (sq - hi.astype(F32)).astype(BF16)
    ss = (jnp.dot(hi, ones_bd, preferred_element_type=F32)
          + jnp.dot(lo, ones_bd, preferred_element_type=F32))
    return x * lax.rsqrt(ss * (1.0 / HEAD_DIM) + EPS) * gain


def _inproj_kernel(*refs, rope):
    if rope:
        (x_ref, mod_ref, nw_ref, w_ref, ones_ref, gq_ref, gk_ref, cos_ref, sin_ref,
         attn_ref, ret_ref, dn_ref, ab_ref, mg_ref) = refs
    else:
        (x_ref, mod_ref, nw_ref, w_ref, ones_ref, gq_ref, gk_ref,
         attn_ref, ret_ref, dn_ref, ab_ref, mg_ref) = refs
    m = mod_ref[0]
    h = _norm_mod(x_ref[...], nw_ref[...], m[3:4], m[4:5]).astype(BF16)

    def proj(lo, width):
        return jnp.dot(h, w_ref[:, lo:lo + width], preferred_element_type=F32)

    def rot(v):
        return _rope(v, cos_ref[...], sin_ref[...]) if rope else v

    a = proj(0, ATTN_W)
    q = _head_rms(a[:, :512], ones_ref[...], gq_ref[...])
    k = _head_rms(a[:, 512:640], ones_ref[:LANES, :LANES], gk_ref[...])
    attn_ref[:, :512] = rot(q) * (HEAD_DIM ** -0.5)
    attn_ref[:, 512:640] = rot(k)
    attn_ref[:, 640:768] = a[:, 640:768]

    r = proj(ATTN_W, RET_W)
    ret_ref[:, :512] = rot(r[:, :512]) * (RET_DK ** -0.5)
    ret_ref[:, 512:1024] = rot(r[:, 512:1024])
    ret_ref[:, 1024:] = r[:, 1024:]

    dn_ref[...] = proj(ATTN_W + RET_W, DN_W)
    ab_ref[...] = proj(ATTN_W + RET_W + DN_W, AB_W)
    mg_ref[...] = proj(ATTN_W + RET_W + DN_W + AB_W, MG_W)


def _inproj(x, mod, nw, w, ones_bd, gq, gk, rope_tabs, *, rows_per_cond, seq_len, tm=256):
    rows, d = x.shape
    tiles_per_cond = rows_per_cond // tm
    tiles_per_seq = seq_len // tm
    rope = rope_tabs is not None
    in_specs = [pl.BlockSpec((tm, d), lambda i: (i, 0)),
                pl.BlockSpec((1, N_MOD, d), lambda i: (i // tiles_per_cond, 0, 0)),
                _resident((1, d)),
                _resident(w.shape),
                _resident(ones_bd.shape),
                _resident(gq.shape),
                _resident(gk.shape)]
    args = [x, mod, nw, w, ones_bd, gq, gk]
    if rope:
        in_specs += [pl.BlockSpec((tm, LANES), lambda i: (i % tiles_per_seq, 0))] * 2
        args += list(rope_tabs)
    widths = (ATTN_W, RET_W, DN_W, AB_W, MG_W)
    return pl.pallas_call(
        functools.partial(_inproj_kernel, rope=rope),
        grid=(rows // tm,),
        in_specs=in_specs,
        out_specs=[pl.BlockSpec((tm, wd), lambda i: (i, 0)) for wd in widths],
        out_shape=[jax.ShapeDtypeStruct((rows, wd), F32) for wd in widths],
        compiler_params=_cparams(1),
        name="inproj",
    )(*args)


ATTN_GROUP = ATTN_HEADS // ATTN_KV_HEADS
LOG2E = math.log2(math.e)
ATTN_CHUNK_GROUP = 8
ATTN_ROW_BLOCKS = 2


def _attn_kernel(*refs, tq, kv_chunk, n_chunks, has_ctx):
    if has_ctx:
        q_ref, k_ref, v_ref, ck_ref, cv_ref, o_ref, mx_ref, ls_ref, acc_ref, qs_ref = refs
    else:
        q_ref, k_ref, v_ref, o_ref, mx_ref, ls_ref, acc_ref, qs_ref = refs
    lane = lax.broadcasted_iota(jnp.int32, (tq, LANES), 1)
    low = lane < HEAD_DIM
    parts = []
    for hd in range(ATTN_HEADS):
        g = hd // ATTN_GROUP
        blk = q_ref[:, (hd // 2) * LANES:(hd // 2 + 1) * LANES] * LOG2E
        if hd % 2 != g:
            blk = pltpu.roll(blk, HEAD_DIM, 1)
        parts.append(jnp.where(low if g == 0 else jnp.logical_not(low), blk, 0.0))
    qs_ref[...] = jnp.concatenate(parts, axis=0).astype(BF16)

    def lane_fold(op, acc, x):
        for b in range(x.shape[1] // LANES):
            acc = op(acc, x[:, b * LANES:(b + 1) * LANES])
        return acc

    mx_ref[...] = jnp.full(mx_ref.shape, -jnp.inf, F32)
    ls_ref[...] = jnp.zeros(ls_ref.shape, F32)
    acc_ref[...] = jnp.zeros(acc_ref.shape, F32)
    rbs = qs_ref.shape[0] // ATTN_ROW_BLOCKS
    blocks = [slice(r * rbs, (r + 1) * rbs) for r in range(ATTN_ROW_BLOCKS)]

    def scores(sl, k):
        return _dot_nt(qs_ref[sl], k)

    def update(sl, s, v):
        reps = s.shape[1] // LANES
        m_old = mx_ref[sl]
        cmax = lane_fold(jnp.maximum, s[:, :LANES], s[:, LANES:])
        m_new = jnp.maximum(m_old, jnp.max(cmax, axis=-1, keepdims=True))
        alpha = jnp.exp2(m_old - m_new)
        p = jnp.exp2(s - jnp.concatenate([m_new] * reps, axis=1))
        mx_ref[sl] = m_new
        ls_ref[sl] = alpha * ls_ref[sl] + lane_fold(jnp.add, p[:, :LANES], p[:, LANES:])
        acc_ref[sl] = alpha * acc_ref[sl] + _dot(p, v)

    def chunks(kvs):
        units = [(sl, k.astype(BF16), v.astype(BF16)) for k, v in kvs for sl in blocks]
        ahead = None
        for sl, k, v in units + [(None, None, None)]:
            nxt = scores(sl, k) if sl is not None else None
            if ahead is not None:
                update(*ahead)
            ahead = (sl, nxt, v)

    group = min(ATTN_CHUNK_GROUP, n_chunks)

    def body(i, carry):
        kvs = []
        for j in range(group):
            off = pl.multiple_of((i * group + j) * kv_chunk, kv_chunk)
            kvs.append((k_ref[pl.ds(off, kv_chunk), :], v_ref[pl.ds(off, kv_chunk), :]))
        chunks(kvs)
        return carry

    tail = [(ck_ref[0], cv_ref[0])] if has_ctx else []
    if group == n_chunks:
        chunks([(k_ref[c * kv_chunk:(c + 1) * kv_chunk, :], v_ref[c * kv_chunk:(c + 1) * kv_chunk, :])
                for c in range(n_chunks)] + tail)
    else:
        lax.fori_loop(0, n_chunks // group, body, 0)
        if tail:
            chunks(tail)
    res = acc_ref[...] / jnp.sum(ls_ref[...], axis=-1, keepdims=True)

    for b in range(ATTN_HEADS // 2):
        g = (2 * b) // ATTN_GROUP
        even = res[2 * b * tq:(2 * b + 1) * tq]
        odd = res[(2 * b + 1) * tq:(2 * b + 2) * tq]
        if g == 0:
            odd = pltpu.roll(odd, HEAD_DIM, 1)
        else:
            even = pltpu.roll(even, HEAD_DIM, 1)
        o_ref[:, b * LANES:(b + 1) * LANES] = jnp.where(low, even, odd)


def _attention(attn, ctx_k, ctx_v, *, batch, seq_len, tq=128):
    rows = attn.shape[0]
    kv_chunk = min(seq_len, 512)
    n_chunks = seq_len // kv_chunk
    qt = seq_len // tq
    stacked = ATTN_HEADS * tq
    has_ctx = ctx_k is not None
    in_specs = [pl.BlockSpec((tq, 512), lambda b, i: (b * qt + i, 0)),
                pl.BlockSpec((seq_len, LANES), lambda b, i: (b, 4)),
                pl.BlockSpec((seq_len, LANES), lambda b, i: (b, 5))]
    args = [attn, attn, attn]
    if has_ctx:
        past = ctx_k.shape[1]
        in_specs += [pl.BlockSpec((1, past, LANES), lambda b, i: (b, 0, 0))] * 2
        args += [ctx_k, ctx_v]
    scratch = [pltpu.VMEM((stacked, LANES), F32)] * 3 + [pltpu.VMEM((stacked, LANES), BF16)]
    return pl.pallas_call(
        functools.partial(_attn_kernel, tq=tq, kv_chunk=kv_chunk, n_chunks=n_chunks, has_ctx=has_ctx),
        grid=(batch, qt),
        in_specs=in_specs,
        out_specs=pl.BlockSpec((tq, 512), lambda b, i: (b * qt + i, 0)),
        out_shape=jax.ShapeDtypeStruct((rows, 512), F32),
        scratch_shapes=scratch,
        compiler_params=_cparams(2),
        name="attention",
    )(*args)


def _retention_tables():
    C = RET_CHUNK
    h = np.arange(RET_HEADS, dtype=np.float64)
    pos = np.arange(C, dtype=np.float64)
    diff = pos[:, None] - pos[None, :]
    inner, qd, kd, cd = [], [], [], []
    for direction, expo in enumerate((RET_DECAY_EXP_FWD, RET_DECAY_EXP_BWD)):
        lg = np.log1p(-np.exp2(-expo - h))[:, None, None]
        if direction == 0:
            inner.append(np.where(diff >= 0, np.exp(lg * np.maximum(diff, 0.0)), 0.0))
            qd.append(np.exp(lg[:, :, 0] * (pos + 1.0)))
            kd.append(np.exp(lg[:, :, 0] * (C - 1.0 - pos)))
        else:
            inner.append(np.where(diff <= 0, np.exp(lg * np.maximum(-diff, 0.0)), 0.0))
            qd.append(np.exp(lg[:, :, 0] * (C - pos)))
            kd.append(np.exp(lg[:, :, 0] * pos))
        cd.append(np.exp(lg[:, 0, 0] * C))
    inner = np.stack(inner, axis=1)
    rowdec = np.stack([np.stack(qd, 1), np.stack(kd, 1)], axis=2)
    rowdec = np.repeat(rowdec[..., None], RET_DK, axis=-1)
    rowdec = np.concatenate([rowdec[0::2], rowdec[1::2]], axis=-1)
    cd = np.stack(cd, axis=1)
    block = np.kron(np.eye(2), np.ones((RET_DK, RET_DK)))
    per_row = np.repeat(np.stack([cd[0::2], cd[1::2]], axis=-1), RET_DK, axis=-1)
    cd = per_row[:, :, :, None] * block
    return (jnp.asarray(inner, F32), jnp.asarray(rowdec, F32), jnp.asarray(cd, F32))


RET_BLOCK = 8


def _split_bf16(x):
    hi = x.astype(BF16)
    return hi, (x - hi.astype(F32)).astype(BF16)


def _ret_kernel(q_ref, k_ref, v_ref, g_ref, inner_ref, dec_ref, cd_ref, ones_ref, s0_ref, nw_ref,
                o_ref, st_ref, *, n_chunks, n_pairs):
    C = RET_CHUNK
    D = RET_DK
    nb = min(RET_BLOCK, n_chunks)
    lanes_of = [slice(p * LANES, (p + 1) * LANES) for p in range(n_pairs)]
    lane = lax.broadcasted_iota(jnp.int32, (C, LANES), 1)
    first = lane < D
    ones = ones_ref[...]

    def head_mean(x):
        hi, lo = _split_bf16(x)
        return (jnp.dot(hi, ones, preferred_element_type=F32)
                + jnp.dot(lo, ones, preferred_element_type=F32)) * (1.0 / D)

    def sweep(direction):
        def body(i, states):
            order = [i * nb + j for j in range(nb)]
            if direction == 1:
                order = [n_chunks - 1 - c for c in order]
            units = [(p, pl.multiple_of(c * C, C)) for p in range(n_pairs) for c in order]
            n = len(units)
            q = [q_ref[pl.ds(r, C), lanes_of[p]] for p, r in units]
            k = [k_ref[pl.ds(r, C), lanes_of[p]] for p, r in units]
            v = [v_ref[pl.ds(r, C), lanes_of[p]] for p, r in units]
            if direction == 1:
                prev = [o_ref[pl.ds(r, C), lanes_of[p]] for p, r in units]
                gate = [g_ref[pl.ds(r, C), lanes_of[p]] for p, r in units]
            att = [jnp.concatenate(
                [_dot_nt(jnp.where(first, q[e], 0.0), k[e]) * inner_ref[2 * units[e][0], direction],
                 _dot_nt(jnp.where(first, 0.0, q[e]), k[e]) * inner_ref[2 * units[e][0] + 1, direction]],
                axis=1) for e in range(n)]
            kv = [_dot_tn(k[e] * dec_ref[units[e][0], direction, 1], v[e]) for e in range(n)]
            seen = []
            states = list(states)
            for e, (p, _) in enumerate(units):
                cd = cd_ref[p, direction]
                seen.append(states[p])
                states[p] = states[p] * cd + jnp.where(cd != 0.0, kv[e], 0.0)
            v2 = [jnp.concatenate([jnp.where(first, v[e], 0.0), jnp.where(first, 0.0, v[e])], axis=0)
                  for e in range(n)]
            o = [_dot(att[e], v2[e]) + _dot(q[e] * dec_ref[units[e][0], direction, 0], seen[e])
                 for e in range(n)]
            if direction == 1:
                o = [o[e] + prev[e] for e in range(n)]
                mean = [head_mean(o[e]) for e in range(n)]
                d = [o[e] - mean[e] for e in range(n)]
                var = [head_mean(d[e] * d[e]) for e in range(n)]
                o = [d[e] * lax.rsqrt(var[e] + EPS) * nw_ref[:, lanes_of[units[e][0]]] * _silu(gate[e])
                     for e in range(n)]
            for e, (p, r) in enumerate(units):
                o_ref[pl.ds(r, C), lanes_of[p]] = o[e]
            return tuple(states)

        zero = jnp.zeros((D, D), F32)
        states = tuple(
            jnp.concatenate([jnp.concatenate([s0_ref[0, direction, 2 * p], zero], axis=1),
                             jnp.concatenate([zero, s0_ref[0, direction, 2 * p + 1]], axis=1)], axis=0)
            for p in range(n_pairs))
        states = lax.fori_loop(0, n_chunks // nb, body, states)
        for p in range(n_pairs):
            st_ref[0, direction, 2 * p] = states[p][:D, :D]
            st_ref[0, direction, 2 * p + 1] = states[p][D:, D:]

    sweep(0)
    sweep(1)


def _retention(ret, s0, nw, tables, ones_bd, *, batch, seq_len, n_pairs):
    rows = ret.shape[0]
    inner, rowdec, cd = tables
    C = RET_CHUNK
    U = n_pairs
    groups = RET_HEADS // 2 // U
    col = lambda j: pl.BlockSpec((seq_len, U * LANES), lambda b, hp, j=j: (b, groups * j + hp))
    st_spec = pl.BlockSpec((1, 2, 2 * U, RET_DK, RET_DK), lambda b, hp: (b, 0, hp, 0, 0))
    return pl.pallas_call(
        functools.partial(_ret_kernel, n_chunks=seq_len // C, n_pairs=U),
        grid=(batch, groups),
        in_specs=[col(0), col(1), col(2), col(3),
                  pl.BlockSpec((2 * U, 2, C, C), lambda b, hp: (hp, 0, 0, 0)),
                  pl.BlockSpec((U, 2, 2, C, LANES), lambda b, hp: (hp, 0, 0, 0, 0)),
                  pl.BlockSpec((U, 2, LANES, LANES), lambda b, hp: (hp, 0, 0, 0)),
                  pl.BlockSpec((LANES, LANES), lambda b, hp: (0, 0)),
                  st_spec,
                  pl.BlockSpec((1, U * LANES), lambda b, hp: (0, hp))],
        out_specs=[pl.BlockSpec((seq_len, U * LANES), lambda b, hp: (b, hp)), st_spec],
        out_shape=[jax.ShapeDtypeStruct((rows, 512), F32),
                   jax.ShapeDtypeStruct((batch, 2, RET_HEADS, RET_DK, RET_DK), F32)],
        compiler_params=_cparams(2),
        name="retention",
    )(ret, ret, ret, ret, inner, rowdec, cd, ones_bd, s0, nw)


DN_BLOCK = 8
SOLVE_BASE = 8
CONV_BLOCK = 256


def _dn_kernel(q_ref, k_ref, v_ref, g_ref, ab_ref, cw_ref, alog_ref, dtb_ref, nw_ref, s0_ref,
               o_ref, st_ref, qs_ref, ks_ref, vs_ref, of_ref, ob_ref,
               wp_ref, bm_ref, qp_ref, op_ref, dec_ref, *, seq_len, n_heads):
    C = DN_CHUNK
    T = seq_len
    U = n_heads
    n_chunks = T // C
    head0 = pl.program_id(1) * U
    lanes_of = [slice(hh * LANES, (hh + 1) * LANES) for hh in range(U)]

    blk = min(CONV_BLOCK, T)
    row = lax.broadcasted_iota(jnp.int32, (blk, LANES), 0)
    tensors = [(src, dst, t, hh) for hh in range(U)
               for t, (src, dst) in enumerate(((q_ref, qs_ref), (k_ref, ks_ref), (v_ref, vs_ref)))]

    def conv(i, carry):
        r0 = pl.multiple_of(i * blk, blk)
        above = pl.multiple_of(jnp.maximum(r0 - SUBLANES, 0), SUBLANES)
        below = pl.multiple_of(jnp.minimum(r0 + blk, T - SUBLANES), SUBLANES)
        x = [src[pl.ds(r0, blk), lanes_of[hh]] for src, _, _, hh in tensors]
        up = [jnp.where(r0 > 0, src[pl.ds(above, SUBLANES), lanes_of[hh]], 0.0) for src, _, _, hh in tensors]
        dn = [jnp.where(r0 + blk < T, src[pl.ds(below, SUBLANES), lanes_of[hh]], 0.0)
              for src, _, _, hh in tensors]
        n = len(tensors)
        padded = [jnp.concatenate([up[e], x[e], dn[e]], axis=0) for e in range(n)]
        prev = [padded[e][SUBLANES - 1:SUBLANES - 1 + blk] for e in range(n)]
        nxt = [padded[e][SUBLANES + 1:SUBLANES + 1 + blk] for e in range(n)]
        w = [cw_ref[t, hh] for _, _, t, hh in tensors]
        y = [_silu(w[e][0:1] * prev[e] + w[e][1:2] * x[e] + w[e][2:3] * nxt[e]) for e in range(n)]
        for e, (_, dst, t, hh) in enumerate(tensors):
            if t < 2:
                scale = lax.rsqrt(jnp.sum(y[e] * y[e], axis=-1, keepdims=True) + EPS)
                y[e] = y[e] * (scale * (DN_DK ** -0.5) if t == 0 else scale)
        for e, (_, dst, t, hh) in enumerate(tensors):
            dst[pl.ds(r0, blk), lanes_of[hh]] = y[e]
        return carry

    lax.fori_loop(0, T // blk, conv, 0)

    ri = lax.broadcasted_iota(jnp.int32, (C, C), 0)
    ci = lax.broadcasted_iota(jnp.int32, (C, C), 1)
    eye = ri == ci
    eye_f = eye.astype(F32)
    lane = lax.broadcasted_iota(jnp.int32, (1, LANES), 1)
    masks = ((ri >= ci, ri > ci), (ri <= ci, ri < ci))
    incl_bf = tuple(m[0].astype(F32).astype(BF16) for m in masks)
    base_blocks = (ri // SOLVE_BASE) == (ci // SOLVE_BASE)
    level_masks = []
    size = SOLVE_BASE
    while size < C:
        level_masks.append(jnp.logical_and((ri // (2 * size)) == (ci // (2 * size)),
                                           (ri // size) != (ci // size)))
        size *= 2

    def load(r0, hh):
        return (ab_ref[pl.ds(r0, C), :],) + tuple(ref[pl.ds(r0, C), lanes_of[hh]]
                                                   for ref in (qs_ref, ks_ref, vs_ref))

    def prep(operands, direction, hh):
        ab, q, k, v = operands
        incl, strict = masks[direction]
        sel_a = (lane == direction * DN_HEADS + head0 + hh).astype(F32)
        sel_b = (lane == 2 * DN_HEADS + direction * DN_HEADS + head0 + hh).astype(F32)
        da = jnp.sum(ab * sel_a, axis=-1, keepdims=True)
        db = jnp.sum(ab * sel_b, axis=-1, keepdims=True)
        z = da + dtb_ref[direction, hh]
        softplus = jnp.maximum(z, 0.0) + jnp.log1p(jnp.exp(-jnp.abs(z)))
        g = -jnp.exp(alog_ref[direction, hh]) * softplus
        beta = _sigmoid(db)
        kb = k * beta
        g1 = g.astype(BF16)
        g2 = (g - g1.astype(F32)).astype(BF16)
        g3 = (g - g1.astype(F32) - g2.astype(F32)).astype(BF16)
        G12 = _dot(incl_bf[direction], jnp.concatenate([g1, g2], axis=1))
        G = G12[:, :LANES] + G12[:, LANES:] + _dot(incl_bf[direction], g3)
        kk = _dot_nt(kb, k)
        qk = _dot_nt(q, k)
        yield
        Gc = G[:, :C]
        Grow = jnp.sum(jnp.where(eye, Gc, 0.0), axis=0, keepdims=True)
        L = jnp.where(incl, jnp.exp(jnp.where(incl, Gc - Grow, 0.0)), 0.0)
        N = jnp.where(strict, -(kk * L), 0.0)
        P = jnp.where(base_blocks, N, 0.0)
        Tm = eye_f + P
        P = _dot(P, P)
        yield
        for _ in range(int(math.log2(SOLVE_BASE)) - 2):
            Tm, P = Tm + _dot(Tm, P), _dot(P, P)
            yield
        Tm = Tm + _dot(Tm, P)
        yield
        for off_blocks in level_masks:
            TX = _dot(Tm, jnp.where(off_blocks, N, 0.0))
            yield
            Tm = Tm + _dot(TX, Tm)
            yield
        eG = jnp.exp(G)
        g_last = G[C - 1:C] if direction == 0 else G[0:1]
        wu = _dot(Tm, jnp.concatenate([kb * eG, v * beta], axis=1))
        yield
        kd = k * jnp.exp(g_last - G)
        att = qk * L
        kd_wu = _dot_tn(kd, wu)
        att_wu = _dot(att, wu)
        return ((-kd_wu[:, :LANES]).astype(BF16), kd_wu[:, LANES:],
                (q * eG - att_wu[:, :LANES]).astype(BF16), att_wu[:, LANES:],
                jnp.broadcast_to(jnp.exp(g_last), (SUBLANES, LANES)))

    def run_staged(generators):
        results = [None] * len(generators)
        live = list(enumerate(generators))
        while live:
            still = []
            for idx, gen in live:
                try:
                    next(gen)
                    still.append((idx, gen))
                except StopIteration as done:
                    results[idx] = done.value
            live = still
        return results

    slots = (wp_ref, bm_ref, qp_ref, op_ref, dec_ref)
    nb = min(DN_BLOCK, n_chunks)
    n_blocks = n_chunks // nb
    per_half = 2 * nb * U

    def block_rows(i):
        return ([pl.multiple_of((i * nb + j) * C, C) for j in range(nb)]
                + [pl.multiple_of((n_chunks - 1 - (i * nb + j)) * C, C) for j in range(nb)])

    def prep_block(i):
        rows = block_rows(i)
        return [prep(load(r, hh), s // nb, hh) for hh in range(U) for s, r in enumerate(rows)]

    def store_block(prepared, base):
        for e in range(per_half):
            for ref, val in zip(slots, prepared[e]):
                ref[base + e] = val

    def recurrence(states, base):
        states = list(states)
        outs = []
        for j in range(nb):
            step_out = []
            for hh in range(U):
                for direction in range(2):
                    e = base + hh * 2 * nb + direction * nb + j
                    S = states[2 * hh + direction]
                    S16 = S.astype(BF16)
                    step_out.append(_dot(qp_ref[e], S16) + op_ref[e])
                    states[2 * hh + direction] = S * dec_ref[e, 0:1] + _dot(wp_ref[e], S16) + bm_ref[e]
            outs.append(step_out)
            yield
        return outs, tuple(states)

    def store_outputs(i, outs):
        rows = block_rows(i)
        for j in range(nb):
            for hh in range(U):
                of_ref[pl.ds(rows[j], C), lanes_of[hh]] = outs[j][2 * hh]
                ob_ref[pl.ds(rows[nb + j], C), lanes_of[hh]] = outs[j][2 * hh + 1]

    state = tuple(s0_ref[0, direction, hh] for hh in range(U) for direction in range(2))
    if n_blocks == 1:
        store_block(run_staged(prep_block(0)), 0)
        (outs, state), = run_staged([recurrence(state, 0)])
        store_outputs(0, outs)
    else:
        store_block(run_staged(prep_block(0)), 0)

        def body(i, carry):
            cur = (i % 2) * per_half
            nxt = per_half - cur
            *prepared, (outs, carry) = run_staged(prep_block(i + 1) + [recurrence(carry, cur)])
            store_outputs(i, outs)
            store_block(prepared, nxt)
            return carry

        state = lax.fori_loop(0, n_blocks - 1, body, state)
        (outs, state), = run_staged([recurrence(state, ((n_blocks - 1) % 2) * per_half)])
        store_outputs(n_blocks - 1, outs)
    for hh in range(U):
        for direction in range(2):
            st_ref[0, direction, hh] = state[2 * hh + direction]

    def fin(i, carry):
        r0 = pl.multiple_of(i * blk, blk)
        for hh in range(U):
            o = of_ref[pl.ds(r0, blk), lanes_of[hh]] + ob_ref[pl.ds(r0, blk), lanes_of[hh]]
            y = o * lax.rsqrt(jnp.mean(o * o, axis=-1, keepdims=True) + EPS) * nw_ref[...]
            o_ref[pl.ds(r0, blk), lanes_of[hh]] = y * _silu(g_ref[pl.ds(r0, blk), lanes_of[hh]])
        return carry

    lax.fori_loop(0, T // blk, fin, 0)


def _deltanet(dn, ab, conv_w, alog, dtb, nw, s0, *, batch, seq_len, n_heads):
    rows = dn.shape[0]
    T = seq_len
    C = DN_CHUNK
    U = n_heads
    nb = min(DN_BLOCK, T // C)
    n_slots = 4 * nb * U
    groups = DN_HEADS // U
    col = lambda j: pl.BlockSpec((T, U * LANES), lambda b, h, j=j: (b, groups * j + h))
    st_spec = pl.BlockSpec((1, 2, U, DN_DK, DN_DK), lambda b, h: (b, 0, h, 0, 0))
    gate_spec = pl.BlockSpec((2, U, 1, LANES), lambda b, h: (0, h, 0, 0))
    return pl.pallas_call(
        functools.partial(_dn_kernel, seq_len=T, n_heads=U),
        grid=(batch, groups),
        in_specs=[col(0), col(1), col(2), col(3),
                  pl.BlockSpec((T, LANES), lambda b, h: (b, 0)),
                  pl.BlockSpec((3, U, 3, LANES), lambda b, h: (0, h, 0, 0)),
                  gate_spec, gate_spec,
                  pl.BlockSpec((1, LANES), lambda b, h: (0, 0)),
                  st_spec],
        out_specs=[pl.BlockSpec((T, U * LANES), lambda b, h: (b, h)), st_spec],
        out_shape=[jax.ShapeDtypeStruct((rows, 512), F32),
                   jax.ShapeDtypeStruct((batch, 2, DN_HEADS, DN_DK, DN_DK), F32)],
        scratch_shapes=([pltpu.VMEM((T, U * LANES), F32)] * 5
                        + [pltpu.VMEM((n_slots, DN_DK, LANES), BF16), pltpu.VMEM((n_slots, DN_DK, LANES), F32),
                           pltpu.VMEM((n_slots, C, LANES), BF16), pltpu.VMEM((n_slots, C, LANES), F32),
                           pltpu.VMEM((n_slots, SUBLANES, LANES), F32)]),
        compiler_params=_cparams(2),
        name="deltanet",
    )(dn, dn, dn, dn, ab, conv_w, alog, dtb, nw, s0)


def _merge_ffn_kernel(x_ref, mod_ref, a_ref, r_ref, d_ref, mg_ref, wbr_ref, wo_ref,
                      nw_ref, win_ref, wout_ref, fnw_ref, o_ref, *, dff, final):
    d = x_ref.shape[-1]
    m = mod_ref[0]
    merged = jnp.zeros(x_ref.shape, F32)
    for i, br in enumerate((a_ref, r_ref, d_ref)):
        merged = merged + _sigmoid(mg_ref[:, i * d:(i + 1) * d]) * _dot(br[...], wbr_ref[i])
    x = x_ref[...] + m[5:6] * _dot(merged, wo_ref[...])
    o_ref[...] = _ffn_rows(x, m, nw_ref, win_ref, wout_ref, fnw_ref, mod_base=6, dff=dff, final=final)


def _merge_ffn(x, mod, a, r, dn, mg, w_br, w_o, nw, w_in, w_out, fnw, *, rows_per_cond, final, tm=512):
    rows, d = x.shape
    dff = w_out.shape[0]
    tiles_per_cond = rows_per_cond // tm
    row_spec = lambda wd: pl.BlockSpec((tm, wd), lambda i: (i, 0))
    return pl.pallas_call(
        functools.partial(_merge_ffn_kernel, dff=dff, final=final),
        grid=(rows // tm,),
        in_specs=[row_spec(d),
                  pl.BlockSpec((1, N_MOD, d), lambda i: (i // tiles_per_cond, 0, 0)),
                  row_spec(512), row_spec(512), row_spec(512), row_spec(MG_W),
                  _resident(w_br.shape), _resident(w_o.shape),
                  _resident((1, d)), _resident(w_in.shape), _resident(w_out.shape), _resident((1, d))],
        out_specs=row_spec(d),
        out_shape=jax.ShapeDtypeStruct((rows, d), F32),
        compiler_params=_cparams(1),
        name="merge_ffn",
    )(x, mod, a, r, dn, mg, w_br, w_o, nw, w_in, w_out, fnw)


def _rope_tables(seq_len):
    n_freq = HEAD_DIM // 4
    inv = ROPE_THETA ** (-jnp.arange(n_freq, dtype=F32) / n_freq)
    t = jnp.arange(seq_len)
    row = (t // GRID_W).astype(F32)
    colp = (t % GRID_W).astype(F32)
    ang = jnp.concatenate([row[:, None] * inv, colp[:, None] * inv], axis=-1)
    c, s = jnp.cos(ang), jnp.sin(ang)
    cos = jnp.concatenate([c, c, c, c], axis=-1)
    sin = jnp.concatenate([-s, s, -s, s], axis=-1)
    return cos, sin


def _reorder_w_in(w):
    d = w.shape[0]
    o_da = 768 + 2048 + 1536
    o_dg = o_da + 4 * DN_HEADS
    o_mg = o_dg + 512
    parts = [w[:, :o_da], w[:, o_dg:o_mg], w[:, o_da:o_dg],
             jnp.zeros((d, AB_W - 4 * DN_HEADS), w.dtype), w[:, o_mg:]]
    return jnp.concatenate([p.astype(BF16) for p in parts], axis=1)


def kernel(x_prompt, x_sample, cache_k, cache_v, state_ret, state_delta, c, c_ctx,
           w_mod, b_mod, norm_ffn1, ffn1_w_in, ffn1_w_out, norm_mix, w_in,
           attn_q_norm, attn_k_norm, ret_norm, dn_conv, dn_a_log, dn_dt_bias, dn_norm,
           w_br_attn, w_br_ret, w_br_dn, w_out, norm_ffn2, ffn2_w_in, ffn2_w_out, norm_final):
    bp, tp, d = x_prompt.shape
    bs, ts, _ = x_sample.shape
    depth = w_mod.shape[0]
    past = cache_k.shape[2]

    conds = jnp.concatenate([c_ctx[None, :], c], axis=0)
    mod = _modulation(conds, w_mod, b_mod).reshape(depth, 1 + bs, N_MOD, d)

    ones_bd = jnp.asarray(np.kron(np.eye(ATTN_HEADS), np.ones((HEAD_DIM, HEAD_DIM))), BF16)
    ret_tabs = _retention_tables()
    rope_tabs = _rope_tables(ts)
    ret_zero = jnp.zeros((bp, 2, RET_HEADS, RET_DK, RET_DK), F32)
    dn_zero = jnp.zeros((bp, 2, DN_HEADS, DN_DK, DN_DK), F32)
    fnw = norm_final.reshape(1, d)

    groups = {
        "prompt": dict(x=x_prompt.reshape(bp * tp, d), batch=bp, seq=tp, rows_per_cond=bp * tp, rope=None,
                       dn_heads=DN_HEADS, ret_pairs=RET_HEADS // 2),
        "sample": dict(x=x_sample.reshape(bs * ts, d), batch=bs, seq=ts, rows_per_cond=ts, rope=rope_tabs,
                       dn_heads=1, ret_pairs=1),
    }
    new_k, new_v, new_rs, new_ds = [], [], [], []
    for l in range(depth):
        w1_in, w1_out = ffn1_w_in[l].astype(BF16), ffn1_w_out[l].astype(BF16)
        w2_in, w2_out = ffn2_w_in[l].astype(BF16), ffn2_w_out[l].astype(BF16)
        w_proj = _reorder_w_in(w_in[l])
        w_br = jnp.stack([w_br_attn[l], w_br_ret[l], w_br_dn[l]]).astype(BF16)
        w_o = w_out[l].astype(BF16)
        gq = jnp.tile(attn_q_norm[l], ATTN_HEADS).reshape(1, 512)
        gk = jnp.tile(attn_k_norm[l], ATTN_KV_HEADS).reshape(1, LANES)
        conv_w = dn_conv[l].reshape(3, 3, DN_HEADS, LANES).transpose(1, 2, 0, 3)
        alog = jnp.broadcast_to(dn_a_log[l][:, :, None, None], (2, DN_HEADS, 1, LANES))
        dtb = jnp.broadcast_to(dn_dt_bias[l][:, :, None, None], (2, DN_HEADS, 1, LANES))
        for name, grp in groups.items():
            is_prompt = name == "prompt"
            x = grp["x"]
            gmod = mod[l, :1] if is_prompt else mod[l, 1:]
            rpc = grp["rows_per_cond"]
            x = _ffn(x, gmod, norm_ffn1[l].reshape(1, d), w1_in, w1_out, fnw,
                     mod_base=0, rows_per_cond=rpc, final=False)
            attn, ret, dn, ab, mg = _inproj(x, gmod, norm_mix[l].reshape(1, d), w_proj, ones_bd, gq, gk,
                                            grp["rope"], rows_per_cond=rpc, seq_len=grp["seq"])
            if is_prompt:
                a_out = _attention(attn, None, None, batch=bp, seq_len=tp, tq=tp)
                rs0, ds0 = ret_zero, dn_zero
            else:
                a_out = _attention(attn, cache_k[:, l].reshape(bs, past, LANES),
                                   cache_v[:, l].reshape(bs, past, LANES), batch=bs, seq_len=ts)
                rs0, ds0 = state_ret[:, l], state_delta[:, l]
            r_out, rs = _retention(ret, rs0, ret_norm[l].reshape(1, 512), ret_tabs, ones_bd,
                                   batch=grp["batch"], seq_len=grp["seq"], n_pairs=grp["ret_pairs"])
            d_out, ds = _deltanet(dn, ab, conv_w, alog, dtb, dn_norm[l].reshape(1, LANES), ds0,
                                  batch=grp["batch"], seq_len=grp["seq"], n_heads=grp["dn_heads"])
            x = _merge_ffn(x, gmod, a_out, r_out, d_out, mg, w_br, w_o, norm_ffn2[l].reshape(1, d),
                           w2_in, w2_out, fnw, rows_per_cond=rpc, final=(l == depth - 1))
            grp["x"] = x
            if is_prompt:
                new_k.append(attn[:, 512:640].reshape(bp, tp, ATTN_KV_HEADS, HEAD_DIM))
                new_v.append(attn[:, 640:768].reshape(bp, tp, ATTN_KV_HEADS, HEAD_DIM))
                new_rs.append(rs)
                new_ds.append(ds)

    y_prompt = groups["prompt"]["x"].reshape(bp, tp, d)
    y_sample = groups["sample"]["x"].reshape(bs, ts, d)
    return (y_prompt, y_sample, jnp.stack(new_k, axis=1), jnp.stack(new_v, axis=1),
            jnp.stack(new_rs, axis=1), jnp.stack(new_ds, axis=1))
```

```python
import functools
import math

import numpy as np
import jax
import jax.numpy as jnp
from jax import lax
from jax.experimental import pallas as pl
from jax.experimental.pallas import tpu as pltpu

F32 = jnp.float32
BF16 = jnp.bfloat16

EPS = 1e-6
ROPE_THETA = 10000.0
GRID_W = 64
N_MOD = 9

ATTN_HEADS = 8
ATTN_KV_HEADS = 2
HEAD_DIM = 64
RET_HEADS = 8
RET_DK = 64
RET_CHUNK = 128
RET_DECAY_EXP_FWD = 5.0
RET_DECAY_EXP_BWD = 5.5
DN_HEADS = 4
DN_DK = 128
DN_CHUNK = 64
N_BRANCH = 3

LANES = 128
SUBLANES = 8
VMEM_LIMIT = 56 * 1024 * 1024

ATTN_W = 768
RET_W = 2048
DN_W = 2048
AB_W = 128
MG_W = 3072
IN_W = ATTN_W + RET_W + DN_W + AB_W + MG_W


def _cparams(n_axes):
    return pltpu.CompilerParams(dimension_semantics=("parallel",) * n_axes,
                                vmem_limit_bytes=VMEM_LIMIT)


def _resident(shape):
    zeros = (0,) * len(shape)
    return pl.BlockSpec(shape, lambda *_: zeros, pipeline_mode=pl.Buffered(1))


def _dot(a, b):
    return jnp.dot(a.astype(BF16), b.astype(BF16), preferred_element_type=F32)


def _dot_nt(a, b):
    return lax.dot_general(a.astype(BF16), b.astype(BF16), (((1,), (1,)), ((), ())),
                           preferred_element_type=F32)


def _dot_tn(a, b):
    return lax.dot_general(a.astype(BF16), b.astype(BF16), (((0,), (0,)), ((), ())),
                           preferred_element_type=F32)


def _sigmoid(x):
    return 1.0 / (1.0 + jnp.exp(-x))


def _silu(x):
    return x * _sigmoid(x)


def _norm_mod(x, nw, shift, scale):
    y = x * lax.rsqrt(jnp.mean(x * x, axis=-1, keepdims=True) + EPS) * nw
    return y * (1.0 + scale) + shift


def _mod_kernel(c_ref, w_ref, b_ref, o_ref):
    o_ref[0] = _dot(_silu(c_ref[...]), w_ref[0]) + b_ref[0]


def _modulation(conds, w_mod, b_mod):
    depth, d, n = w_mod.shape
    nc = conds.shape[0]
    tn = n // N_MOD
    return pl.pallas_call(
        _mod_kernel,
        grid=(depth, n // tn),
        in_specs=[pl.BlockSpec((nc, d), lambda l, j: (0, 0)),
                  pl.BlockSpec((1, d, tn), lambda l, j: (l, 0, j)),
                  pl.BlockSpec((1, 1, tn), lambda l, j: (l, 0, j))],
        out_specs=pl.BlockSpec((1, nc, tn), lambda l, j: (l, 0, j)),
        out_shape=jax.ShapeDtypeStruct((depth, nc, n), F32),
        compiler_params=_cparams(2),
        name="modulation",
    )(conds, w_mod, b_mod.reshape(depth, 1, n))


FFN_CHUNK = 256


def _ffn_rows(x, m, nw_ref, win_ref, wout_ref, fnw_ref, *, mod_base, dff, final):
    shift, scale, gate = (m[mod_base + i:mod_base + i + 1] for i in range(3))
    h = _norm_mod(x, nw_ref[...], shift, scale).astype(BF16)
    acc = jnp.zeros(x.shape, F32)
    for c in range(dff // FFN_CHUNK):
        lo = c * FFN_CHUNK
        hg = jnp.dot(h, win_ref[:, lo:lo + FFN_CHUNK], preferred_element_type=F32)
        hu = jnp.dot(h, win_ref[:, dff + lo:dff + lo + FFN_CHUNK], preferred_element_type=F32)
        a = (_silu(hg) * hu).astype(BF16)
        acc = acc + jnp.dot(a, wout_ref[lo:lo + FFN_CHUNK, :], preferred_element_type=F32)
    y = x + 0.5 * gate * acc
    if final:
        y = y * lax.rsqrt(jnp.mean(y * y, axis=-1, keepdims=True) + EPS) * fnw_ref[...]
    return y


def _ffn_kernel(x_ref, mod_ref, nw_ref, win_ref, wout_ref, fnw_ref, o_ref, *, mod_base, dff, final):
    o_ref[...] = _ffn_rows(x_ref[...], mod_ref[0], nw_ref, win_ref, wout_ref, fnw_ref,
                           mod_base=mod_base, dff=dff, final=final)


def _ffn(x, mod, nw, w_in, w_out, fnw, *, mod_base, rows_per_cond, final, tm=512):
    rows, d = x.shape
    dff = w_out.shape[0]
    tiles_per_cond = rows_per_cond // tm
    return pl.pallas_call(
        functools.partial(_ffn_kernel, mod_base=mod_base, dff=dff, final=final),
        grid=(rows // tm,),
        in_specs=[pl.BlockSpec((tm, d), lambda i: (i, 0)),
                  pl.BlockSpec((1, N_MOD, d), lambda i: (i // tiles_per_cond, 0, 0)),
                  _resident((1, d)),
                  _resident(w_in.shape),
                  _resident(w_out.shape),
                  _resident((1, d))],
        out_specs=pl.BlockSpec((tm, d), lambda i: (i, 0)),
        out_shape=jax.ShapeDtypeStruct((rows, d), F32),
        compiler_params=_cparams(1),
        name="ffn",
    )(x, mod, nw, w_in, w_out, fnw)


def _swap_halves(x):
    n = x.shape[-1]
    lane = lax.broadcasted_iota(jnp.int32, x.shape, 1)
    first = (lane % HEAD_DIM) < (HEAD_DIM // 2)
    return jnp.where(first, pltpu.roll(x, n - HEAD_DIM // 2, 1), pltpu.roll(x, HEAD_DIM // 2, 1))


def _rope(x, cos, sin):
    reps = x.shape[-1] // LANES
    c = jnp.concatenate([cos] * reps, axis=1) if reps > 1 else cos
    s = jnp.concatenate([sin] * reps, axis=1) if reps > 1 else sin
    return x * c + _swap_halves(x) * s


def _head_rms(x, ones_bd, gain):
    sq = x * x
    hi = sq.astype(BF16)
    lo = (sq - hi.astype(F32)).astype(BF16)
    ss = (jnp.dot(hi, ones_bd, preferred_element_type=F32)
          + jnp.dot(lo, ones_bd, preferred_element_type=F32))
    return x * lax.rsqrt(ss * (1.0 / HEAD_DIM) + EPS) * gain


def _inproj_kernel(*refs, rope):
    if rope:
        (x_ref, mod_ref, nw_ref, w_ref, ones_ref, gq_ref, gk_ref, cos_ref, sin_ref,
         attn_ref, ret_ref, dn_ref, ab_ref, mg_ref) = refs
    else:
        (x_ref, mod_ref, nw_ref, w_ref, ones_ref, gq_ref, gk_ref,
         attn_ref, ret_ref, dn_ref, ab_ref, mg_ref) = refs
    m = mod_ref[0]
    h = _norm_mod(x_ref[...], nw_ref[...], m[3:4], m[4:5]).astype(BF16)

    def proj(lo, width):
        return jnp.dot(h, w_ref[:, lo:lo + width], preferred_element_type=F32)

    def rot(v):
        return _rope(v, cos_ref[...], sin_ref[...]) if rope else v

    a = proj(0, ATTN_W)
    q = _head_rms(a[:, :512], ones_ref[...], gq_ref[...])
    k = _head_rms(a[:, 512:640], ones_ref[:LANES, :LANES], gk_ref[...])
    attn_ref[:, :512] = rot(q) * (HEAD_DIM ** -0.5)
    attn_ref[:, 512:640] = rot(k)
    attn_ref[:, 640:768] = a[:, 640:768]

    r = proj(ATTN_W, RET_W)
    ret_ref[:, :512] = rot(r[:, :512]) * (RET_DK ** -0.5)
    ret_ref[:, 512:1024] = rot(r[:, 512:1024])
    ret_ref[:, 1024:] = r[:, 1024:]

    dn_ref[...] = proj(ATTN_W + RET_W, DN_W)
    ab_ref[...] = proj(ATTN_W + RET_W + DN_W, AB_W)
    mg_ref[...] = proj(ATTN_W + RET_W + DN_W + AB_W, MG_W)


def _inproj(x, mod, nw, w, ones_bd, gq, gk, rope_tabs, *, rows_per_cond, seq_len, tm=256):
    rows, d = x.shape
    tiles_per_cond = rows_per_cond // tm
    tiles_per_seq = seq_len // tm
    rope = rope_tabs is not None
    in_specs = [pl.BlockSpec((tm, d), lambda i: (i, 0)),
                pl.BlockSpec((1, N_MOD, d), lambda i: (i // tiles_per_cond, 0, 0)),
                _resident((1, d)),
                _resident(w.shape),
                _resident(ones_bd.shape),
                _resident(gq.shape),
                _resident(gk.shape)]
    args = [x, mod, nw, w, ones_bd, gq, gk]
    if rope:
        in_specs += [pl.BlockSpec((tm, LANES), lambda i: (i % tiles_per_seq, 0))] * 2
        args += list(rope_tabs)
    widths = (ATTN_W, RET_W, DN_W, AB_W, MG_W)
    return pl.pallas_call(
        functools.partial(_inproj_kernel, rope=rope),
        grid=(rows // tm,),
        in_specs=in_specs,
        out_specs=[pl.BlockSpec((tm, wd), lambda i: (i, 0)) for wd in widths],
        out_shape=[jax.ShapeDtypeStruct((rows, wd), F32) for wd in widths],
        compiler_params=_cparams(1),
        name="inproj",
    )(*args)


ATTN_GROUP = ATTN_HEADS // ATTN_KV_HEADS
LOG2E = math.log2(math.e)
ATTN_CHUNK_GROUP = 8
ATTN_ROW_BLOCKS = 2


def _attn_kernel(*refs, tq, kv_chunk, n_chunks, has_ctx):
    if has_ctx:
        q_ref, k_ref, v_ref, ck_ref, cv_ref, o_ref, mx_ref, ls_ref, acc_ref, qs_ref = refs
    else:
        q_ref, k_ref, v_ref, o_ref, mx_ref, ls_ref, acc_ref, qs_ref = refs
    lane = lax.broadcasted_iota(jnp.int32, (tq, LANES), 1)
    low = lane < HEAD_DIM
    parts = []
    for hd in range(ATTN_HEADS):
        g = hd // ATTN_GROUP
        blk = q_ref[:, (hd // 2) * LANES:(hd // 2 + 1) * LANES] * LOG2E
        if hd % 2 != g:
            blk = pltpu.roll(blk, HEAD_DIM, 1)
        parts.append(jnp.where(low if g == 0 else jnp.logical_not(low), blk, 0.0))
    qs_ref[...] = jnp.concatenate(parts, axis=0).astype(BF16)

    def lane_fold(op, acc, x):
        for b in range(x.shape[1] // LANES):
            acc = op(acc, x[:, b * LANES:(b + 1) * LANES])
        return acc

    mx_ref[...] = jnp.full(mx_ref.shape, -jnp.inf, F32)
    ls_ref[...] = jnp.zeros(ls_ref.shape, F32)
    acc_ref[...] = jnp.zeros(acc_ref.shape, F32)
    rbs = qs_ref.shape[0] // ATTN_ROW_BLOCKS
    blocks = [slice(r * rbs, (r + 1) * rbs) for r in range(ATTN_ROW_BLOCKS)]

    def scores(sl, k):
        return _dot_nt(qs_ref[sl], k)

    def update(sl, s, v):
        reps = s.shape[1] // LANES
        m_old = mx_ref[sl]
        cmax = lane_fold(jnp.maximum, s[:, :LANES], s[:, LANES:])
        m_new = jnp.maximum(m_old, jnp.max(cmax, axis=-1, keepdims=True))
        alpha = jnp.exp2(m_old - m_new)
        p = jnp.exp2(s - jnp.concatenate([m_new] * reps, axis=1))
        mx_ref[sl] = m_new
        ls_ref[sl] = alpha * ls_ref[sl] + lane_fold(jnp.add, p[:, :LANES], p[:, LANES:])
        acc_ref[sl] = alpha * acc_ref[sl] + _dot(p, v)

    def chunks(kvs):
        units = [(sl, k.astype(BF16), v.astype(BF16)) for k, v in kvs for sl in blocks]
        ahead = None
        for sl, k, v in units + [(None, None, None)]:
            nxt = scores(sl, k) if sl is not None else None
            if ahead is not None:
                update(*ahead)
            ahead = (sl, nxt, v)

    group = min(ATTN_CHUNK_GROUP, n_chunks)

    def body(i, carry):
        kvs = []
        for j in range(group):
            off = pl.multiple_of((i * group + j) * kv_chunk, kv_chunk)
            kvs.append((k_ref[pl.ds(off, kv_chunk), :], v_ref[pl.ds(off, kv_chunk), :]))
        chunks(kvs)
        return carry

    tail = [(ck_ref[0], cv_ref[0])] if has_ctx else []
    if group == n_chunks:
        chunks([(k_ref[c * kv_chunk:(c + 1) * kv_chunk, :], v_ref[c * kv_chunk:(c + 1) * kv_chunk, :])
                for c in range(n_chunks)] + tail)
    else:
        lax.fori_loop(0, n_chunks // group, body, 0)
        if tail:
            chunks(tail)
    res = acc_ref[...] / jnp.sum(ls_ref[...], axis=-1, keepdims=True)

    for b in range(ATTN_HEADS // 2):
        g = (2 * b) // ATTN_GROUP
        even = res[2 * b * tq:(2 * b + 1) * tq]
        odd = res[(2 * b + 1) * tq:(2 * b + 2) * tq]
        if g == 0:
            odd = pltpu.roll(odd, HEAD_DIM, 1)
        else:
            even = pltpu.roll(even, HEAD_DIM, 1)
        o_ref[:, b * LANES:(b + 1) * LANES] = jnp.where(low, even, odd)


def _attention(attn, ctx_k, ctx_v, *, batch, seq_len, tq=128):
    rows = attn.shape[0]
    kv_chunk = min(seq_len, 512)
    n_chunks = seq_len // kv_chunk
    qt = seq_len // tq
    stacked = ATTN_HEADS * tq
    has_ctx = ctx_k is not None
    in_specs = [pl.BlockSpec((tq, 512), lambda b, i: (b * qt + i, 0)),
                pl.BlockSpec((seq_len, LANES), lambda b, i: (b, 4)),
                pl.BlockSpec((seq_len, LANES), lambda b, i: (b, 5))]
    args = [attn, attn, attn]
    if has_ctx:
        past = ctx_k.shape[1]
        in_specs += [pl.BlockSpec((1, past, LANES), lambda b, i: (b, 0, 0))] * 2
        args += [ctx_k, ctx_v]
    scratch = [pltpu.VMEM((stacked, LANES), F32)] * 3 + [pltpu.VMEM((stacked, LANES), BF16)]
    return pl.pallas_call(
        functools.partial(_attn_kernel, tq=tq, kv_chunk=kv_chunk, n_chunks=n_chunks, has_ctx=has_ctx),
        grid=(batch, qt),
        in_specs=in_specs,
        out_specs=pl.BlockSpec((tq, 512), lambda b, i: (b * qt + i, 0)),
        out_shape=jax.ShapeDtypeStruct((rows, 512), F32),
        scratch_shapes=scratch,
        compiler_params=_cparams(2),
        name="attention",
    )(*args)


def _retention_tables():
    C = RET_CHUNK
    h = np.arange(RET_HEADS, dtype=np.float64)
    pos = np.arange(C, dtype=np.float64)
    diff = pos[:, None] - pos[None, :]
    inner, qd, kd, cd = [], [], [], []
    for direction, expo in enumerate((RET_DECAY_EXP_FWD, RET_DECAY_EXP_BWD)):
        lg = np.log1p(-np.exp2(-expo - h))[:, None, None]
        if direction == 0:
            inner.append(np.where(diff >= 0, np.exp(lg * np.maximum(diff, 0.0)), 0.0))
            qd.append(np.exp(lg[:, :, 0] * (pos + 1.0)))
            kd.append(np.exp(lg[:, :, 0] * (C - 1.0 - pos)))
        else:
            inner.append(np.where(diff <= 0, np.exp(lg * np.maximum(-diff, 0.0)), 0.0))
            qd.append(np.exp(lg[:, :, 0] * (C - pos)))
            kd.append(np.exp(lg[:, :, 0] * pos))
        cd.append(np.exp(lg[:, 0, 0] * C))
    inner = np.stack(inner, axis=1)
    rowdec = np.stack([np.stack(qd, 1), np.stack(kd, 1)], axis=2)
    rowdec = np.repeat(rowdec[..., None], RET_DK, axis=-1)
    rowdec = np.concatenate([rowdec[0::2], rowdec[1::2]], axis=-1)
    cd = np.stack(cd, axis=1)
    block = np.kron(np.eye(2), np.ones((RET_DK, RET_DK)))
    per_row = np.repeat(np.stack([cd[0::2], cd[1::2]], axis=-1), RET_DK, axis=-1)
    cd = per_row[:, :, :, None] * block
    return (jnp.asarray(inner, F32), jnp.asarray(rowdec, F32), jnp.asarray(cd, F32))


RET_BLOCK = 16


def _split_bf16(x):
    hi = x.astype(BF16)
    return hi, (x - hi.astype(F32)).astype(BF16)


def _ret_kernel(q_ref, k_ref, v_ref, g_ref, inner_ref, dec_ref, cd_ref, ones_ref, s0_ref, nw_ref,
                o_ref, st_ref, *, n_chunks, n_pairs):
    C = RET_CHUNK
    D = RET_DK
    nb = min(RET_BLOCK, n_chunks)
    lanes_of = [slice(p * LANES, (p + 1) * LANES) for p in range(n_pairs)]
    lane = lax.broadcasted_iota(jnp.int32, (C, LANES), 1)
    first = lane < D
    ones = ones_ref[...]

    def head_mean(x):
        hi, lo = _split_bf16(x)
        return (jnp.dot(hi, ones, preferred_element_type=F32)
                + jnp.dot(lo, ones, preferred_element_type=F32)) * (1.0 / D)

    def sweep(direction):
        def body(i, states):
            order = [i * nb + j for j in range(nb)]
            if direction == 1:
                order = [n_chunks - 1 - c for c in order]
            units = [(p, pl.multiple_of(c * C, C)) for p in range(n_pairs) for c in order]
            n = len(units)
            q = [q_ref[pl.ds(r, C), lanes_of[p]] for p, r in units]
            k = [k_ref[pl.ds(r, C), lanes_of[p]] for p, r in units]
            v = [v_ref[pl.ds(r, C), lanes_of[p]] for p, r in units]
            if direction == 1:
                prev = [o_ref[pl.ds(r, C), lanes_of[p]] for p, r in units]
                gate = [g_ref[pl.ds(r, C), lanes_of[p]] for p, r in units]
            att = [jnp.concatenate(
                [_dot_nt(jnp.where(first, q[e], 0.0), k[e]) * inner_ref[2 * units[e][0], direction],
                 _dot_nt(jnp.where(first, 0.0, q[e]), k[e]) * inner_ref[2 * units[e][0] + 1, direction]],
                axis=1) for e in range(n)]
            kv = [_dot_tn(k[e] * dec_ref[units[e][0], direction, 1], v[e]) for e in range(n)]
            seen = []
            states = list(states)
            for e, (p, _) in enumerate(units):
                cd = cd_ref[p, direction]
                seen.append(states[p])
                states[p] = states[p] * cd + jnp.where(cd != 0.0, kv[e], 0.0)
            v2 = [jnp.concatenate([jnp.where(first, v[e], 0.0), jnp.where(first, 0.0, v[e])], axis=0)
                  for e in range(n)]
            o = [_dot(att[e], v2[e]) + _dot(q[e] * dec_ref[units[e][0], direction, 0], seen[e])
                 for e in range(n)]
            if direction == 1:
                o = [o[e] + prev[e] for e in range(n)]
                mean = [head_mean(o[e]) for e in range(n)]
                d = [o[e] - mean[e] for e in range(n)]
                var = [head_mean(d[e] * d[e]) for e in range(n)]
                o = [d[e] * lax.rsqrt(var[e] + EPS) * nw_ref[:, lanes_of[units[e][0]]] * _silu(gate[e])
                     for e in range(n)]
            for e, (p, r) in enumerate(units):
                o_ref[pl.ds(r, C), lanes_of[p]] = o[e]
            return tuple(states)

        zero = jnp.zeros((D, D), F32)
        states = tuple(
            jnp.concatenate([jnp.concatenate([s0_ref[0, direction, 2 * p], zero], axis=1),
                             jnp.concatenate([zero, s0_ref[0, direction, 2 * p + 1]], axis=1)], axis=0)
            for p in range(n_pairs))
        states = lax.fori_loop(0, n_chunks // nb, body, states)
        for p in range(n_pairs):
            st_ref[0, direction, 2 * p] = states[p][:D, :D]
            st_ref[0, direction, 2 * p + 1] = states[p][D:, D:]

    sweep(0)
    sweep(1)


def _retention(ret, s0, nw, tables, ones_bd, *, batch, seq_len, n_pairs):
    rows = ret.shape[0]
    inner, rowdec, cd = tables
    C = RET_CHUNK
    U = n_pairs
    groups = RET_HEADS // 2 // U
    col = lambda j: pl.BlockSpec((seq_len, U * LANES), lambda b, hp, j=j: (b, groups * j + hp))
    st_spec = pl.BlockSpec((1, 2, 2 * U, RET_DK, RET_DK), lambda b, hp: (b, 0, hp, 0, 0))
    return pl.pallas_call(
        functools.partial(_ret_kernel, n_chunks=seq_len // C, n_pairs=U),
        grid=(batch, groups),
        in_specs=[col(0), col(1), col(2), col(3),
                  pl.BlockSpec((2 * U, 2, C, C), lambda b, hp: (hp, 0, 0, 0)),
                  pl.BlockSpec((U, 2, 2, C, LANES), lambda b, hp: (hp, 0, 0, 0, 0)),
                  pl.BlockSpec((U, 2, LANES, LANES), lambda b, hp: (hp, 0, 0, 0)),
                  pl.BlockSpec((LANES, LANES), lambda b, hp: (0, 0)),
                  st_spec,
                  pl.BlockSpec((1, U * LANES), lambda b, hp: (0, hp))],
        out_specs=[pl.BlockSpec((seq_len, U * LANES), lambda b, hp: (b, hp)), st_spec],
        out_shape=[jax.ShapeDtypeStruct((rows, 512), F32),
                   jax.ShapeDtypeStruct((batch, 2, RET_HEADS, RET_DK, RET_DK), F32)],
        compiler_params=_cparams(2),
        name="retention",
    )(ret, ret, ret, ret, inner, rowdec, cd, ones_bd, s0, nw)


DN_BLOCK = 8
SOLVE_BASE = 8
CONV_BLOCK = 256
FIN_BLOCK = 1024


def _dn_kernel(q_ref, k_ref, v_ref, g_ref, ab_ref, cw_ref, alog_ref, dtb_ref, nw_ref, s0_ref,
               o_ref, st_ref, qs_ref, ks_ref, vs_ref, of_ref, ob_ref,
               wp_ref, bm_ref, qp_ref, op_ref, dec_ref, *, seq_len, n_heads):
    C = DN_CHUNK
    T = seq_len
    U = n_heads
    n_chunks = T // C
    head0 = pl.program_id(1) * U
    lanes_of = [slice(hh * LANES, (hh + 1) * LANES) for hh in range(U)]

    blk = min(CONV_BLOCK, T)
    row = lax.broadcasted_iota(jnp.int32, (blk, LANES), 0)
    tensors = [(src, dst, t, hh) for hh in range(U)
               for t, (src, dst) in enumerate(((q_ref, qs_ref), (k_ref, ks_ref), (v_ref, vs_ref)))]

    def conv(i, carry):
        r0 = pl.multiple_of(i * blk, blk)
        above = pl.multiple_of(jnp.maximum(r0 - SUBLANES, 0), SUBLANES)
        below = pl.multiple_of(jnp.minimum(r0 + blk, T - SUBLANES), SUBLANES)
        x = [src[pl.ds(r0, blk), lanes_of[hh]] for src, _, _, hh in tensors]
        up = [jnp.where(r0 > 0, src[pl.ds(above, SUBLANES), lanes_of[hh]], 0.0) for src, _, _, hh in tensors]
        dn = [jnp.where(r0 + blk < T, src[pl.ds(below, SUBLANES), lanes_of[hh]], 0.0)
              for src, _, _, hh in tensors]
        n = len(tensors)
        padded = [jnp.concatenate([up[e], x[e], dn[e]], axis=0) for e in range(n)]
        prev = [padded[e][SUBLANES - 1:SUBLANES - 1 + blk] for e in range(n)]
        nxt = [padded[e][SUBLANES + 1:SUBLANES + 1 + blk] for e in range(n)]
        w = [cw_ref[t, hh] for _, _, t, hh in tensors]
        y = [_silu(w[e][0:1] * prev[e] + w[e][1:2] * x[e] + w[e][2:3] * nxt[e]) for e in range(n)]
        for e, (_, dst, t, hh) in enumerate(tensors):
            if t < 2:
                scale = lax.rsqrt(jnp.sum(y[e] * y[e], axis=-1, keepdims=True) + EPS)
                y[e] = y[e] * (scale * (DN_DK ** -0.5) if t == 0 else scale)
        for e, (_, dst, t, hh) in enumerate(tensors):
            dst[pl.ds(r0, blk), lanes_of[hh]] = y[e]
        return carry

    lax.fori_loop(0, T // blk, conv, 0)

    ri = lax.broadcasted_iota(jnp.int32, (C, C), 0)
    ci = lax.broadcasted_iota(jnp.int32, (C, C), 1)
    eye = ri == ci
    eye_f = eye.astype(F32)
    lane = lax.broadcasted_iota(jnp.int32, (1, LANES), 1)
    masks = ((ri >= ci, ri > ci), (ri <= ci, ri < ci))
    incl_bf = tuple(m[0].astype(F32).astype(BF16) for m in masks)
    base_blocks = (ri // SOLVE_BASE) == (ci // SOLVE_BASE)
    level_masks = []
    size = SOLVE_BASE
    while size < C:
        level_masks.append(jnp.logical_and((ri // (2 * size)) == (ci // (2 * size)),
                                           (ri // size) != (ci // size)))
        size *= 2

    def load(r0, hh):
        return (ab_ref[pl.ds(r0, C), :],) + tuple(ref[pl.ds(r0, C), lanes_of[hh]]
                                                   for ref in (qs_ref, ks_ref, vs_ref))

    def prep(operands, direction, hh):
        ab, q, k, v = operands
        incl, strict = masks[direction]
        sel_a = (lane == direction * DN_HEADS + head0 + hh).astype(F32)
        sel_b = (lane == 2 * DN_HEADS + direction * DN_HEADS + head0 + hh).astype(F32)
        da = jnp.sum(ab * sel_a, axis=-1, keepdims=True)
        db = jnp.sum(ab * sel_b, axis=-1, keepdims=True)
        z = da + dtb_ref[direction, hh]
        softplus = jnp.maximum(z, 0.0) + jnp.log1p(jnp.exp(-jnp.abs(z)))
        g = -jnp.exp(alog_ref[direction, hh]) * softplus
        beta = _sigmoid(db)
        kb = k * beta
        g1 = g.astype(BF16)
        g2 = (g - g1.astype(F32)).astype(BF16)
        g3 = (g - g1.astype(F32) - g2.astype(F32)).astype(BF16)
        G12 = _dot(incl_bf[direction], jnp.concatenate([g1, g2], axis=1))
        G = G12[:, :LANES] + G12[:, LANES:] + _dot(incl_bf[direction], g3)
        kk = _dot_nt(kb, k)
        qk = _dot_nt(q, k)
        yield
        Gc = G[:, :C]
        Grow = jnp.sum(jnp.where(eye, Gc, 0.0), axis=0, keepdims=True)
        L = jnp.where(incl, jnp.exp(jnp.where(incl, Gc - Grow, 0.0)), 0.0)
        N = jnp.where(strict, -(kk * L), 0.0)
        P = jnp.where(base_blocks, N, 0.0)
        Tm = eye_f + P
        P = _dot(P, P)
        yield
        for _ in range(int(math.log2(SOLVE_BASE)) - 2):
            Tm, P = Tm + _dot(Tm, P), _dot(P, P)
            yield
        Tm = Tm + _dot(Tm, P)
        yield
        for off_blocks in level_masks:
            TX = _dot(Tm, jnp.where(off_blocks, N, 0.0))
            yield
            Tm = Tm + _dot(TX, Tm)
            yield
        eG = jnp.exp(G)
        g_last = G[C - 1:C] if direction == 0 else G[0:1]
        wu = _dot(Tm, jnp.concatenate([kb * eG, v * beta], axis=1))
        yield
        kd = k * jnp.exp(g_last - G)
        att = qk * L
        kd_wu = _dot_tn(kd, wu)
        att_wu = _dot(att, wu)
        return ((-kd_wu[:, :LANES]).astype(BF16), kd_wu[:, LANES:],
                (q * eG - att_wu[:, :LANES]).astype(BF16), att_wu[:, LANES:],
                jnp.broadcast_to(jnp.exp(g_last), (SUBLANES, LANES)))

    def run_staged(generators):
        results = [None] * len(generators)
        live = list(enumerate(generators))
        while live:
            still = []
            for idx, gen in live:
                try:
                    next(gen)
                    still.append((idx, gen))
                except StopIteration as done:
                    results[idx] = done.value
            live = still
        return results

    slots = (wp_ref, bm_ref, qp_ref, op_ref, dec_ref)
    nb = min(DN_BLOCK, n_chunks)
    n_blocks = n_chunks // nb
    per_half = 2 * nb * U

    def block_rows(i):
        return ([pl.multiple_of((i * nb + j) * C, C) for j in range(nb)]
                + [pl.multiple_of((n_chunks - 1 - (i * nb + j)) * C, C) for j in range(nb)])

    def prep_block(i):
        rows = block_rows(i)
        return [prep(load(r, hh), s // nb, hh) for hh in range(U) for s, r in enumerate(rows)]

    def store_block(prepared, base):
        for e in range(per_half):
            for ref, val in zip(slots, prepared[e]):
                ref[base + e] = val

    def recurrence(states, base):
        states = list(states)
        outs = []
        for j in range(nb):
            step_out = []
            for hh in range(U):
                for direction in range(2):
                    e = base + hh * 2 * nb + direction * nb + j
                    S = states[2 * hh + direction]
                    S16 = S.astype(BF16)
                    step_out.append(_dot(qp_ref[e], S16) + op_ref[e])
                    states[2 * hh + direction] = S * dec_ref[e, 0:1] + _dot(wp_ref[e], S16) + bm_ref[e]
            outs.append(step_out)
            yield
        return outs, tuple(states)

    def store_outputs(i, outs):
        rows = block_rows(i)
        for j in range(nb):
            for hh in range(U):
                of_ref[pl.ds(rows[j], C), lanes_of[hh]] = outs[j][2 * hh]
                ob_ref[pl.ds(rows[nb + j], C), lanes_of[hh]] = outs[j][2 * hh + 1]

    state = tuple(s0_ref[0, direction, hh] for hh in range(U) for direction in range(2))
    if n_blocks == 1:
        store_block(run_staged(prep_block(0)), 0)
        (outs, state), = run_staged([recurrence(state, 0)])
        store_outputs(0, outs)
    else:
        store_block(run_staged(prep_block(0)), 0)

        def body(i, carry):
            cur = (i % 2) * per_half
            nxt = per_half - cur
            *prepared, (outs, carry) = run_staged(prep_block(i + 1) + [recurrence(carry, cur)])
            store_outputs(i, outs)
            store_block(prepared, nxt)
            return carry

        state = lax.fori_loop(0, n_blocks - 1, body, state)
        (outs, state), = run_staged([recurrence(state, ((n_blocks - 1) % 2) * per_half)])
        store_outputs(n_blocks - 1, outs)
    for hh in range(U):
        for direction in range(2):
            st_ref[0, direction, hh] = state[2 * hh + direction]

    fblk = min(FIN_BLOCK, T)

    def fin(i, carry):
        r0 = pl.multiple_of(i * fblk, fblk)
        o = [of_ref[pl.ds(r0, fblk), lanes_of[hh]] + ob_ref[pl.ds(r0, fblk), lanes_of[hh]] for hh in range(U)]
        ms = [jnp.mean(o[hh] * o[hh], axis=-1, keepdims=True) for hh in range(U)]
        for hh in range(U):
            y = o[hh] * lax.rsqrt(ms[hh] + EPS) * nw_ref[...]
            o_ref[pl.ds(r0, fblk), lanes_of[hh]] = y * _silu(g_ref[pl.ds(r0, fblk), lanes_of[hh]])
        return carry

    lax.fori_loop(0, T // fblk, fin, 0)


def _deltanet(dn, ab, conv_w, alog, dtb, nw, s0, *, batch, seq_len, n_heads):
    rows = dn.shape[0]
    T = seq_len
    C = DN_CHUNK
    U = n_heads
    nb = min(DN_BLOCK, T // C)
    n_slots = 4 * nb * U
    groups = DN_HEADS // U
    col = lambda j: pl.BlockSpec((T, U * LANES), lambda b, h, j=j: (b, groups * j + h))
    st_spec = pl.BlockSpec((1, 2, U, DN_DK, DN_DK), lambda b, h: (b, 0, h, 0, 0))
    gate_spec = pl.BlockSpec((2, U, 1, LANES), lambda b, h: (0, h, 0, 0))
    return pl.pallas_call(
        functools.partial(_dn_kernel, seq_len=T, n_heads=U),
        grid=(batch, groups),
        in_specs=[col(0), col(1), col(2), col(3),
                  pl.BlockSpec((T, LANES), lambda b, h: (b, 0)),
                  pl.BlockSpec((3, U, 3, LANES), lambda b, h: (0, h, 0, 0)),
                  gate_spec, gate_spec,
                  pl.BlockSpec((1, LANES), lambda b, h: (0, 0)),
                  st_spec],
        out_specs=[pl.BlockSpec((T, U * LANES), lambda b, h: (b, h)), st_spec],
        out_shape=[jax.ShapeDtypeStruct((rows, 512), F32),
                   jax.ShapeDtypeStruct((batch, 2, DN_HEADS, DN_DK, DN_DK), F32)],
        scratch_shapes=([pltpu.VMEM((T, U * LANES), F32)] * 5
                        + [pltpu.VMEM((n_slots, DN_DK, LANES), BF16), pltpu.VMEM((n_slots, DN_DK, LANES), F32),
                           pltpu.VMEM((n_slots, C, LANES), BF16), pltpu.VMEM((n_slots, C, LANES), F32),
                           pltpu.VMEM((n_slots, SUBLANES, LANES), F32)]),
        compiler_params=_cparams(2),
        name="deltanet",
    )(dn, dn, dn, dn, ab, conv_w, alog, dtb, nw, s0)


def _merge_ffn_kernel(x_ref, mod_ref, a_ref, r_ref, d_ref, mg_ref, wbr_ref, wo_ref,
                      nw_ref, win_ref, wout_ref, fnw_ref, o_ref, *, dff, final):
    d = x_ref.shape[-1]
    m = mod_ref[0]
    merged = jnp.zeros(x_ref.shape, F32)
    for i, br in enumerate((a_ref, r_ref, d_ref)):
        merged = merged + _sigmoid(mg_ref[:, i * d:(i + 1) * d]) * _dot(br[...], wbr_ref[i])
    x = x_ref[...] + m[5:6] * _dot(merged, wo_ref[...])
    o_ref[...] = _ffn_rows(x, m, nw_ref, win_ref, wout_ref, fnw_ref, mod_base=6, dff=dff, final=final)


def _merge_ffn(x, mod, a, r, dn, mg, w_br, w_o, nw, w_in, w_out, fnw, *, rows_per_cond, final, tm=512):
    rows, d = x.shape
    dff = w_out.shape[0]
    tiles_per_cond = rows_per_cond // tm
    row_spec = lambda wd: pl.BlockSpec((tm, wd), lambda i: (i, 0))
    return pl.pallas_call(
        functools.partial(_merge_ffn_kernel, dff=dff, final=final),
        grid=(rows // tm,),
        in_specs=[row_spec(d),
                  pl.BlockSpec((1, N_MOD, d), lambda i: (i // tiles_per_cond, 0, 0)),
                  row_spec(512), row_spec(512), row_spec(512), row_spec(MG_W),
                  _resident(w_br.shape), _resident(w_o.shape),
                  _resident((1, d)), _resident(w_in.shape), _resident(w_out.shape), _resident((1, d))],
        out_specs=row_spec(d),
        out_shape=jax.ShapeDtypeStruct((rows, d), F32),
        compiler_params=_cparams(1),
        name="merge_ffn",
    )(x, mod, a, r, dn, mg, w_br, w_o, nw, w_in, w_out, fnw)


def _rope_tables(seq_len):
    n_freq = HEAD_DIM // 4
    inv = ROPE_THETA ** (-np.arange(n_freq, dtype=np.float64) / n_freq)
    t = np.arange(seq_len)
    row = (t // GRID_W).astype(np.float64)
    colp = (t % GRID_W).astype(np.float64)
    ang = np.concatenate([row[:, None] * inv, colp[:, None] * inv], axis=-1)
    c, s = np.cos(ang), np.sin(ang)
    cos = np.concatenate([c, c, c, c], axis=-1)
    sin = np.concatenate([-s, s, -s, s], axis=-1)
    return jnp.asarray(cos, F32), jnp.asarray(sin, F32)


def _reorder_w_in(w):
    d = w.shape[0]
    o_da = 768 + 2048 + 1536
    o_dg = o_da + 4 * DN_HEADS
    o_mg = o_dg + 512
    parts = [w[:, :o_da], w[:, o_dg:o_mg], w[:, o_da:o_dg],
             jnp.zeros((d, AB_W - 4 * DN_HEADS), w.dtype), w[:, o_mg:]]
    return jnp.concatenate([p.astype(BF16) for p in parts], axis=1)


def kernel(x_prompt, x_sample, cache_k, cache_v, state_ret, state_delta, c, c_ctx,
           w_mod, b_mod, norm_ffn1, ffn1_w_in, ffn1_w_out, norm_mix, w_in,
           attn_q_norm, attn_k_norm, ret_norm, dn_conv, dn_a_log, dn_dt_bias, dn_norm,
           w_br_attn, w_br_ret, w_br_dn, w_out, norm_ffn2, ffn2_w_in, ffn2_w_out, norm_final):
    bp, tp, d = x_prompt.shape
    bs, ts, _ = x_sample.shape
    depth = w_mod.shape[0]
    past = cache_k.shape[2]

    conds = jnp.concatenate([c_ctx[None, :], c], axis=0)
    mod = _modulation(conds, w_mod, b_mod).reshape(depth, 1 + bs, N_MOD, d)

    ones_bd = jnp.asarray(np.kron(np.eye(ATTN_HEADS), np.ones((HEAD_DIM, HEAD_DIM))), BF16)
    ret_tabs = _retention_tables()
    rope_tabs = _rope_tables(ts)
    ret_zero = jnp.zeros((bp, 2, RET_HEADS, RET_DK, RET_DK), F32)
    dn_zero = jnp.zeros((bp, 2, DN_HEADS, DN_DK, DN_DK), F32)
    fnw = norm_final.reshape(1, d)

    groups = {
        "prompt": dict(x=x_prompt.reshape(bp * tp, d), batch=bp, seq=tp, rows_per_cond=bp * tp, rope=None,
                       dn_heads=DN_HEADS, ret_pairs=RET_HEADS // 2),
        "sample": dict(x=x_sample.reshape(bs * ts, d), batch=bs, seq=ts, rows_per_cond=ts, rope=rope_tabs,
                       dn_heads=1, ret_pairs=1),
    }
    new_k, new_v, new_rs, new_ds = [], [], [], []
    for l in range(depth):
        w1_in, w1_out = ffn1_w_in[l].astype(BF16), ffn1_w_out[l].astype(BF16)
        w2_in, w2_out = ffn2_w_in[l].astype(BF16), ffn2_w_out[l].astype(BF16)
        w_proj = _reorder_w_in(w_in[l])
        w_br = jnp.stack([w_br_attn[l], w_br_ret[l], w_br_dn[l]]).astype(BF16)
        w_o = w_out[l].astype(BF16)
        gq = jnp.tile(attn_q_norm[l], ATTN_HEADS).reshape(1, 512)
        gk = jnp.tile(attn_k_norm[l], ATTN_KV_HEADS).reshape(1, LANES)
        conv_w = dn_conv[l].reshape(3, 3, DN_HEADS, LANES).transpose(1, 2, 0, 3)
        alog = jnp.broadcast_to(dn_a_log[l][:, :, None, None], (2, DN_HEADS, 1, LANES))
        dtb = jnp.broadcast_to(dn_dt_bias[l][:, :, None, None], (2, DN_HEADS, 1, LANES))
        for name, grp in groups.items():
            is_prompt = name == "prompt"
            x = grp["x"]
            gmod = mod[l, :1] if is_prompt else mod[l, 1:]
            rpc = grp["rows_per_cond"]
            x = _ffn(x, gmod, norm_ffn1[l].reshape(1, d), w1_in, w1_out, fnw,
                     mod_base=0, rows_per_cond=rpc, final=False)
            attn, ret, dn, ab, mg = _inproj(x, gmod, norm_mix[l].reshape(1, d), w_proj, ones_bd, gq, gk,
                                            grp["rope"], rows_per_cond=rpc, seq_len=grp["seq"])
            if is_prompt:
                a_out = _attention(attn, None, None, batch=bp, seq_len=tp, tq=tp)
                rs0, ds0 = ret_zero, dn_zero
            else:
                a_out = _attention(attn, cache_k[:, l].reshape(bs, past, LANES),
                                   cache_v[:, l].reshape(bs, past, LANES), batch=bs, seq_len=ts)
                rs0, ds0 = state_ret[:, l], state_delta[:, l]
            r_out, rs = _retention(ret, rs0, ret_norm[l].reshape(1, 512), ret_tabs, ones_bd,
                                   batch=grp["batch"], seq_len=grp["seq"], n_pairs=grp["ret_pairs"])
            d_out, ds = _deltanet(dn, ab, conv_w, alog, dtb, dn_norm[l].reshape(1, LANES), ds0,
                                  batch=grp["batch"], seq_len=grp["seq"], n_heads=grp["dn_heads"])
            x = _merge_ffn(x, gmod, a_out, r_out, d_out, mg, w_br, w_o, norm_ffn2[l].reshape(1, d),
                           w2_in, w2_out, fnw, rows_per_cond=rpc, final=(l == depth - 1))
            grp["x"] = x
            if is_prompt:
                new_k.append(attn[:, 512:640].reshape(bp, tp, ATTN_KV_HEADS, HEAD_DIM))
                new_v.append(attn[:, 640:768].reshape(bp, tp, ATTN_KV_HEADS, HEAD_DIM))
                new_rs.append(rs)
                new_ds.append(ds)

    y_prompt = groups["prompt"]["x"].reshape(bp, tp, d)
    y_sample = groups["sample"]["x"].reshape(bs, ts, d)
    return (y_prompt, y_sample, jnp.stack(new_k, axis=1), jnp.stack(new_v, axis=1),
            jnp.stack(new_rs, axis=1), jnp.stack(new_ds, axis=1))
```

```python
import functools
import math

import numpy as np
import jax
import jax.numpy as jnp
from jax import lax
from jax.experimental import pallas as pl
from jax.experimental.pallas import tpu as pltpu

F32 = jnp.float32
BF16 = jnp.bfloat16

EPS = 1e-6
ROPE_THETA = 10000.0
GRID_W = 64
N_MOD = 9

ATTN_HEADS = 8
ATTN_KV_HEADS = 2
HEAD_DIM = 64
RET_HEADS = 8
RET_DK = 64
RET_CHUNK = 128
RET_DECAY_EXP_FWD = 5.0
RET_DECAY_EXP_BWD = 5.5
DN_HEADS = 4
DN_DK = 128
DN_CHUNK = 64
N_BRANCH = 3

LANES = 128
SUBLANES = 8
VMEM_LIMIT = 56 * 1024 * 1024

ATTN_W = 768
RET_W = 2048
DN_W = 2048
AB_W = 128
MG_W = 3072
IN_W = ATTN_W + RET_W + DN_W + AB_W + MG_W


def _cparams(n_axes):
    return pltpu.CompilerParams(dimension_semantics=("parallel",) * n_axes,
                                vmem_limit_bytes=VMEM_LIMIT)


def _resident(shape):
    zeros = (0,) * len(shape)
    return pl.BlockSpec(shape, lambda *_: zeros, pipeline_mode=pl.Buffered(1))


def _dot(a, b):
    return jnp.dot(a.astype(BF16), b.astype(BF16), preferred_element_type=F32)


def _dot_nt(a, b):
    return lax.dot_general(a.astype(BF16), b.astype(BF16), (((1,), (1,)), ((), ())),
                           preferred_element_type=F32)


def _dot_tn(a, b):
    return lax.dot_general(a.astype(BF16), b.astype(BF16), (((0,), (0,)), ((), ())),
                           preferred_element_type=F32)


def _sigmoid(x):
    return 1.0 / (1.0 + jnp.exp(-x))


def _silu(x):
    return x * _sigmoid(x)


def _norm_mod(x, nw, shift, scale):
    y = x * lax.rsqrt(jnp.mean(x * x, axis=-1, keepdims=True) + EPS) * nw
    return y * (1.0 + scale) + shift


def _mod_kernel(c_ref, w_ref, b_ref, o_ref):
    o_ref[0] = _dot(_silu(c_ref[...]), w_ref[0]) + b_ref[0]


def _modulation(conds, w_mod, b_mod):
    depth, d, n = w_mod.shape
    nc = conds.shape[0]
    tn = n // N_MOD
    return pl.pallas_call(
        _mod_kernel,
        grid=(depth, n // tn),
        in_specs=[pl.BlockSpec((nc, d), lambda l, j: (0, 0)),
                  pl.BlockSpec((1, d, tn), lambda l, j: (l, 0, j)),
                  pl.BlockSpec((1, 1, tn), lambda l, j: (l, 0, j))],
        out_specs=pl.BlockSpec((1, nc, tn), lambda l, j: (l, 0, j)),
        out_shape=jax.ShapeDtypeStruct((depth, nc, n), F32),
        compiler_params=_cparams(2),
        name="modulation",
    )(conds, w_mod, b_mod.reshape(depth, 1, n))


FFN_CHUNK = 256


def _ffn_rows(x, m, nw_ref, win_ref, wout_ref, fnw_ref, *, mod_base, dff, final):
    shift, scale, gate = (m[mod_base + i:mod_base + i + 1] for i in range(3))
    h = _norm_mod(x, nw_ref[...], shift, scale).astype(BF16)
    acc = jnp.zeros(x.shape, F32)
    for c in range(dff // FFN_CHUNK):
        lo = c * FFN_CHUNK
        hg = jnp.dot(h, win_ref[:, lo:lo + FFN_CHUNK], preferred_element_type=F32)
        hu = jnp.dot(h, win_ref[:, dff + lo:dff + lo + FFN_CHUNK], preferred_element_type=F32)
        a = (_silu(hg) * hu).astype(BF16)
        acc = acc + jnp.dot(a, wout_ref[lo:lo + FFN_CHUNK, :], preferred_element_type=F32)
    y = x + 0.5 * gate * acc
    if final:
        y = y * lax.rsqrt(jnp.mean(y * y, axis=-1, keepdims=True) + EPS) * fnw_ref[...]
    return y


def _ffn_kernel(x_ref, mod_ref, nw_ref, win_ref, wout_ref, fnw_ref, o_ref, *, mod_base, dff, final):
    o_ref[...] = _ffn_rows(x_ref[...], mod_ref[0], nw_ref, win_ref, wout_ref, fnw_ref,
                           mod_base=mod_base, dff=dff, final=final)


def _ffn(x, mod, nw, w_in, w_out, fnw, *, mod_base, rows_per_cond, final, tm=512):
    rows, d = x.shape
    dff = w_out.shape[0]
    tiles_per_cond = rows_per_cond // tm
    return pl.pallas_call(
        functools.partial(_ffn_kernel, mod_base=mod_base, dff=dff, final=final),
        grid=(rows // tm,),
        in_specs=[pl.BlockSpec((tm, d), lambda i: (i, 0)),
                  pl.BlockSpec((1, N_MOD, d), lambda i: (i // tiles_per_cond, 0, 0)),
                  _resident((1, d)),
                  _resident(w_in.shape),
                  _resident(w_out.shape),
                  _resident((1, d))],
        out_specs=pl.BlockSpec((tm, d), lambda i: (i, 0)),
        out_shape=jax.ShapeDtypeStruct((rows, d), F32),
        compiler_params=_cparams(1),
        name="ffn",
    )(x, mod, nw, w_in, w_out, fnw)


def _swap_halves(x):
    n = x.shape[-1]
    lane = lax.broadcasted_iota(jnp.int32, x.shape, 1)
    first = (lane % HEAD_DIM) < (HEAD_DIM // 2)
    return jnp.where(first, pltpu.roll(x, n - HEAD_DIM // 2, 1), pltpu.roll(x, HEAD_DIM // 2, 1))


def _rope(x, cos, sin):
    reps = x.shape[-1] // LANES
    c = jnp.concatenate([cos] * reps, axis=1) if reps > 1 else cos
    s = jnp.concatenate([sin] * reps, axis=1) if reps > 1 else sin
    return x * c + _swap_halves(x) * s


def _head_rms(x, ones_bd, gain):
    sq = x * x
    hi = sq.astype(BF16)
    lo = (sq - hi.astype(F32)).astype(BF16)
    ss = (jnp.dot(hi, ones_bd, preferred_element_type=F32)
          + jnp.dot(lo, ones_bd, preferred_element_type=F32))
    return x * lax.rsqrt(ss * (1.0 / HEAD_DIM) + EPS) * gain


def _inproj_kernel(*refs, rope):
    if rope:
        (x_ref, mod_ref, nw_ref, w_ref, ones_ref, gq_ref, gk_ref, cos_ref, sin_ref,
         attn_ref, ret_ref, dn_ref, ab_ref, mg_ref) = refs
    else:
        (x_ref, mod_ref, nw_ref, w_ref, ones_ref, gq_ref, gk_ref,
         attn_ref, ret_ref, dn_ref, ab_ref, mg_ref) = refs
    m = mod_ref[0]
    h = _norm_mod(x_ref[...], nw_ref[...], m[3:4], m[4:5]).astype(BF16)

    def proj(lo, width):
        return jnp.dot(h, w_ref[:, lo:lo + width], preferred_element_type=F32)

    def rot(v):
        return _rope(v, cos_ref[...], sin_ref[...]) if rope else v

    a = proj(0, ATTN_W)
    q = _head_rms(a[:, :512], ones_ref[...], gq_ref[...])
    k = _head_rms(a[:, 512:640], ones_ref[:LANES, :LANES], gk_ref[...])
    attn_ref[:, :512] = rot(q) * (HEAD_DIM ** -0.5)
    attn_ref[:, 512:640] = rot(k)
    attn_ref[:, 640:768] = a[:, 640:768]

    r = proj(ATTN_W, RET_W)
    ret_ref[:, :512] = rot(r[:, :512]) * (RET_DK ** -0.5)
    ret_ref[:, 512:1024] = rot(r[:, 512:1024])
    ret_ref[:, 1024:] = r[:, 1024:]

    dn_ref[...] = proj(ATTN_W + RET_W, DN_W)
    ab_ref[...] = proj(ATTN_W + RET_W + DN_W, AB_W)
    mg_ref[...] = proj(ATTN_W + RET_W + DN_W + AB_W, MG_W)


def _inproj(x, mod, nw, w, ones_bd, gq, gk, rope_tabs, *, rows_per_cond, seq_len, tm=256):
    rows, d = x.shape
    tiles_per_cond = rows_per_cond // tm
    tiles_per_seq = seq_len // tm
    rope = rope_tabs is not None
    in_specs = [pl.BlockSpec((tm, d), lambda i: (i, 0)),
                pl.BlockSpec((1, N_MOD, d), lambda i: (i // tiles_per_cond, 0, 0)),
                _resident((1, d)),
                _resident(w.shape),
                _resident(ones_bd.shape),
                _resident(gq.shape),
                _resident(gk.shape)]
    args = [x, mod, nw, w, ones_bd, gq, gk]
    if rope:
        in_specs += [pl.BlockSpec((tm, LANES), lambda i: (i % tiles_per_seq, 0))] * 2
        args += list(rope_tabs)
    widths = (ATTN_W, RET_W, DN_W, AB_W, MG_W)
    return pl.pallas_call(
        functools.partial(_inproj_kernel, rope=rope),
        grid=(rows // tm,),
        in_specs=in_specs,
        out_specs=[pl.BlockSpec((tm, wd), lambda i: (i, 0)) for wd in widths],
        out_shape=[jax.ShapeDtypeStruct((rows, wd), F32) for wd in widths],
        compiler_params=_cparams(1),
        name="inproj",
    )(*args)


ATTN_GROUP = ATTN_HEADS // ATTN_KV_HEADS
LOG2E = math.log2(math.e)
ATTN_MAX_CHUNKS = 8
ATTN_EPILOGUE_DELAY = 2
ATTN_ROW_BLOCKS = 2


def _attn_kernel(*refs, tq, n_tiles, kv_chunk, n_chunks, has_ctx):
    if has_ctx:
        q_ref, k_ref, v_ref, ck_ref, cv_ref, o_ref, mx_ref, ls_ref, acc_ref, qs_ref = refs
    else:
        q_ref, k_ref, v_ref, o_ref, mx_ref, ls_ref, acc_ref, qs_ref = refs
    lane = lax.broadcasted_iota(jnp.int32, (tq, LANES), 1)
    low = lane < HEAD_DIM
    stacked = ATTN_HEADS * tq
    rbs = stacked // ATTN_ROW_BLOCKS

    def prologue(t):
        parts = []
        for hd in range(ATTN_HEADS):
            g = hd // ATTN_GROUP
            blk = q_ref[t * tq:(t + 1) * tq, (hd // 2) * LANES:(hd // 2 + 1) * LANES] * LOG2E
            if hd % 2 != g:
                blk = pltpu.roll(blk, HEAD_DIM, 1)
            parts.append(jnp.where(low if g == 0 else jnp.logical_not(low), blk, 0.0))
        rows = slice(t * stacked, (t + 1) * stacked)
        qs_ref[rows] = jnp.concatenate(parts, axis=0).astype(BF16)
        mx_ref[rows] = jnp.full((stacked, LANES), -jnp.inf, F32)
        ls_ref[rows] = jnp.zeros((stacked, LANES), F32)
        acc_ref[rows] = jnp.zeros((stacked, LANES), F32)

    def epilogue(t):
        rows = slice(t * stacked, (t + 1) * stacked)
        res = acc_ref[rows] / jnp.sum(ls_ref[rows], axis=-1, keepdims=True)
        for b in range(ATTN_HEADS // 2):
            g = (2 * b) // ATTN_GROUP
            even = res[2 * b * tq:(2 * b + 1) * tq]
            odd = res[(2 * b + 1) * tq:(2 * b + 2) * tq]
            if g == 0:
                odd = pltpu.roll(odd, HEAD_DIM, 1)
            else:
                even = pltpu.roll(even, HEAD_DIM, 1)
            o_ref[t * tq:(t + 1) * tq, b * LANES:(b + 1) * LANES] = jnp.where(low, even, odd)

    def lane_fold(op, acc, x):
        for b in range(x.shape[1] // LANES):
            acc = op(acc, x[:, b * LANES:(b + 1) * LANES])
        return acc

    def scores(sl, k):
        return _dot_nt(qs_ref[sl], k)

    def update(sl, s, v):
        reps = s.shape[1] // LANES
        m_old = mx_ref[sl]
        cmax = lane_fold(jnp.maximum, s[:, :LANES], s[:, LANES:])
        m_new = jnp.maximum(m_old, jnp.max(cmax, axis=-1, keepdims=True))
        alpha = jnp.exp2(m_old - m_new)
        p = jnp.exp2(s - jnp.concatenate([m_new] * reps, axis=1))
        mx_ref[sl] = m_new
        ls_ref[sl] = alpha * ls_ref[sl] + lane_fold(jnp.add, p[:, :LANES], p[:, LANES:])
        acc_ref[sl] = alpha * acc_ref[sl] + _dot(p, v)

    for t in range(n_tiles):
        prologue(t)
    kvs = [(k_ref[c * kv_chunk:(c + 1) * kv_chunk, :].astype(BF16),
            v_ref[c * kv_chunk:(c + 1) * kv_chunk, :].astype(BF16)) for c in range(n_chunks)]
    if has_ctx:
        kvs.append((ck_ref[0].astype(BF16), cv_ref[0].astype(BF16)))
    units = [(t, slice(t * stacked + r * rbs, t * stacked + (r + 1) * rbs), k, v)
             for t in range(n_tiles) for k, v in kvs for r in range(ATTN_ROW_BLOCKS)]
    last_of = {t: max(i for i, u in enumerate(units) if u[0] == t) for t in range(n_tiles)}
    ahead = None
    due = []
    for idx in range(len(units) + 1):
        nxt = None
        if idx < len(units):
            t, sl, k, v = units[idx]
            nxt = (idx, sl, scores(sl, k), v)
        if ahead is not None:
            done, sl, s, v = ahead
            update(sl, s, v)
            for t in range(n_tiles):
                if last_of[t] == done:
                    due.append((done + ATTN_EPILOGUE_DELAY, t))
        for pos, t in list(due):
            if pos <= idx or nxt is None:
                epilogue(t)
                due.remove((pos, t))
        ahead = nxt


def _attention(attn, ctx_k, ctx_v, *, batch, seq_len, tq, n_tiles):
    rows = attn.shape[0]
    kv_chunk = min(seq_len, 512)
    n_chunks = seq_len // kv_chunk
    assert n_chunks <= ATTN_MAX_CHUNKS, "the key loop is fully unrolled"
    step = tq * n_tiles
    qt = seq_len // step
    stacked = ATTN_HEADS * step
    has_ctx = ctx_k is not None
    in_specs = [pl.BlockSpec((step, 512), lambda b, i: (b * qt + i, 0)),
                pl.BlockSpec((seq_len, LANES), lambda b, i: (b, 4)),
                pl.BlockSpec((seq_len, LANES), lambda b, i: (b, 5))]
    args = [attn, attn, attn]
    if has_ctx:
        past = ctx_k.shape[1]
        in_specs += [pl.BlockSpec((1, past, LANES), lambda b, i: (b, 0, 0))] * 2
        args += [ctx_k, ctx_v]
    scratch = [pltpu.VMEM((stacked, LANES), F32)] * 3 + [pltpu.VMEM((stacked, LANES), BF16)]
    return pl.pallas_call(
        functools.partial(_attn_kernel, tq=tq, n_tiles=n_tiles, kv_chunk=kv_chunk, n_chunks=n_chunks,
                          has_ctx=has_ctx),
        grid=(batch, qt),
        in_specs=in_specs,
        out_specs=pl.BlockSpec((step, 512), lambda b, i: (b * qt + i, 0)),
        out_shape=jax.ShapeDtypeStruct((rows, 512), F32),
        scratch_shapes=scratch,
        compiler_params=_cparams(2),
        name="attention",
    )(*args)


def _retention_tables():
    C = RET_CHUNK
    h = np.arange(RET_HEADS, dtype=np.float64)
    pos = np.arange(C, dtype=np.float64)
    diff = pos[:, None] - pos[None, :]
    inner, qd, kd, cd = [], [], [], []
    for direction, expo in enumerate((RET_DECAY_EXP_FWD, RET_DECAY_EXP_BWD)):
        lg = np.log1p(-np.exp2(-expo - h))[:, None, None]
        if direction == 0:
            inner.append(np.where(diff >= 0, np.exp(lg * np.maximum(diff, 0.0)), 0.0))
            qd.append(np.exp(lg[:, :, 0] * (pos + 1.0)))
            kd.append(np.exp(lg[:, :, 0] * (C - 1.0 - pos)))
        else:
            inner.append(np.where(diff <= 0, np.exp(lg * np.maximum(-diff, 0.0)), 0.0))
            qd.append(np.exp(lg[:, :, 0] * (C - pos)))
            kd.append(np.exp(lg[:, :, 0] * pos))
        cd.append(np.exp(lg[:, 0, 0] * C))
    inner = np.stack(inner, axis=1)
    rowdec = np.stack([np.stack(qd, 1), np.stack(kd, 1)], axis=2)
    rowdec = np.repeat(rowdec[..., None], RET_DK, axis=-1)
    rowdec = np.concatenate([rowdec[0::2], rowdec[1::2]], axis=-1)
    cd = np.stack(cd, axis=1)
    block = np.kron(np.eye(2), np.ones((RET_DK, RET_DK)))
    per_row = np.repeat(np.stack([cd[0::2], cd[1::2]], axis=-1), RET_DK, axis=-1)
    cd = per_row[:, :, :, None] * block
    return (jnp.asarray(inner, F32), jnp.asarray(rowdec, F32), jnp.asarray(cd, F32))


RET_BLOCK = 16


def _split_bf16(x):
    hi = x.astype(BF16)
    return hi, (x - hi.astype(F32)).astype(BF16)


def _ret_kernel(q_ref, k_ref, v_ref, g_ref, inner_ref, dec_ref, cd_ref, ones_ref, s0_ref, nw_ref,
                o_ref, st_ref, *, n_chunks, n_pairs):
    C = RET_CHUNK
    D = RET_DK
    nb = min(RET_BLOCK, n_chunks)
    lanes_of = [slice(p * LANES, (p + 1) * LANES) for p in range(n_pairs)]
    lane = lax.broadcasted_iota(jnp.int32, (C, LANES), 1)
    first = lane < D
    ones = ones_ref[...]

    def head_mean(x):
        hi, lo = _split_bf16(x)
        return (jnp.dot(hi, ones, preferred_element_type=F32)
                + jnp.dot(lo, ones, preferred_element_type=F32)) * (1.0 / D)

    def sweep(direction):
        def body(i, states):
            order = [i * nb + j for j in range(nb)]
            if direction == 1:
                order = [n_chunks - 1 - c for c in order]
            units = [(p, pl.multiple_of(c * C, C)) for p in range(n_pairs) for c in order]
            n = len(units)
            q = [q_ref[pl.ds(r, C), lanes_of[p]] for p, r in units]
            k = [k_ref[pl.ds(r, C), lanes_of[p]] for p, r in units]
            v = [v_ref[pl.ds(r, C), lanes_of[p]] for p, r in units]
            if direction == 1:
                prev = [o_ref[pl.ds(r, C), lanes_of[p]] for p, r in units]
                gate = [g_ref[pl.ds(r, C), lanes_of[p]] for p, r in units]
            att = [jnp.concatenate(
                [_dot_nt(jnp.where(first, q[e], 0.0), k[e]) * inner_ref[2 * units[e][0], direction],
                 _dot_nt(jnp.where(first, 0.0, q[e]), k[e]) * inner_ref[2 * units[e][0] + 1, direction]],
                axis=1) for e in range(n)]
            kv = [_dot_tn(k[e] * dec_ref[units[e][0], direction, 1], v[e]) for e in range(n)]
            seen = []
            states = list(states)
            for e, (p, _) in enumerate(units):
                cd = cd_ref[p, direction]
                seen.append(states[p])
                states[p] = states[p] * cd + jnp.where(cd != 0.0, kv[e], 0.0)
            v2 = [jnp.concatenate([jnp.where(first, v[e], 0.0), jnp.where(first, 0.0, v[e])], axis=0)
                  for e in range(n)]
            o = [_dot(att[e], v2[e]) + _dot(q[e] * dec_ref[units[e][0], direction, 0], seen[e])
                 for e in range(n)]
            if direction == 1:
                o = [o[e] + prev[e] for e in range(n)]
                mean = [head_mean(o[e]) for e in range(n)]
                d = [o[e] - mean[e] for e in range(n)]
                var = [head_mean(d[e] * d[e]) for e in range(n)]
                o = [d[e] * lax.rsqrt(var[e] + EPS) * nw_ref[:, lanes_of[units[e][0]]] * _silu(gate[e])
                     for e in range(n)]
            for e, (p, r) in enumerate(units):
                o_ref[pl.ds(r, C), lanes_of[p]] = o[e]
            return tuple(states)

        zero = jnp.zeros((D, D), F32)
        states = tuple(
            jnp.concatenate([jnp.concatenate([s0_ref[0, direction, 2 * p], zero], axis=1),
                             jnp.concatenate([zero, s0_ref[0, direction, 2 * p + 1]], axis=1)], axis=0)
            for p in range(n_pairs))
        states = lax.fori_loop(0, n_chunks // nb, body, states)
        for p in range(n_pairs):
            st_ref[0, direction, 2 * p] = states[p][:D, :D]
            st_ref[0, direction, 2 * p + 1] = states[p][D:, D:]

    sweep(0)
    sweep(1)


def _retention(ret, s0, nw, tables, ones_bd, *, batch, seq_len, n_pairs):
    rows = ret.shape[0]
    inner, rowdec, cd = tables
    C = RET_CHUNK
    U = n_pairs
    groups = RET_HEADS // 2 // U
    col = lambda j: pl.BlockSpec((seq_len, U * LANES), lambda b, hp, j=j: (b, groups * j + hp))
    st_spec = pl.BlockSpec((1, 2, 2 * U, RET_DK, RET_DK), lambda b, hp: (b, 0, hp, 0, 0))
    return pl.pallas_call(
        functools.partial(_ret_kernel, n_chunks=seq_len // C, n_pairs=U),
        grid=(batch, groups),
        in_specs=[col(0), col(1), col(2), col(3),
                  pl.BlockSpec((2 * U, 2, C, C), lambda b, hp: (hp, 0, 0, 0)),
                  pl.BlockSpec((U, 2, 2, C, LANES), lambda b, hp: (hp, 0, 0, 0, 0)),
                  pl.BlockSpec((U, 2, LANES, LANES), lambda b, hp: (hp, 0, 0, 0)),
                  pl.BlockSpec((LANES, LANES), lambda b, hp: (0, 0)),
                  st_spec,
                  pl.BlockSpec((1, U * LANES), lambda b, hp: (0, hp))],
        out_specs=[pl.BlockSpec((seq_len, U * LANES), lambda b, hp: (b, hp)), st_spec],
        out_shape=[jax.ShapeDtypeStruct((rows, 512), F32),
                   jax.ShapeDtypeStruct((batch, 2, RET_HEADS, RET_DK, RET_DK), F32)],
        compiler_params=_cparams(2),
        name="retention",
    )(ret, ret, ret, ret, inner, rowdec, cd, ones_bd, s0, nw)


DN_BLOCK = 8
SOLVE_BASE = 8
CONV_BLOCK = 256
FIN_BLOCK = 1024


def _dn_kernel(q_ref, k_ref, v_ref, g_ref, ab_ref, cw_ref, alog_ref, dtb_ref, nw_ref, s0_ref,
               o_ref, st_ref, qs_ref, ks_ref, vs_ref, of_ref, ob_ref,
               wp_ref, bm_ref, qp_ref, op_ref, dec_ref, *, seq_len, n_heads):
    C = DN_CHUNK
    T = seq_len
    U = n_heads
    n_chunks = T // C
    head0 = pl.program_id(1) * U
    lanes_of = [slice(hh * LANES, (hh + 1) * LANES) for hh in range(U)]

    blk = min(CONV_BLOCK, T)
    row = lax.broadcasted_iota(jnp.int32, (blk, LANES), 0)
    tensors = [(src, dst, t, hh) for hh in range(U)
               for t, (src, dst) in enumerate(((q_ref, qs_ref), (k_ref, ks_ref), (v_ref, vs_ref)))]

    def conv(i, carry):
        r0 = pl.multiple_of(i * blk, blk)
        above = pl.multiple_of(jnp.maximum(r0 - SUBLANES, 0), SUBLANES)
        below = pl.multiple_of(jnp.minimum(r0 + blk, T - SUBLANES), SUBLANES)
        x = [src[pl.ds(r0, blk), lanes_of[hh]] for src, _, _, hh in tensors]
        up = [jnp.where(r0 > 0, src[pl.ds(above, SUBLANES), lanes_of[hh]], 0.0) for src, _, _, hh in tensors]
        dn = [jnp.where(r0 + blk < T, src[pl.ds(below, SUBLANES), lanes_of[hh]], 0.0)
              for src, _, _, hh in tensors]
        n = len(tensors)
        padded = [jnp.concatenate([up[e], x[e], dn[e]], axis=0) for e in range(n)]
        prev = [padded[e][SUBLANES - 1:SUBLANES - 1 + blk] for e in range(n)]
        nxt = [padded[e][SUBLANES + 1:SUBLANES + 1 + blk] for e in range(n)]
        w = [cw_ref[t, hh] for _, _, t, hh in tensors]
        y = [_silu(w[e][0:1] * prev[e] + w[e][1:2] * x[e] + w[e][2:3] * nxt[e]) for e in range(n)]
        for e, (_, dst, t, hh) in enumerate(tensors):
            if t < 2:
                scale = lax.rsqrt(jnp.sum(y[e] * y[e], axis=-1, keepdims=True) + EPS)
                y[e] = y[e] * (scale * (DN_DK ** -0.5) if t == 0 else scale)
        for e, (_, dst, t, hh) in enumerate(tensors):
            dst[pl.ds(r0, blk), lanes_of[hh]] = y[e]
        return carry

    lax.fori_loop(0, T // blk, conv, 0)

    ri = lax.broadcasted_iota(jnp.int32, (C, C), 0)
    ci = lax.broadcasted_iota(jnp.int32, (C, C), 1)
    eye = ri == ci
    eye_f = eye.astype(F32)
    lane = lax.broadcasted_iota(jnp.int32, (1, LANES), 1)
    masks = ((ri >= ci, ri > ci), (ri <= ci, ri < ci))
    incl_bf = tuple(m[0].astype(F32).astype(BF16) for m in masks)
    base_blocks = (ri // SOLVE_BASE) == (ci // SOLVE_BASE)
    level_masks = []
    size = SOLVE_BASE
    while size < C:
        level_masks.append(jnp.logical_and((ri // (2 * size)) == (ci // (2 * size)),
                                           (ri // size) != (ci // size)))
        size *= 2

    def load(r0, hh):
        return (ab_ref[pl.ds(r0, C), :],) + tuple(ref[pl.ds(r0, C), lanes_of[hh]]
                                                   for ref in (qs_ref, ks_ref, vs_ref))

    def prep(operands, direction, hh):
        ab, q, k, v = operands
        incl, strict = masks[direction]
        sel_a = (lane == direction * DN_HEADS + head0 + hh).astype(F32)
        sel_b = (lane == 2 * DN_HEADS + direction * DN_HEADS + head0 + hh).astype(F32)
        da = jnp.sum(ab * sel_a, axis=-1, keepdims=True)
        db = jnp.sum(ab * sel_b, axis=-1, keepdims=True)
        z = da + dtb_ref[direction, hh]
        softplus = jnp.maximum(z, 0.0) + jnp.log1p(jnp.exp(-jnp.abs(z)))
        g = -jnp.exp(alog_ref[direction, hh]) * softplus
        beta = _sigmoid(db)
        kb = k * beta
        g1 = g.astype(BF16)
        g2 = (g - g1.astype(F32)).astype(BF16)
        g3 = (g - g1.astype(F32) - g2.astype(F32)).astype(BF16)
        G12 = _dot(incl_bf[direction], jnp.concatenate([g1, g2], axis=1))
        G = G12[:, :LANES] + G12[:, LANES:] + _dot(incl_bf[direction], g3)
        kk = _dot_nt(kb, k)
        qk = _dot_nt(q, k)
        yield
        Gc = G[:, :C]
        Grow = jnp.sum(jnp.where(eye, Gc, 0.0), axis=0, keepdims=True)
        L = jnp.where(incl, jnp.exp(jnp.where(incl, Gc - Grow, 0.0)), 0.0)
        N = jnp.where(strict, -(kk * L), 0.0)
        P = jnp.where(base_blocks, N, 0.0)
        Tm = eye_f + P
        P = _dot(P, P)
        yield
        for _ in range(int(math.log2(SOLVE_BASE)) - 2):
            Tm, P = Tm + _dot(Tm, P), _dot(P, P)
            yield
        Tm = Tm + _dot(Tm, P)
        yield
        for off_blocks in level_masks:
            TX = _dot(Tm, jnp.where(off_blocks, N, 0.0))
            yield
            Tm = Tm + _dot(TX, Tm)
            yield
        eG = jnp.exp(G)
        g_last = G[C - 1:C] if direction == 0 else G[0:1]
        wu = _dot(Tm, jnp.concatenate([kb * eG, v * beta], axis=1))
        yield
        kd = k * jnp.exp(g_last - G)
        att = qk * L
        kd_wu = _dot_tn(kd, wu)
        att_wu = _dot(att, wu)
        return ((-kd_wu[:, :LANES]).astype(BF16), kd_wu[:, LANES:],
                (q * eG - att_wu[:, :LANES]).astype(BF16), att_wu[:, LANES:],
                jnp.broadcast_to(jnp.exp(g_last), (SUBLANES, LANES)))

    def run_staged(generators):
        results = [None] * len(generators)
        live = list(enumerate(generators))
        while live:
            still = []
            for idx, gen in live:
                try:
                    next(gen)
                    still.append((idx, gen))
                except StopIteration as done:
                    results[idx] = done.value
            live = still
        return results

    slots = (wp_ref, bm_ref, qp_ref, op_ref, dec_ref)
    nb = min(DN_BLOCK, n_chunks)
    n_blocks = n_chunks // nb
    per_half = 2 * nb * U

    def block_rows(i):
        return ([pl.multiple_of((i * nb + j) * C, C) for j in range(nb)]
                + [pl.multiple_of((n_chunks - 1 - (i * nb + j)) * C, C) for j in range(nb)])

    def prep_block(i):
        rows = block_rows(i)
        return [prep(load(r, hh), s // nb, hh) for hh in range(U) for s, r in enumerate(rows)]

    def store_block(prepared, base):
        for e in range(per_half):
            for ref, val in zip(slots, prepared[e]):
                ref[base + e] = val

    def recurrence(states, base):
        states = list(states)
        outs = []
        for j in range(nb):
            step_out = []
            for hh in range(U):
                for direction in range(2):
                    e = base + hh * 2 * nb + direction * nb + j
                    S = states[2 * hh + direction]
                    S16 = S.astype(BF16)
                    step_out.append(_dot(qp_ref[e], S16) + op_ref[e])
                    states[2 * hh + direction] = S * dec_ref[e, 0:1] + _dot(wp_ref[e], S16) + bm_ref[e]
            outs.append(step_out)
            yield
        return outs, tuple(states)

    def store_outputs(i, outs):
        rows = block_rows(i)
        for j in range(nb):
            for hh in range(U):
                of_ref[pl.ds(rows[j], C), lanes_of[hh]] = outs[j][2 * hh]
                ob_ref[pl.ds(rows[nb + j], C), lanes_of[hh]] = outs[j][2 * hh + 1]

    state = tuple(s0_ref[0, direction, hh] for hh in range(U) for direction in range(2))
    if n_blocks == 1:
        store_block(run_staged(prep_block(0)), 0)
        (outs, state), = run_staged([recurrence(state, 0)])
        store_outputs(0, outs)
    else:
        store_block(run_staged(prep_block(0)), 0)

        def body(i, carry):
            cur = (i % 2) * per_half
            nxt = per_half - cur
            *prepared, (outs, carry) = run_staged(prep_block(i + 1) + [recurrence(carry, cur)])
            store_outputs(i, outs)
            store_block(prepared, nxt)
            return carry

        state = lax.fori_loop(0, n_blocks - 1, body, state)
        (outs, state), = run_staged([recurrence(state, ((n_blocks - 1) % 2) * per_half)])
        store_outputs(n_blocks - 1, outs)
    for hh in range(U):
        for direction in range(2):
            st_ref[0, direction, hh] = state[2 * hh + direction]

    fblk = min(FIN_BLOCK, T)

    def fin(i, carry):
        r0 = pl.multiple_of(i * fblk, fblk)
        o = [of_ref[pl.ds(r0, fblk), lanes_of[hh]] + ob_ref[pl.ds(r0, fblk), lanes_of[hh]] for hh in range(U)]
        ms = [jnp.mean(o[hh] * o[hh], axis=-1, keepdims=True) for hh in range(U)]
        for hh in range(U):
            y = o[hh] * lax.rsqrt(ms[hh] + EPS) * nw_ref[...]
            o_ref[pl.ds(r0, fblk), lanes_of[hh]] = y * _silu(g_ref[pl.ds(r0, fblk), lanes_of[hh]])
        return carry

    lax.fori_loop(0, T // fblk, fin, 0)


def _deltanet(dn, ab, conv_w, alog, dtb, nw, s0, *, batch, seq_len, n_heads):
    rows = dn.shape[0]
    T = seq_len
    C = DN_CHUNK
    U = n_heads
    nb = min(DN_BLOCK, T // C)
    n_slots = 4 * nb * U
    groups = DN_HEADS // U
    col = lambda j: pl.BlockSpec((T, U * LANES), lambda b, h, j=j: (b, groups * j + h))
    st_spec = pl.BlockSpec((1, 2, U, DN_DK, DN_DK), lambda b, h: (b, 0, h, 0, 0))
    gate_spec = pl.BlockSpec((2, U, 1, LANES), lambda b, h: (0, h, 0, 0))
    return pl.pallas_call(
        functools.partial(_dn_kernel, seq_len=T, n_heads=U),
        grid=(batch, groups),
        in_specs=[col(0), col(1), col(2), col(3),
                  pl.BlockSpec((T, LANES), lambda b, h: (b, 0)),
                  pl.BlockSpec((3, U, 3, LANES), lambda b, h: (0, h, 0, 0)),
                  gate_spec, gate_spec,
                  pl.BlockSpec((1, LANES), lambda b, h: (0, 0)),
                  st_spec],
        out_specs=[pl.BlockSpec((T, U * LANES), lambda b, h: (b, h)), st_spec],
        out_shape=[jax.ShapeDtypeStruct((rows, 512), F32),
                   jax.ShapeDtypeStruct((batch, 2, DN_HEADS, DN_DK, DN_DK), F32)],
        scratch_shapes=([pltpu.VMEM((T, U * LANES), F32)] * 5
                        + [pltpu.VMEM((n_slots, DN_DK, LANES), BF16), pltpu.VMEM((n_slots, DN_DK, LANES), F32),
                           pltpu.VMEM((n_slots, C, LANES), BF16), pltpu.VMEM((n_slots, C, LANES), F32),
                           pltpu.VMEM((n_slots, SUBLANES, LANES), F32)]),
        compiler_params=_cparams(2),
        name="deltanet",
    )(dn, dn, dn, dn, ab, conv_w, alog, dtb, nw, s0)


def _merge_ffn_kernel(x_ref, mod_ref, a_ref, r_ref, d_ref, mg_ref, wbr_ref, wo_ref,
                      nw_ref, win_ref, wout_ref, fnw_ref, o_ref, *, dff, final):
    d = x_ref.shape[-1]
    m = mod_ref[0]
    merged = jnp.zeros(x_ref.shape, F32)
    for i, br in enumerate((a_ref, r_ref, d_ref)):
        merged = merged + _sigmoid(mg_ref[:, i * d:(i + 1) * d]) * _dot(br[...], wbr_ref[i])
    x = x_ref[...] + m[5:6] * _dot(merged, wo_ref[...])
    o_ref[...] = _ffn_rows(x, m, nw_ref, win_ref, wout_ref, fnw_ref, mod_base=6, dff=dff, final=final)


def _merge_ffn(x, mod, a, r, dn, mg, w_br, w_o, nw, w_in, w_out, fnw, *, rows_per_cond, final, tm=512):
    rows, d = x.shape
    dff = w_out.shape[0]
    tiles_per_cond = rows_per_cond // tm
    row_spec = lambda wd: pl.BlockSpec((tm, wd), lambda i: (i, 0))
    return pl.pallas_call(
        functools.partial(_merge_ffn_kernel, dff=dff, final=final),
        grid=(rows // tm,),
        in_specs=[row_spec(d),
                  pl.BlockSpec((1, N_MOD, d), lambda i: (i // tiles_per_cond, 0, 0)),
                  row_spec(512), row_spec(512), row_spec(512), row_spec(MG_W),
                  _resident(w_br.shape), _resident(w_o.shape),
                  _resident((1, d)), _resident(w_in.shape), _resident(w_out.shape), _resident((1, d))],
        out_specs=row_spec(d),
        out_shape=jax.ShapeDtypeStruct((rows, d), F32),
        compiler_params=_cparams(1),
        name="merge_ffn",
    )(x, mod, a, r, dn, mg, w_br, w_o, nw, w_in, w_out, fnw)


def _rope_tables(seq_len):
    n_freq = HEAD_DIM // 4
    inv = ROPE_THETA ** (-np.arange(n_freq, dtype=np.float64) / n_freq)
    t = np.arange(seq_len)
    row = (t // GRID_W).astype(np.float64)
    colp = (t % GRID_W).astype(np.float64)
    ang = np.concatenate([row[:, None] * inv, colp[:, None] * inv], axis=-1)
    c, s = np.cos(ang), np.sin(ang)
    cos = np.concatenate([c, c, c, c], axis=-1)
    sin = np.concatenate([-s, s, -s, s], axis=-1)
    return jnp.asarray(cos, F32), jnp.asarray(sin, F32)


def _reorder_w_in(w):
    d = w.shape[0]
    o_da = 768 + 2048 + 1536
    o_dg = o_da + 4 * DN_HEADS
    o_mg = o_dg + 512
    parts = [w[:, :o_da], w[:, o_dg:o_mg], w[:, o_da:o_dg],
             jnp.zeros((d, AB_W - 4 * DN_HEADS), w.dtype), w[:, o_mg:]]
    return jnp.concatenate([p.astype(BF16) for p in parts], axis=1)


def kernel(x_prompt, x_sample, cache_k, cache_v, state_ret, state_delta, c, c_ctx,
           w_mod, b_mod, norm_ffn1, ffn1_w_in, ffn1_w_out, norm_mix, w_in,
           attn_q_norm, attn_k_norm, ret_norm, dn_conv, dn_a_log, dn_dt_bias, dn_norm,
           w_br_attn, w_br_ret, w_br_dn, w_out, norm_ffn2, ffn2_w_in, ffn2_w_out, norm_final):
    bp, tp, d = x_prompt.shape
    bs, ts, _ = x_sample.shape
    depth = w_mod.shape[0]
    past = cache_k.shape[2]

    conds = jnp.concatenate([c_ctx[None, :], c], axis=0)
    mod = _modulation(conds, w_mod, b_mod).reshape(depth, 1 + bs, N_MOD, d)

    ones_bd = jnp.asarray(np.kron(np.eye(ATTN_HEADS), np.ones((HEAD_DIM, HEAD_DIM))), BF16)
    ret_tabs = _retention_tables()
    rope_tabs = _rope_tables(ts)
    ret_zero = jnp.zeros((bp, 2, RET_HEADS, RET_DK, RET_DK), F32)
    dn_zero = jnp.zeros((bp, 2, DN_HEADS, DN_DK, DN_DK), F32)
    fnw = norm_final.reshape(1, d)

    groups = {
        "prompt": dict(x=x_prompt.reshape(bp * tp, d), batch=bp, seq=tp, rows_per_cond=bp * tp, rope=None,
                       dn_heads=DN_HEADS, ret_pairs=RET_HEADS // 2),
        "sample": dict(x=x_sample.reshape(bs * ts, d), batch=bs, seq=ts, rows_per_cond=ts, rope=rope_tabs,
                       dn_heads=1, ret_pairs=1),
    }
    new_k, new_v, new_rs, new_ds = [], [], [], []
    for l in range(depth):
        w1_in, w1_out = ffn1_w_in[l].astype(BF16), ffn1_w_out[l].astype(BF16)
        w2_in, w2_out = ffn2_w_in[l].astype(BF16), ffn2_w_out[l].astype(BF16)
        w_proj = _reorder_w_in(w_in[l])
        w_br = jnp.stack([w_br_attn[l], w_br_ret[l], w_br_dn[l]]).astype(BF16)
        w_o = w_out[l].astype(BF16)
        gq = jnp.tile(attn_q_norm[l], ATTN_HEADS).reshape(1, 512)
        gk = jnp.tile(attn_k_norm[l], ATTN_KV_HEADS).reshape(1, LANES)
        conv_w = dn_conv[l].reshape(3, 3, DN_HEADS, LANES).transpose(1, 2, 0, 3)
        alog = jnp.broadcast_to(dn_a_log[l][:, :, None, None], (2, DN_HEADS, 1, LANES))
        dtb = jnp.broadcast_to(dn_dt_bias[l][:, :, None, None], (2, DN_HEADS, 1, LANES))
        for name, grp in groups.items():
            is_prompt = name == "prompt"
            x = grp["x"]
            gmod = mod[l, :1] if is_prompt else mod[l, 1:]
            rpc = grp["rows_per_cond"]
            x = _ffn(x, gmod, norm_ffn1[l].reshape(1, d), w1_in, w1_out, fnw,
                     mod_base=0, rows_per_cond=rpc, final=False)
            attn, ret, dn, ab, mg = _inproj(x, gmod, norm_mix[l].reshape(1, d), w_proj, ones_bd, gq, gk,
                                            grp["rope"], rows_per_cond=rpc, seq_len=grp["seq"])
            if is_prompt:
                a_out = _attention(attn, None, None, batch=bp, seq_len=tp, tq=tp, n_tiles=1)
                rs0, ds0 = ret_zero, dn_zero
            else:
                a_out = _attention(attn, cache_k[:, l].reshape(bs, past, LANES),
                                   cache_v[:, l].reshape(bs, past, LANES), batch=bs, seq_len=ts,
                                   tq=128, n_tiles=2)
                rs0, ds0 = state_ret[:, l], state_delta[:, l]
            r_out, rs = _retention(ret, rs0, ret_norm[l].reshape(1, 512), ret_tabs, ones_bd,
                                   batch=grp["batch"], seq_len=grp["seq"], n_pairs=grp["ret_pairs"])
            d_out, ds = _deltanet(dn, ab, conv_w, alog, dtb, dn_norm[l].reshape(1, LANES), ds0,
                                  batch=grp["batch"], seq_len=grp["seq"], n_heads=grp["dn_heads"])
            x = _merge_ffn(x, gmod, a_out, r_out, d_out, mg, w_br, w_o, norm_ffn2[l].reshape(1, d),
                           w2_in, w2_out, fnw, rows_per_cond=rpc, final=(l == depth - 1))
            grp["x"] = x
            if is_prompt:
                new_k.append(attn[:, 512:640].reshape(bp, tp, ATTN_KV_HEADS, HEAD_DIM))
                new_v.append(attn[:, 640:768].reshape(bp, tp, ATTN_KV_HEADS, HEAD_DIM))
                new_rs.append(rs)
                new_ds.append(ds)

    y_prompt = groups["prompt"]["x"].reshape(bp, tp, d)
    y_sample = groups["sample"]["x"].reshape(bs, ts, d)
    return (y_prompt, y_sample, jnp.stack(new_k, axis=1), jnp.stack(new_v, axis=1),
            jnp.stack(new_rs, axis=1), jnp.stack(new_ds, axis=1))
```

```python
import functools
import math

import numpy as np
import jax
import jax.numpy as jnp
from jax import lax
from jax.experimental import pallas as pl
from jax.experimental.pallas import tpu as pltpu

F32 = jnp.float32
BF16 = jnp.bfloat16

EPS = 1e-6
ROPE_THETA = 10000.0
GRID_W = 64
N_MOD = 9

ATTN_HEADS = 8
ATTN_KV_HEADS = 2
HEAD_DIM = 64
RET_HEADS = 8
RET_DK = 64
RET_CHUNK = 128
RET_DECAY_EXP_FWD = 5.0
RET_DECAY_EXP_BWD = 5.5
DN_HEADS = 4
DN_DK = 128
DN_CHUNK = 64

LANES = 128
SUBLANES = 8
VMEM_LIMIT = 56 * 1024 * 1024

BRANCH_W = 512
KV_W = 128
ATTN_W = 768
RET_W = 2048
DN_W = 2048
AB_W = 128
MG_W = 3072


def _cparams(n_axes):
    return pltpu.CompilerParams(dimension_semantics=("parallel",) * n_axes,
                                vmem_limit_bytes=VMEM_LIMIT)


def _resident(shape):
    zeros = (0,) * len(shape)
    return pl.BlockSpec(shape, lambda *_: zeros, pipeline_mode=pl.Buffered(1))


def _dot(a, b):
    return jnp.dot(a.astype(BF16), b.astype(BF16), preferred_element_type=F32)


def _dot_nt(a, b):
    return lax.dot_general(a.astype(BF16), b.astype(BF16), (((1,), (1,)), ((), ())),
                           preferred_element_type=F32)


def _dot_tn(a, b):
    return lax.dot_general(a.astype(BF16), b.astype(BF16), (((0,), (0,)), ((), ())),
                           preferred_element_type=F32)


def _split_bf16(x):
    hi = x.astype(BF16)
    return hi, (x - hi.astype(F32)).astype(BF16)


def _sigmoid(x):
    return 1.0 / (1.0 + jnp.exp(-x))


def _silu(x):
    return x * _sigmoid(x)


def _norm_mod(x, nw, shift, scale):
    y = x * lax.rsqrt(jnp.mean(x * x, axis=-1, keepdims=True) + EPS) * nw
    return y * (1.0 + scale) + shift


def _mod_kernel(c_ref, w_ref, b_ref, o_ref):
    o_ref[0] = _dot(_silu(c_ref[...]), w_ref[0]) + b_ref[0]


def _modulation(conds, w_mod, b_mod):
    depth, d, n = w_mod.shape
    nc = conds.shape[0]
    tn = n // N_MOD
    return pl.pallas_call(
        _mod_kernel,
        grid=(depth, n // tn),
        in_specs=[pl.BlockSpec((nc, d), lambda l, j: (0, 0)),
                  pl.BlockSpec((1, d, tn), lambda l, j: (l, 0, j)),
                  pl.BlockSpec((1, 1, tn), lambda l, j: (l, 0, j))],
        out_specs=pl.BlockSpec((1, nc, tn), lambda l, j: (l, 0, j)),
        out_shape=jax.ShapeDtypeStruct((depth, nc, n), F32),
        compiler_params=_cparams(2),
        name="modulation",
    )(conds, w_mod, b_mod.reshape(depth, 1, n))


FFN_CHUNK = 256


def _ffn_rows(x, m, nw_ref, win_ref, wout_ref, fnw_ref, *, mod_base, dff, final):
    shift, scale, gate = (m[mod_base + i:mod_base + i + 1] for i in range(3))
    h = _norm_mod(x, nw_ref[...], shift, scale).astype(BF16)
    acc = jnp.zeros(x.shape, F32)
    for c in range(dff // FFN_CHUNK):
        lo = c * FFN_CHUNK
        hg = jnp.dot(h, win_ref[:, lo:lo + FFN_CHUNK], preferred_element_type=F32)
        hu = jnp.dot(h, win_ref[:, dff + lo:dff + lo + FFN_CHUNK], preferred_element_type=F32)
        a = (_silu(hg) * hu).astype(BF16)
        acc = acc + jnp.dot(a, wout_ref[lo:lo + FFN_CHUNK, :], preferred_element_type=F32)
    y = x + 0.5 * gate * acc
    if final:
        y = y * lax.rsqrt(jnp.mean(y * y, axis=-1, keepdims=True) + EPS) * fnw_ref[...]
    return y


def _ffn_kernel(x_ref, mod_ref, nw_ref, win_ref, wout_ref, fnw_ref, o_ref, *, mod_base, dff, final):
    o_ref[...] = _ffn_rows(x_ref[...], mod_ref[0], nw_ref, win_ref, wout_ref, fnw_ref,
                           mod_base=mod_base, dff=dff, final=final)


def _ffn(x, mod, nw, w_in, w_out, fnw, *, mod_base, rows_per_cond, final, tm=512):
    rows, d = x.shape
    dff = w_out.shape[0]
    tiles_per_cond = rows_per_cond // tm
    return pl.pallas_call(
        functools.partial(_ffn_kernel, mod_base=mod_base, dff=dff, final=final),
        grid=(rows // tm,),
        in_specs=[pl.BlockSpec((tm, d), lambda i: (i, 0)),
                  pl.BlockSpec((1, N_MOD, d), lambda i: (i // tiles_per_cond, 0, 0)),
                  _resident((1, d)),
                  _resident(w_in.shape),
                  _resident(w_out.shape),
                  _resident((1, d))],
        out_specs=pl.BlockSpec((tm, d), lambda i: (i, 0)),
        out_shape=jax.ShapeDtypeStruct((rows, d), F32),
        compiler_params=_cparams(1),
        name="ffn",
    )(x, mod, nw, w_in, w_out, fnw)


def _swap_halves(x):
    n = x.shape[-1]
    lane = lax.broadcasted_iota(jnp.int32, x.shape, 1)
    first = (lane % HEAD_DIM) < (HEAD_DIM // 2)
    return jnp.where(first, pltpu.roll(x, n - HEAD_DIM // 2, 1), pltpu.roll(x, HEAD_DIM // 2, 1))


def _rope(x, cos, sin):
    reps = x.shape[-1] // LANES
    c = jnp.concatenate([cos] * reps, axis=1) if reps > 1 else cos
    s = jnp.concatenate([sin] * reps, axis=1) if reps > 1 else sin
    return x * c + _swap_halves(x) * s


def _head_rms(x, ones_bd, gain):
    hi, lo = _split_bf16(x * x)
    ss = (jnp.dot(hi, ones_bd, preferred_element_type=F32)
          + jnp.dot(lo, ones_bd, preferred_element_type=F32))
    return x * lax.rsqrt(ss * (1.0 / HEAD_DIM) + EPS) * gain


def _inproj_kernel(*refs, rope):
    if rope:
        (x_ref, mod_ref, nw_ref, w_ref, ones_ref, gq_ref, gk_ref, cos_ref, sin_ref,
         attn_ref, ret_ref, dn_ref, ab_ref, mg_ref) = refs
    else:
        (x_ref, mod_ref, nw_ref, w_ref, ones_ref, gq_ref, gk_ref,
         attn_ref, ret_ref, dn_ref, ab_ref, mg_ref) = refs
    m = mod_ref[0]
    h = _norm_mod(x_ref[...], nw_ref[...], m[3:4], m[4:5]).astype(BF16)

    def proj(lo, width):
        return jnp.dot(h, w_ref[:, lo:lo + width], preferred_element_type=F32)

    def rot(v):
        return _rope(v, cos_ref[...], sin_ref[...]) if rope else v

    a = proj(0, ATTN_W)
    q = _head_rms(a[:, :BRANCH_W], ones_ref[...], gq_ref[...])
    k = _head_rms(a[:, BRANCH_W:BRANCH_W + KV_W], ones_ref[:LANES, :LANES], gk_ref[...])
    attn_ref[:, :BRANCH_W] = rot(q) * (HEAD_DIM ** -0.5)
    attn_ref[:, BRANCH_W:BRANCH_W + KV_W] = rot(k)
    attn_ref[:, BRANCH_W + KV_W:ATTN_W] = a[:, BRANCH_W + KV_W:ATTN_W]

    r = proj(ATTN_W, RET_W)
    ret_ref[:, :BRANCH_W] = rot(r[:, :BRANCH_W]) * (RET_DK ** -0.5)
    ret_ref[:, BRANCH_W:2 * BRANCH_W] = rot(r[:, BRANCH_W:2 * BRANCH_W])
    ret_ref[:, 1024:] = r[:, 1024:]

    dn_ref[...] = proj(ATTN_W + RET_W, DN_W)
    ab_ref[...] = proj(ATTN_W + RET_W + DN_W, AB_W)
    mg_ref[...] = proj(ATTN_W + RET_W + DN_W + AB_W, MG_W)


def _inproj(x, mod, nw, w, ones_bd, gq, gk, rope_tabs, *, rows_per_cond, seq_len, tm=256):
    rows, d = x.shape
    tiles_per_cond = rows_per_cond // tm
    tiles_per_seq = seq_len // tm
    rope = rope_tabs is not None
    in_specs = [pl.BlockSpec((tm, d), lambda i: (i, 0)),
                pl.BlockSpec((1, N_MOD, d), lambda i: (i // tiles_per_cond, 0, 0)),
                _resident((1, d)),
                _resident(w.shape),
                _resident(ones_bd.shape),
                _resident(gq.shape),
                _resident(gk.shape)]
    args = [x, mod, nw, w, ones_bd, gq, gk]
    if rope:
        in_specs += [pl.BlockSpec((tm, LANES), lambda i: (i % tiles_per_seq, 0))] * 2
        args += list(rope_tabs)
    widths = (ATTN_W, RET_W, DN_W, AB_W, MG_W)
    return pl.pallas_call(
        functools.partial(_inproj_kernel, rope=rope),
        grid=(rows // tm,),
        in_specs=in_specs,
        out_specs=[pl.BlockSpec((tm, wd), lambda i: (i, 0)) for wd in widths],
        out_shape=[jax.ShapeDtypeStruct((rows, wd), F32) for wd in widths],
        compiler_params=_cparams(1),
        name="inproj",
    )(*args)


ATTN_GROUP = ATTN_HEADS // ATTN_KV_HEADS
LOG2E = math.log2(math.e)
ATTN_KV_CHUNK = 512
ATTN_MAX_CHUNKS = 8
ATTN_EPILOGUE_DELAY = 2
ATTN_ROW_BLOCKS = 2


def _attn_kernel(*refs, tq, n_tiles, kv_chunk, n_chunks, has_ctx):
    if has_ctx:
        q_ref, k_ref, v_ref, ck_ref, cv_ref, o_ref, mx_ref, ls_ref, acc_ref, qs_ref = refs
    else:
        q_ref, k_ref, v_ref, o_ref, mx_ref, ls_ref, acc_ref, qs_ref = refs
    lane = lax.broadcasted_iota(jnp.int32, (tq, LANES), 1)
    low = lane < HEAD_DIM
    stacked = ATTN_HEADS * tq
    rbs = stacked // ATTN_ROW_BLOCKS

    def prologue(t):
        parts = []
        for hd in range(ATTN_HEADS):
            g = hd // ATTN_GROUP
            blk = q_ref[t * tq:(t + 1) * tq, (hd // 2) * LANES:(hd // 2 + 1) * LANES] * LOG2E
            if hd % 2 != g:
                blk = pltpu.roll(blk, HEAD_DIM, 1)
            parts.append(jnp.where(low if g == 0 else jnp.logical_not(low), blk, 0.0))
        rows = slice(t * stacked, (t + 1) * stacked)
        qs_ref[rows] = jnp.concatenate(parts, axis=0).astype(BF16)
        mx_ref[rows] = jnp.full((stacked, LANES), -jnp.inf, F32)
        ls_ref[rows] = jnp.zeros((stacked, LANES), F32)
        acc_ref[rows] = jnp.zeros((stacked, LANES), F32)

    def epilogue(t):
        rows = slice(t * stacked, (t + 1) * stacked)
        res = acc_ref[rows] / jnp.sum(ls_ref[rows], axis=-1, keepdims=True)
        for b in range(ATTN_HEADS // 2):
            g = (2 * b) // ATTN_GROUP
            even = res[2 * b * tq:(2 * b + 1) * tq]
            odd = res[(2 * b + 1) * tq:(2 * b + 2) * tq]
            if g == 0:
                odd = pltpu.roll(odd, HEAD_DIM, 1)
            else:
                even = pltpu.roll(even, HEAD_DIM, 1)
            o_ref[t * tq:(t + 1) * tq, b * LANES:(b + 1) * LANES] = jnp.where(low, even, odd)

    def lane_fold(op, acc, x):
        for b in range(x.shape[1] // LANES):
            acc = op(acc, x[:, b * LANES:(b + 1) * LANES])
        return acc

    def scores(sl, k):
        return _dot_nt(qs_ref[sl], k)

    def update(sl, s, v):
        reps = s.shape[1] // LANES
        m_old = mx_ref[sl]
        cmax = lane_fold(jnp.maximum, s[:, :LANES], s[:, LANES:])
        m_new = jnp.maximum(m_old, jnp.max(cmax, axis=-1, keepdims=True))
        alpha = jnp.exp2(m_old - m_new)
        p = jnp.exp2(s - jnp.concatenate([m_new] * reps, axis=1))
        mx_ref[sl] = m_new
        ls_ref[sl] = alpha * ls_ref[sl] + lane_fold(jnp.add, p[:, :LANES], p[:, LANES:])
        acc_ref[sl] = alpha * acc_ref[sl] + _dot(p, v)

    for t in range(n_tiles):
        prologue(t)
    kvs = [(k_ref[c * kv_chunk:(c + 1) * kv_chunk, :].astype(BF16),
            v_ref[c * kv_chunk:(c + 1) * kv_chunk, :].astype(BF16)) for c in range(n_chunks)]
    if has_ctx:
        kvs.append((ck_ref[0].astype(BF16), cv_ref[0].astype(BF16)))
    units = [(t, slice(t * stacked + r * rbs, t * stacked + (r + 1) * rbs), k, v)
             for t in range(n_tiles) for k, v in kvs for r in range(ATTN_ROW_BLOCKS)]
    last_of = {t: max(i for i, u in enumerate(units) if u[0] == t) for t in range(n_tiles)}
    ahead = None
    due = []
    for idx in range(len(units) + 1):
        nxt = None
        if idx < len(units):
            t, sl, k, v = units[idx]
            nxt = (idx, sl, scores(sl, k), v)
        if ahead is not None:
            done, sl, s, v = ahead
            update(sl, s, v)
            for t in range(n_tiles):
                if last_of[t] == done:
                    due.append((done + ATTN_EPILOGUE_DELAY, t))
        for pos, t in list(due):
            if pos <= idx or nxt is None:
                epilogue(t)
                due.remove((pos, t))
        ahead = nxt


def _attention(attn, ctx_k, ctx_v, *, batch, seq_len, tq, n_tiles):
    rows = attn.shape[0]
    kv_chunk = min(seq_len, ATTN_KV_CHUNK)
    n_chunks = seq_len // kv_chunk
    assert n_chunks <= ATTN_MAX_CHUNKS, "the key loop is fully unrolled"
    step = tq * n_tiles
    qt = seq_len // step
    stacked = ATTN_HEADS * step
    has_ctx = ctx_k is not None
    in_specs = [pl.BlockSpec((step, BRANCH_W), lambda b, i: (b * qt + i, 0)),
                pl.BlockSpec((seq_len, LANES), lambda b, i: (b, 4)),
                pl.BlockSpec((seq_len, LANES), lambda b, i: (b, 5))]
    args = [attn, attn, attn]
    if has_ctx:
        past = ctx_k.shape[1]
        in_specs += [pl.BlockSpec((1, past, LANES), lambda b, i: (b, 0, 0))] * 2
        args += [ctx_k, ctx_v]
    scratch = [pltpu.VMEM((stacked, LANES), F32)] * 3 + [pltpu.VMEM((stacked, LANES), BF16)]
    return pl.pallas_call(
        functools.partial(_attn_kernel, tq=tq, n_tiles=n_tiles, kv_chunk=kv_chunk, n_chunks=n_chunks,
                          has_ctx=has_ctx),
        grid=(batch, qt),
        in_specs=in_specs,
        out_specs=pl.BlockSpec((step, BRANCH_W), lambda b, i: (b * qt + i, 0)),
        out_shape=jax.ShapeDtypeStruct((rows, BRANCH_W), F32),
        scratch_shapes=scratch,
        compiler_params=_cparams(2),
        name="attention",
    )(*args)


def _retention_tables():
    C = RET_CHUNK
    h = np.arange(RET_HEADS, dtype=np.float64)
    pos = np.arange(C, dtype=np.float64)
    diff = pos[:, None] - pos[None, :]
    inner, qd, kd, cd = [], [], [], []
    for direction, expo in enumerate((RET_DECAY_EXP_FWD, RET_DECAY_EXP_BWD)):
        lg = np.log1p(-np.exp2(-expo - h))[:, None, None]
        if direction == 0:
            inner.append(np.where(diff >= 0, np.exp(lg * np.maximum(diff, 0.0)), 0.0))
            qd.append(np.exp(lg[:, :, 0] * (pos + 1.0)))
            kd.append(np.exp(lg[:, :, 0] * (C - 1.0 - pos)))
        else:
            inner.append(np.where(diff <= 0, np.exp(lg * np.maximum(-diff, 0.0)), 0.0))
            qd.append(np.exp(lg[:, :, 0] * (C - pos)))
            kd.append(np.exp(lg[:, :, 0] * pos))
        cd.append(np.exp(lg[:, 0, 0] * C))
    inner = np.stack(inner, axis=1)
    rowdec = np.stack([np.stack(qd, 1), np.stack(kd, 1)], axis=2)
    rowdec = np.repeat(rowdec[..., None], RET_DK, axis=-1)
    rowdec = np.concatenate([rowdec[0::2], rowdec[1::2]], axis=-1)
    cd = np.stack(cd, axis=1)
    block = np.kron(np.eye(2), np.ones((RET_DK, RET_DK)))
    per_row = np.repeat(np.stack([cd[0::2], cd[1::2]], axis=-1), RET_DK, axis=-1)
    cd = per_row[:, :, :, None] * block
    return (jnp.asarray(inner, F32), jnp.asarray(rowdec, F32), jnp.asarray(cd, F32))


RET_BLOCK = 16


def _ret_kernel(q_ref, k_ref, v_ref, g_ref, inner_ref, dec_ref, cd_ref, ones_ref, s0_ref, nw_ref,
                o_ref, st_ref, *, n_chunks, n_pairs):
    C = RET_CHUNK
    D = RET_DK
    nb = min(RET_BLOCK, n_chunks)
    lanes_of = [slice(p * LANES, (p + 1) * LANES) for p in range(n_pairs)]
    lane = lax.broadcasted_iota(jnp.int32, (C, LANES), 1)
    first = lane < D
    ones = ones_ref[...]

    def head_mean(x):
        hi, lo = _split_bf16(x)
        return (jnp.dot(hi, ones, preferred_element_type=F32)
                + jnp.dot(lo, ones, preferred_element_type=F32)) * (1.0 / D)

    def sweep(direction):
        def body(i, states):
            order = [i * nb + j for j in range(nb)]
            if direction == 1:
                order = [n_chunks - 1 - c for c in order]
            units = [(p, pl.multiple_of(c * C, C)) for p in range(n_pairs) for c in order]
            n = len(units)
            q = [q_ref[pl.ds(r, C), lanes_of[p]] for p, r in units]
            k = [k_ref[pl.ds(r, C), lanes_of[p]] for p, r in units]
            v = [v_ref[pl.ds(r, C), lanes_of[p]] for p, r in units]
            if direction == 1:
                prev = [o_ref[pl.ds(r, C), lanes_of[p]] for p, r in units]
                gate = [g_ref[pl.ds(r, C), lanes_of[p]] for p, r in units]
            att = [jnp.concatenate(
                [_dot_nt(jnp.where(first, q[e], 0.0), k[e]) * inner_ref[2 * units[e][0], direction],
                 _dot_nt(jnp.where(first, 0.0, q[e]), k[e]) * inner_ref[2 * units[e][0] + 1, direction]],
                axis=1) for e in range(n)]
            kv = [_dot_tn(k[e] * dec_ref[units[e][0], direction, 1], v[e]) for e in range(n)]
            seen = []
            states = list(states)
            for e, (p, _) in enumerate(units):
                cd = cd_ref[p, direction]
                seen.append(states[p])
                states[p] = states[p] * cd + jnp.where(cd != 0.0, kv[e], 0.0)
            v2 = [jnp.concatenate([jnp.where(first, v[e], 0.0), jnp.where(first, 0.0, v[e])], axis=0)
                  for e in range(n)]
            o = [_dot(att[e], v2[e]) + _dot(q[e] * dec_ref[units[e][0], direction, 0], seen[e])
                 for e in range(n)]
            if direction == 1:
                o = [o[e] + prev[e] for e in range(n)]
                mean = [head_mean(o[e]) for e in range(n)]
                d = [o[e] - mean[e] for e in range(n)]
                var = [head_mean(d[e] * d[e]) for e in range(n)]
                o = [d[e] * lax.rsqrt(var[e] + EPS) * nw_ref[:, lanes_of[units[e][0]]] * _silu(gate[e])
                     for e in range(n)]
            for e, (p, r) in enumerate(units):
                o_ref[pl.ds(r, C), lanes_of[p]] = o[e]
            return tuple(states)

        zero = jnp.zeros((D, D), F32)
        states = tuple(
            jnp.concatenate([jnp.concatenate([s0_ref[0, direction, 2 * p], zero], axis=1),
                             jnp.concatenate([zero, s0_ref[0, direction, 2 * p + 1]], axis=1)], axis=0)
            for p in range(n_pairs))
        states = lax.fori_loop(0, n_chunks // nb, body, states)
        for p in range(n_pairs):
            st_ref[0, direction, 2 * p] = states[p][:D, :D]
            st_ref[0, direction, 2 * p + 1] = states[p][D:, D:]

    sweep(0)
    sweep(1)


def _retention(ret, s0, nw, tables, ones_bd, *, batch, seq_len, n_pairs):
    rows = ret.shape[0]
    inner, rowdec, cd = tables
    C = RET_CHUNK
    U = n_pairs
    groups = RET_HEADS // 2 // U
    col = lambda j: pl.BlockSpec((seq_len, U * LANES), lambda b, hp, j=j: (b, groups * j + hp))
    st_spec = pl.BlockSpec((1, 2, 2 * U, RET_DK, RET_DK), lambda b, hp: (b, 0, hp, 0, 0))
    return pl.pallas_call(
        functools.partial(_ret_kernel, n_chunks=seq_len // C, n_pairs=U),
        grid=(batch, groups),
        in_specs=[col(0), col(1), col(2), col(3),
                  pl.BlockSpec((2 * U, 2, C, C), lambda b, hp: (hp, 0, 0, 0)),
                  pl.BlockSpec((U, 2, 2, C, LANES), lambda b, hp: (hp, 0, 0, 0, 0)),
                  pl.BlockSpec((U, 2, LANES, LANES), lambda b, hp: (hp, 0, 0, 0)),
                  pl.BlockSpec((LANES, LANES), lambda b, hp: (0, 0)),
                  st_spec,
                  pl.BlockSpec((1, U * LANES), lambda b, hp: (0, hp))],
        out_specs=[pl.BlockSpec((seq_len, U * LANES), lambda b, hp: (b, hp)), st_spec],
        out_shape=[jax.ShapeDtypeStruct((rows, BRANCH_W), F32),
                   jax.ShapeDtypeStruct((batch, 2, RET_HEADS, RET_DK, RET_DK), F32)],
        compiler_params=_cparams(2),
        name="retention",
    )(ret, ret, ret, ret, inner, rowdec, cd, ones_bd, s0, nw)


DN_BLOCK = 8
SOLVE_BASE = 8
CONV_BLOCK = 256
FIN_BLOCK = 1024


def _dn_kernel(q_ref, k_ref, v_ref, g_ref, ab_ref, cw_ref, alog_ref, dtb_ref, nw_ref, s0_ref,
               o_ref, st_ref, qs_ref, ks_ref, vs_ref, of_ref, ob_ref,
               wp_ref, bm_ref, qp_ref, op_ref, dec_ref, *, seq_len, n_heads):
    C = DN_CHUNK
    T = seq_len
    U = n_heads
    n_chunks = T // C
    head0 = pl.program_id(1) * U
    lanes_of = [slice(hh * LANES, (hh + 1) * LANES) for hh in range(U)]

    blk = min(CONV_BLOCK, T)
    row = lax.broadcasted_iota(jnp.int32, (blk, LANES), 0)
    tensors = [(src, dst, t, hh) for hh in range(U)
               for t, (src, dst) in enumerate(((q_ref, qs_ref), (k_ref, ks_ref), (v_ref, vs_ref)))]

    def conv(i, carry):
        r0 = pl.multiple_of(i * blk, blk)
        above = pl.multiple_of(jnp.maximum(r0 - SUBLANES, 0), SUBLANES)
        below = pl.multiple_of(jnp.minimum(r0 + blk, T - SUBLANES), SUBLANES)
        x = [src[pl.ds(r0, blk), lanes_of[hh]] for src, _, _, hh in tensors]
        up = [jnp.where(r0 > 0, src[pl.ds(above, SUBLANES), lanes_of[hh]], 0.0) for src, _, _, hh in tensors]
        dn = [jnp.where(r0 + blk < T, src[pl.ds(below, SUBLANES), lanes_of[hh]], 0.0)
              for src, _, _, hh in tensors]
        n = len(tensors)
        padded = [jnp.concatenate([up[e], x[e], dn[e]], axis=0) for e in range(n)]
        prev = [padded[e][SUBLANES - 1:SUBLANES - 1 + blk] for e in range(n)]
        nxt = [padded[e][SUBLANES + 1:SUBLANES + 1 + blk] for e in range(n)]
        w = [cw_ref[t, hh] for _, _, t, hh in tensors]
        y = [_silu(w[e][0:1] * prev[e] + w[e][1:2] * x[e] + w[e][2:3] * nxt[e]) for e in range(n)]
        for e, (_, dst, t, hh) in enumerate(tensors):
            if t < 2:
                scale = lax.rsqrt(jnp.sum(y[e] * y[e], axis=-1, keepdims=True) + EPS)
                y[e] = y[e] * (scale * (DN_DK ** -0.5) if t == 0 else scale)
        for e, (_, dst, t, hh) in enumerate(tensors):
            dst[pl.ds(r0, blk), lanes_of[hh]] = y[e]
        return carry

    lax.fori_loop(0, T // blk, conv, 0)

    ri = lax.broadcasted_iota(jnp.int32, (C, C), 0)
    ci = lax.broadcasted_iota(jnp.int32, (C, C), 1)
    eye = ri == ci
    eye_f = eye.astype(F32)
    lane = lax.broadcasted_iota(jnp.int32, (1, LANES), 1)
    masks = ((ri >= ci, ri > ci), (ri <= ci, ri < ci))
    incl_bf = tuple(m[0].astype(F32).astype(BF16) for m in masks)
    base_blocks = (ri // SOLVE_BASE) == (ci // SOLVE_BASE)
    level_masks = []
    size = SOLVE_BASE
    while size < C:
        level_masks.append(jnp.logical_and((ri // (2 * size)) == (ci // (2 * size)),
                                           (ri // size) != (ci // size)))
        size *= 2

    def load(r0, hh):
        return (ab_ref[pl.ds(r0, C), :],) + tuple(ref[pl.ds(r0, C), lanes_of[hh]]
                                                   for ref in (qs_ref, ks_ref, vs_ref))

    def prep(operands, direction, hh):
        ab, q, k, v = operands
        incl, strict = masks[direction]
        sel_a = (lane == direction * DN_HEADS + head0 + hh).astype(F32)
        sel_b = (lane == 2 * DN_HEADS + direction * DN_HEADS + head0 + hh).astype(F32)
        da = jnp.sum(ab * sel_a, axis=-1, keepdims=True)
        db = jnp.sum(ab * sel_b, axis=-1, keepdims=True)
        z = da + dtb_ref[direction, hh]
        softplus = jnp.maximum(z, 0.0) + jnp.log1p(jnp.exp(-jnp.abs(z)))
        g = -jnp.exp(alog_ref[direction, hh]) * softplus
        beta = _sigmoid(db)
        kb = k * beta
        g1 = g.astype(BF16)
        g2 = (g - g1.astype(F32)).astype(BF16)
        g3 = (g - g1.astype(F32) - g2.astype(F32)).astype(BF16)
        G12 = _dot(incl_bf[direction], jnp.concatenate([g1, g2], axis=1))
        G = G12[:, :LANES] + G12[:, LANES:] + _dot(incl_bf[direction], g3)
        kk = _dot_nt(kb, k)
        qk = _dot_nt(q, k)
        yield
        Gc = G[:, :C]
        Grow = jnp.sum(jnp.where(eye, Gc, 0.0), axis=0, keepdims=True)
        L = jnp.where(incl, jnp.exp(jnp.where(incl, Gc - Grow, 0.0)), 0.0)
        N = jnp.where(strict, -(kk * L), 0.0)
        P = jnp.where(base_blocks, N, 0.0)
        Tm = eye_f + P
        P = _dot(P, P)
        yield
        for _ in range(int(math.log2(SOLVE_BASE)) - 2):
            Tm, P = Tm + _dot(Tm, P), _dot(P, P)
            yield
        Tm = Tm + _dot(Tm, P)
        yield
        for off_blocks in level_masks:
            TX = _dot(Tm, jnp.where(off_blocks, N, 0.0))
            yield
            Tm = Tm + _dot(TX, Tm)
            yield
        eG = jnp.exp(G)
        g_last = G[C - 1:C] if direction == 0 else G[0:1]
        wu = _dot(Tm, jnp.concatenate([kb * eG, v * beta], axis=1))
        yield
        kd = k * jnp.exp(g_last - G)
        att = qk * L
        kd_wu = _dot_tn(kd, wu)
        att_wu = _dot(att, wu)
        return ((-kd_wu[:, :LANES]).astype(BF16), kd_wu[:, LANES:],
                (q * eG - att_wu[:, :LANES]).astype(BF16), att_wu[:, LANES:],
                jnp.broadcast_to(jnp.exp(g_last), (SUBLANES, LANES)))

    def run_staged(generators):
        results = [None] * len(generators)
        live = list(enumerate(generators))
        while live:
            still = []
            for idx, gen in live:
                try:
                    next(gen)
                    still.append((idx, gen))
                except StopIteration as done:
                    results[idx] = done.value
            live = still
        return results

    slots = (wp_ref, bm_ref, qp_ref, op_ref, dec_ref)
    nb = min(DN_BLOCK, n_chunks)
    n_blocks = n_chunks // nb
    per_half = 2 * nb * U

    def block_rows(i):
        return ([pl.multiple_of((i * nb + j) * C, C) for j in range(nb)]
                + [pl.multiple_of((n_chunks - 1 - (i * nb + j)) * C, C) for j in range(nb)])

    def prep_block(i):
        rows = block_rows(i)
        return [prep(load(r, hh), s // nb, hh) for hh in range(U) for s, r in enumerate(rows)]

    def store_block(prepared, base):
        for e in range(per_half):
            for ref, val in zip(slots, prepared[e]):
                ref[base + e] = val

    def recurrence(states, base):
        states = list(states)
        outs = []
        for j in range(nb):
            step_out = []
            for hh in range(U):
                for direction in range(2):
                    e = base + hh * 2 * nb + direction * nb + j
                    S = states[2 * hh + direction]
                    S16 = S.astype(BF16)
                    step_out.append(_dot(qp_ref[e], S16) + op_ref[e])
                    states[2 * hh + direction] = S * dec_ref[e, 0:1] + _dot(wp_ref[e], S16) + bm_ref[e]
            outs.append(step_out)
            yield
        return outs, tuple(states)

    def store_outputs(i, outs):
        rows = block_rows(i)
        for j in range(nb):
            for hh in range(U):
                of_ref[pl.ds(rows[j], C), lanes_of[hh]] = outs[j][2 * hh]
                ob_ref[pl.ds(rows[nb + j], C), lanes_of[hh]] = outs[j][2 * hh + 1]

    state = tuple(s0_ref[0, direction, hh] for hh in range(U) for direction in range(2))
    if n_blocks == 1:
        store_block(run_staged(prep_block(0)), 0)
        (outs, state), = run_staged([recurrence(state, 0)])
        store_outputs(0, outs)
    else:
        store_block(run_staged(prep_block(0)), 0)

        def body(i, carry):
            cur = (i % 2) * per_half
            nxt = per_half - cur
            *prepared, (outs, carry) = run_staged(prep_block(i + 1) + [recurrence(carry, cur)])
            store_outputs(i, outs)
            store_block(prepared, nxt)
            return carry

        state = lax.fori_loop(0, n_blocks - 1, body, state)
        (outs, state), = run_staged([recurrence(state, ((n_blocks - 1) % 2) * per_half)])
        store_outputs(n_blocks - 1, outs)
    for hh in range(U):
        for direction in range(2):
            st_ref[0, direction, hh] = state[2 * hh + direction]

    fblk = min(FIN_BLOCK, T)

    def fin(i, carry):
        r0 = pl.multiple_of(i * fblk, fblk)
        o = [of_ref[pl.ds(r0, fblk), lanes_of[hh]] + ob_ref[pl.ds(r0, fblk), lanes_of[hh]] for hh in range(U)]
        ms = [jnp.mean(o[hh] * o[hh], axis=-1, keepdims=True) for hh in range(U)]
        for hh in range(U):
            y = o[hh] * lax.rsqrt(ms[hh] + EPS) * nw_ref[...]
            o_ref[pl.ds(r0, fblk), lanes_of[hh]] = y * _silu(g_ref[pl.ds(r0, fblk), lanes_of[hh]])
        return carry

    lax.fori_loop(0, T // fblk, fin, 0)


def _deltanet(dn, ab, conv_w, alog, dtb, nw, s0, *, batch, seq_len, n_heads):
    rows = dn.shape[0]
    T = seq_len
    C = DN_CHUNK
    U = n_heads
    nb = min(DN_BLOCK, T // C)
    n_slots = 4 * nb * U
    groups = DN_HEADS // U
    col = lambda j: pl.BlockSpec((T, U * LANES), lambda b, h, j=j: (b, groups * j + h))
    st_spec = pl.BlockSpec((1, 2, U, DN_DK, DN_DK), lambda b, h: (b, 0, h, 0, 0))
    gate_spec = pl.BlockSpec((2, U, 1, LANES), lambda b, h: (0, h, 0, 0))
    return pl.pallas_call(
        functools.partial(_dn_kernel, seq_len=T, n_heads=U),
        grid=(batch, groups),
        in_specs=[col(0), col(1), col(2), col(3),
                  pl.BlockSpec((T, LANES), lambda b, h: (b, 0)),
                  pl.BlockSpec((3, U, 3, LANES), lambda b, h: (0, h, 0, 0)),
                  gate_spec, gate_spec,
                  pl.BlockSpec((1, LANES), lambda b, h: (0, 0)),
                  st_spec],
        out_specs=[pl.BlockSpec((T, U * LANES), lambda b, h: (b, h)), st_spec],
        out_shape=[jax.ShapeDtypeStruct((rows, BRANCH_W), F32),
                   jax.ShapeDtypeStruct((batch, 2, DN_HEADS, DN_DK, DN_DK), F32)],
        scratch_shapes=([pltpu.VMEM((T, U * LANES), F32)] * 5
                        + [pltpu.VMEM((n_slots, DN_DK, LANES), BF16), pltpu.VMEM((n_slots, DN_DK, LANES), F32),
                           pltpu.VMEM((n_slots, C, LANES), BF16), pltpu.VMEM((n_slots, C, LANES), F32),
                           pltpu.VMEM((n_slots, SUBLANES, LANES), F32)]),
        compiler_params=_cparams(2),
        name="deltanet",
    )(dn, dn, dn, dn, ab, conv_w, alog, dtb, nw, s0)


def _merge_ffn_kernel(x_ref, mod_ref, a_ref, r_ref, d_ref, mg_ref, wbr_ref, wo_ref,
                      nw_ref, win_ref, wout_ref, fnw_ref, o_ref, *, dff, final):
    d = x_ref.shape[-1]
    m = mod_ref[0]
    merged = jnp.zeros(x_ref.shape, F32)
    for i, br in enumerate((a_ref, r_ref, d_ref)):
        merged = merged + _sigmoid(mg_ref[:, i * d:(i + 1) * d]) * _dot(br[...], wbr_ref[i])
    x = x_ref[...] + m[5:6] * _dot(merged, wo_ref[...])
    o_ref[...] = _ffn_rows(x, m, nw_ref, win_ref, wout_ref, fnw_ref, mod_base=6, dff=dff, final=final)


def _merge_ffn(x, mod, a, r, dn, mg, w_br, w_o, nw, w_in, w_out, fnw, *, rows_per_cond, final, tm=512):
    rows, d = x.shape
    dff = w_out.shape[0]
    tiles_per_cond = rows_per_cond // tm
    row_spec = lambda wd: pl.BlockSpec((tm, wd), lambda i: (i, 0))
    return pl.pallas_call(
        functools.partial(_merge_ffn_kernel, dff=dff, final=final),
        grid=(rows // tm,),
        in_specs=[row_spec(d),
                  pl.BlockSpec((1, N_MOD, d), lambda i: (i // tiles_per_cond, 0, 0)),
                  row_spec(BRANCH_W), row_spec(BRANCH_W), row_spec(BRANCH_W), row_spec(MG_W),
                  _resident(w_br.shape), _resident(w_o.shape),
                  _resident((1, d)), _resident(w_in.shape), _resident(w_out.shape), _resident((1, d))],
        out_specs=row_spec(d),
        out_shape=jax.ShapeDtypeStruct((rows, d), F32),
        compiler_params=_cparams(1),
        name="merge_ffn",
    )(x, mod, a, r, dn, mg, w_br, w_o, nw, w_in, w_out, fnw)


def _rope_tables(seq_len):
    n_freq = HEAD_DIM // 4
    inv = ROPE_THETA ** (-np.arange(n_freq, dtype=np.float64) / n_freq)
    t = np.arange(seq_len)
    row = (t // GRID_W).astype(np.float64)
    colp = (t % GRID_W).astype(np.float64)
    ang = np.concatenate([row[:, None] * inv, colp[:, None] * inv], axis=-1)
    c, s = np.cos(ang), np.sin(ang)
    cos = np.concatenate([c, c, c, c], axis=-1)
    sin = np.concatenate([-s, s, -s, s], axis=-1)
    return jnp.asarray(cos, F32), jnp.asarray(sin, F32)


def _reorder_w_in(w):
    d = w.shape[0]
    o_da = ATTN_W + RET_W + 3 * BRANCH_W
    o_dg = o_da + 4 * DN_HEADS
    o_mg = o_dg + BRANCH_W
    parts = [w[:, :o_da], w[:, o_dg:o_mg], w[:, o_da:o_dg],
             jnp.zeros((d, AB_W - 4 * DN_HEADS), w.dtype), w[:, o_mg:]]
    return jnp.concatenate([p.astype(BF16) for p in parts], axis=1)


def kernel(x_prompt, x_sample, cache_k, cache_v, state_ret, state_delta, c, c_ctx,
           w_mod, b_mod, norm_ffn1, ffn1_w_in, ffn1_w_out, norm_mix, w_in,
           attn_q_norm, attn_k_norm, ret_norm, dn_conv, dn_a_log, dn_dt_bias, dn_norm,
           w_br_attn, w_br_ret, w_br_dn, w_out, norm_ffn2, ffn2_w_in, ffn2_w_out, norm_final):
    bp, tp, d = x_prompt.shape
    bs, ts, _ = x_sample.shape
    depth = w_mod.shape[0]
    past = cache_k.shape[2]

    conds = jnp.concatenate([c_ctx[None, :], c], axis=0)
    mod = _modulation(conds, w_mod, b_mod).reshape(depth, 1 + bs, N_MOD, d)

    ones_bd = jnp.asarray(np.kron(np.eye(ATTN_HEADS), np.ones((HEAD_DIM, HEAD_DIM))), BF16)
    ret_tabs = _retention_tables()
    rope_tabs = _rope_tables(ts)
    ret_zero = jnp.zeros((bp, 2, RET_HEADS, RET_DK, RET_DK), F32)
    dn_zero = jnp.zeros((bp, 2, DN_HEADS, DN_DK, DN_DK), F32)
    fnw = norm_final.reshape(1, d)

    groups = {
        "prompt": dict(x=x_prompt.reshape(bp * tp, d), batch=bp, seq=tp, rows_per_cond=bp * tp, rope=None,
                       dn_heads=DN_HEADS, ret_pairs=RET_HEADS // 2),
        "sample": dict(x=x_sample.reshape(bs * ts, d), batch=bs, seq=ts, rows_per_cond=ts, rope=rope_tabs,
                       dn_heads=1, ret_pairs=1),
    }
    new_k, new_v, new_rs, new_ds = [], [], [], []
    for l in range(depth):
        w1_in, w1_out = ffn1_w_in[l].astype(BF16), ffn1_w_out[l].astype(BF16)
        w2_in, w2_out = ffn2_w_in[l].astype(BF16), ffn2_w_out[l].astype(BF16)
        w_proj = _reorder_w_in(w_in[l])
        w_br = jnp.stack([w_br_attn[l], w_br_ret[l], w_br_dn[l]]).astype(BF16)
        w_o = w_out[l].astype(BF16)
        gq = jnp.tile(attn_q_norm[l], ATTN_HEADS).reshape(1, BRANCH_W)
        gk = jnp.tile(attn_k_norm[l], ATTN_KV_HEADS).reshape(1, LANES)
        conv_w = dn_conv[l].reshape(3, 3, DN_HEADS, LANES).transpose(1, 2, 0, 3)
        alog = jnp.broadcast_to(dn_a_log[l][:, :, None, None], (2, DN_HEADS, 1, LANES))
        dtb = jnp.broadcast_to(dn_dt_bias[l][:, :, None, None], (2, DN_HEADS, 1, LANES))
        for name, grp in groups.items():
            is_prompt = name == "prompt"
            x = grp["x"]
            gmod = mod[l, :1] if is_prompt else mod[l, 1:]
            rpc = grp["rows_per_cond"]
            x = _ffn(x, gmod, norm_ffn1[l].reshape(1, d), w1_in, w1_out, fnw,
                     mod_base=0, rows_per_cond=rpc, final=False)
            attn, ret, dn, ab, mg = _inproj(x, gmod, norm_mix[l].reshape(1, d), w_proj, ones_bd, gq, gk,
                                            grp["rope"], rows_per_cond=rpc, seq_len=grp["seq"])
            if is_prompt:
                a_out = _attention(attn, None, None, batch=bp, seq_len=tp, tq=tp, n_tiles=1)
                rs0, ds0 = ret_zero, dn_zero
            else:
                a_out = _attention(attn, cache_k[:, l].reshape(bs, past, LANES),
                                   cache_v[:, l].reshape(bs, past, LANES), batch=bs, seq_len=ts,
                                   tq=128, n_tiles=2)
                rs0, ds0 = state_ret[:, l], state_delta[:, l]
            r_out, rs = _retention(ret, rs0, ret_norm[l].reshape(1, BRANCH_W), ret_tabs, ones_bd,
                                   batch=grp["batch"], seq_len=grp["seq"], n_pairs=grp["ret_pairs"])
            d_out, ds = _deltanet(dn, ab, conv_w, alog, dtb, dn_norm[l].reshape(1, LANES), ds0,
                                  batch=grp["batch"], seq_len=grp["seq"], n_heads=grp["dn_heads"])
            x = _merge_ffn(x, gmod, a_out, r_out, d_out, mg, w_br, w_o, norm_ffn2[l].reshape(1, d),
                           w2_in, w2_out, fnw, rows_per_cond=rpc, final=(l == depth - 1))
            grp["x"] = x
            if is_prompt:
                new_k.append(attn[:, BRANCH_W:BRANCH_W + KV_W].reshape(bp, tp, ATTN_KV_HEADS, HEAD_DIM))
                new_v.append(attn[:, BRANCH_W + KV_W:ATTN_W].reshape(bp, tp, ATTN_KV_HEADS, HEAD_DIM))
                new_rs.append(rs)
                new_ds.append(ds)

    y_prompt = groups["prompt"]["x"].reshape(bp, tp, d)
    y_sample = groups["sample"]["x"].reshape(bs, ts, d)
    return (y_prompt, y_sample, jnp.stack(new_k, axis=1), jnp.stack(new_v, axis=1),
            jnp.stack(new_rs, axis=1), jnp.stack(new_ds, axis=1))
```

```python
import functools
import math

import numpy as np
import jax
import jax.numpy as jnp
from jax import lax
from jax.experimental import pallas as pl
from jax.experimental.pallas import tpu as pltpu

F32 = jnp.float32
BF16 = jnp.bfloat16

EPS = 1e-6
ROPE_THETA = 10000.0
GRID_W = 64
N_MOD = 9

ATTN_HEADS = 8
ATTN_KV_HEADS = 2
HEAD_DIM = 64
RET_HEADS = 8
RET_DK = 64
RET_CHUNK = 128
RET_DECAY_EXP_FWD = 5.0
RET_DECAY_EXP_BWD = 5.5
DN_HEADS = 4
DN_DK = 128
DN_CHUNK = 64

LANES = 128
SUBLANES = 8
VMEM_LIMIT = 56 * 1024 * 1024

BRANCH_W = 512
KV_W = 128
ATTN_W = 768
RET_W = 2048
DN_W = 2048
AB_W = 128
MG_W = 3072


def _cparams(n_axes):
    return pltpu.CompilerParams(dimension_semantics=("parallel",) * n_axes,
                                vmem_limit_bytes=VMEM_LIMIT)


def _resident(shape):
    zeros = (0,) * len(shape)
    return pl.BlockSpec(shape, lambda *_: zeros, pipeline_mode=pl.Buffered(1))


def _dot(a, b):
    return jnp.dot(a.astype(BF16), b.astype(BF16), preferred_element_type=F32)


def _dot_nt(a, b):
    return lax.dot_general(a.astype(BF16), b.astype(BF16), (((1,), (1,)), ((), ())),
                           preferred_element_type=F32)


def _dot_tn(a, b):
    return lax.dot_general(a.astype(BF16), b.astype(BF16), (((0,), (0,)), ((), ())),
                           preferred_element_type=F32)


def _split_bf16(x):
    hi = x.astype(BF16)
    return hi, (x - hi.astype(F32)).astype(BF16)


def _sigmoid(x):
    return 1.0 / (1.0 + jnp.exp(-x))


def _silu(x):
    return x * _sigmoid(x)


def _norm_mod(x, nw, shift, scale):
    y = x * lax.rsqrt(jnp.mean(x * x, axis=-1, keepdims=True) + EPS) * nw
    return y * (1.0 + scale) + shift


def _mod_kernel(c_ref, w_ref, b_ref, o_ref):
    o_ref[0] = _dot(_silu(c_ref[...]), w_ref[0]) + b_ref[0]


def _modulation(conds, w_mod, b_mod):
    depth, d, n = w_mod.shape
    nc = conds.shape[0]
    tn = n // N_MOD
    return pl.pallas_call(
        _mod_kernel,
        grid=(depth, n // tn),
        in_specs=[pl.BlockSpec((nc, d), lambda l, j: (0, 0)),
                  pl.BlockSpec((1, d, tn), lambda l, j: (l, 0, j)),
                  pl.BlockSpec((1, 1, tn), lambda l, j: (l, 0, j))],
        out_specs=pl.BlockSpec((1, nc, tn), lambda l, j: (l, 0, j)),
        out_shape=jax.ShapeDtypeStruct((depth, nc, n), F32),
        compiler_params=_cparams(2),
        name="modulation",
    )(conds, w_mod, b_mod.reshape(depth, 1, n))


FFN_CHUNK = 256


def _ffn_rows(x, m, nw_ref, win_ref, wout_ref, fnw_ref, *, mod_base, dff, final):
    shift, scale, gate = (m[mod_base + i:mod_base + i + 1] for i in range(3))
    h = _norm_mod(x, nw_ref[...], shift, scale).astype(BF16)
    acc = jnp.zeros(x.shape, F32)
    for c in range(dff // FFN_CHUNK):
        lo = c * FFN_CHUNK
        hg = jnp.dot(h, win_ref[:, lo:lo + FFN_CHUNK], preferred_element_type=F32)
        hu = jnp.dot(h, win_ref[:, dff + lo:dff + lo + FFN_CHUNK], preferred_element_type=F32)
        a = (_silu(hg) * hu).astype(BF16)
        acc = acc + jnp.dot(a, wout_ref[lo:lo + FFN_CHUNK, :], preferred_element_type=F32)
    y = x + 0.5 * gate * acc
    if final:
        y = y * lax.rsqrt(jnp.mean(y * y, axis=-1, keepdims=True) + EPS) * fnw_ref[...]
    return y


def _ffn_kernel(x_ref, mod_ref, nw_ref, win_ref, wout_ref, fnw_ref, o_ref, *, mod_base, dff, final):
    o_ref[...] = _ffn_rows(x_ref[...], mod_ref[0], nw_ref, win_ref, wout_ref, fnw_ref,
                           mod_base=mod_base, dff=dff, final=final)


def _ffn(x, mod, nw, w_in, w_out, fnw, *, mod_base, rows_per_cond, final, tm=512):
    rows, d = x.shape
    dff = w_out.shape[0]
    tiles_per_cond = rows_per_cond // tm
    return pl.pallas_call(
        functools.partial(_ffn_kernel, mod_base=mod_base, dff=dff, final=final),
        grid=(rows // tm,),
        in_specs=[pl.BlockSpec((tm, d), lambda i: (i, 0)),
                  pl.BlockSpec((1, N_MOD, d), lambda i: (i // tiles_per_cond, 0, 0)),
                  _resident((1, d)),
                  _resident(w_in.shape),
                  _resident(w_out.shape),
                  _resident((1, d))],
        out_specs=pl.BlockSpec((tm, d), lambda i: (i, 0)),
        out_shape=jax.ShapeDtypeStruct((rows, d), F32),
        compiler_params=_cparams(1),
        name="ffn",
    )(x, mod, nw, w_in, w_out, fnw)


def _swap_halves(x):
    n = x.shape[-1]
    lane = lax.broadcasted_iota(jnp.int32, x.shape, 1)
    first = (lane % HEAD_DIM) < (HEAD_DIM // 2)
    return jnp.where(first, pltpu.roll(x, n - HEAD_DIM // 2, 1), pltpu.roll(x, HEAD_DIM // 2, 1))


def _rope(x, cos, sin):
    reps = x.shape[-1] // LANES
    c = jnp.concatenate([cos] * reps, axis=1) if reps > 1 else cos
    s = jnp.concatenate([sin] * reps, axis=1) if reps > 1 else sin
    return x * c + _swap_halves(x) * s


def _head_rms(x, ones_bd, gain):
    hi, lo = _split_bf16(x * x)
    ss = (jnp.dot(hi, ones_bd, preferred_element_type=F32)
          + jnp.dot(lo, ones_bd, preferred_element_type=F32))
    return x * lax.rsqrt(ss * (1.0 / HEAD_DIM) + EPS) * gain


def _inproj_kernel(*refs, rope):
    if rope:
        (x_ref, mod_ref, nw_ref, w_ref, ones_ref, gq_ref, gk_ref, cos_ref, sin_ref,
         attn_ref, ret_ref, dn_ref, ab_ref, mg_ref) = refs
    else:
        (x_ref, mod_ref, nw_ref, w_ref, ones_ref, gq_ref, gk_ref,
         attn_ref, ret_ref, dn_ref, ab_ref, mg_ref) = refs
    m = mod_ref[0]
    h = _norm_mod(x_ref[...], nw_ref[...], m[3:4], m[4:5]).astype(BF16)

    def proj(lo, width):
        return jnp.dot(h, w_ref[:, lo:lo + width], preferred_element_type=F32)

    def rot(v):
        return _rope(v, cos_ref[...], sin_ref[...]) if rope else v

    a = proj(0, ATTN_W)
    q = _head_rms(a[:, :BRANCH_W], ones_ref[...], gq_ref[...])
    k = _head_rms(a[:, BRANCH_W:BRANCH_W + KV_W], ones_ref[:LANES, :LANES], gk_ref[...])
    attn_ref[:, :BRANCH_W] = rot(q) * (HEAD_DIM ** -0.5)
    attn_ref[:, BRANCH_W:BRANCH_W + KV_W] = rot(k)
    attn_ref[:, BRANCH_W + KV_W:ATTN_W] = a[:, BRANCH_W + KV_W:ATTN_W]

    r = proj(ATTN_W, RET_W)
    ret_ref[:, :BRANCH_W] = rot(r[:, :BRANCH_W]) * (RET_DK ** -0.5)
    ret_ref[:, BRANCH_W:2 * BRANCH_W] = rot(r[:, BRANCH_W:2 * BRANCH_W])
    ret_ref[:, 1024:] = r[:, 1024:]

    dn_ref[...] = proj(ATTN_W + RET_W, DN_W)
    ab_ref[...] = proj(ATTN_W + RET_W + DN_W, AB_W)
    mg_ref[...] = proj(ATTN_W + RET_W + DN_W + AB_W, MG_W)


def _inproj(x, mod, nw, w, ones_bd, gq, gk, rope_tabs, *, rows_per_cond, seq_len, tm=256):
    rows, d = x.shape
    tiles_per_cond = rows_per_cond // tm
    tiles_per_seq = seq_len // tm
    rope = rope_tabs is not None
    in_specs = [pl.BlockSpec((tm, d), lambda i: (i, 0)),
                pl.BlockSpec((1, N_MOD, d), lambda i: (i // tiles_per_cond, 0, 0)),
                _resident((1, d)),
                _resident(w.shape),
                _resident(ones_bd.shape),
                _resident(gq.shape),
                _resident(gk.shape)]
    args = [x, mod, nw, w, ones_bd, gq, gk]
    if rope:
        in_specs += [pl.BlockSpec((tm, LANES), lambda i: (i % tiles_per_seq, 0))] * 2
        args += list(rope_tabs)
    widths = (ATTN_W, RET_W, DN_W, AB_W, MG_W)
    return pl.pallas_call(
        functools.partial(_inproj_kernel, rope=rope),
        grid=(rows // tm,),
        in_specs=in_specs,
        out_specs=[pl.BlockSpec((tm, wd), lambda i: (i, 0)) for wd in widths],
        out_shape=[jax.ShapeDtypeStruct((rows, wd), F32) for wd in widths],
        compiler_params=_cparams(1),
        name="inproj",
    )(*args)


ATTN_GROUP = ATTN_HEADS // ATTN_KV_HEADS
LOG2E = math.log2(math.e)
ATTN_KV_CHUNK = 512
ATTN_MAX_CHUNKS = 8
ATTN_EPILOGUE_DELAY = 2
ATTN_ROW_BLOCKS = 2


def _attn_kernel(*refs, tq, n_tiles, kv_chunk, n_chunks, has_ctx):
    if has_ctx:
        q_ref, k_ref, v_ref, ck_ref, cv_ref, o_ref, mx_ref, ls_ref, acc_ref, qs_ref = refs
    else:
        q_ref, k_ref, v_ref, o_ref, mx_ref, ls_ref, acc_ref, qs_ref = refs
    lane = lax.broadcasted_iota(jnp.int32, (tq, LANES), 1)
    low = lane < HEAD_DIM
    stacked = ATTN_HEADS * tq
    rbs = stacked // ATTN_ROW_BLOCKS

    def prologue(t):
        parts = []
        for hd in range(ATTN_HEADS):
            g = hd // ATTN_GROUP
            blk = q_ref[t * tq:(t + 1) * tq, (hd // 2) * LANES:(hd // 2 + 1) * LANES] * LOG2E
            if hd % 2 != g:
                blk = pltpu.roll(blk, HEAD_DIM, 1)
            parts.append(jnp.where(low if g == 0 else jnp.logical_not(low), blk, 0.0))
        rows = slice(t * stacked, (t + 1) * stacked)
        qs_ref[rows] = jnp.concatenate(parts, axis=0).astype(BF16)
        mx_ref[rows] = jnp.full((stacked, LANES), -jnp.inf, F32)
        ls_ref[rows] = jnp.zeros((stacked, LANES), F32)
        acc_ref[rows] = jnp.zeros((stacked, LANES), F32)

    def epilogue(t):
        rows = slice(t * stacked, (t + 1) * stacked)
        res = acc_ref[rows] / jnp.sum(ls_ref[rows], axis=-1, keepdims=True)
        for b in range(ATTN_HEADS // 2):
            g = (2 * b) // ATTN_GROUP
            even = res[2 * b * tq:(2 * b + 1) * tq]
            odd = res[(2 * b + 1) * tq:(2 * b + 2) * tq]
            if g == 0:
                odd = pltpu.roll(odd, HEAD_DIM, 1)
            else:
                even = pltpu.roll(even, HEAD_DIM, 1)
            o_ref[t * tq:(t + 1) * tq, b * LANES:(b + 1) * LANES] = jnp.where(low, even, odd)

    def lane_fold(op, acc, x):
        for b in range(x.shape[1] // LANES):
            acc = op(acc, x[:, b * LANES:(b + 1) * LANES])
        return acc

    def scores(sl, k):
        return _dot_nt(qs_ref[sl], k)

    def update(sl, s, v):
        reps = s.shape[1] // LANES
        m_old = mx_ref[sl]
        cmax = lane_fold(jnp.maximum, s[:, :LANES], s[:, LANES:])
        m_new = jnp.maximum(m_old, jnp.max(cmax, axis=-1, keepdims=True))
        alpha = jnp.exp2(m_old - m_new)
        p = jnp.exp2(s - jnp.concatenate([m_new] * reps, axis=1))
        mx_ref[sl] = m_new
        ls_ref[sl] = alpha * ls_ref[sl] + lane_fold(jnp.add, p[:, :LANES], p[:, LANES:])
        acc_ref[sl] = alpha * acc_ref[sl] + _dot(p, v)

    for t in range(n_tiles):
        prologue(t)
    kvs = [(k_ref[c * kv_chunk:(c + 1) * kv_chunk, :].astype(BF16),
            v_ref[c * kv_chunk:(c + 1) * kv_chunk, :].astype(BF16)) for c in range(n_chunks)]
    if has_ctx:
        kvs.append((ck_ref[0].astype(BF16), cv_ref[0].astype(BF16)))
    units = [(t, slice(t * stacked + r * rbs, t * stacked + (r + 1) * rbs), k, v)
             for t in range(n_tiles) for k, v in kvs for r in range(ATTN_ROW_BLOCKS)]
    last_of = {t: max(i for i, u in enumerate(units) if u[0] == t) for t in range(n_tiles)}
    ahead = None
    due = []
    for idx in range(len(units) + 1):
        nxt = None
        if idx < len(units):
            t, sl, k, v = units[idx]
            nxt = (idx, sl, scores(sl, k), v)
        if ahead is not None:
            done, sl, s, v = ahead
            update(sl, s, v)
            for t in range(n_tiles):
                if last_of[t] == done:
                    due.append((done + ATTN_EPILOGUE_DELAY, t))
        for pos, t in list(due):
            if pos <= idx or nxt is None:
                epilogue(t)
                due.remove((pos, t))
        ahead = nxt


def _attention(attn, ctx_k, ctx_v, *, batch, seq_len, tq, n_tiles):
    rows = attn.shape[0]
    kv_chunk = min(seq_len, ATTN_KV_CHUNK)
    n_chunks = seq_len // kv_chunk
    assert n_chunks <= ATTN_MAX_CHUNKS, "the key loop is fully unrolled"
    step = tq * n_tiles
    qt = seq_len // step
    stacked = ATTN_HEADS * step
    has_ctx = ctx_k is not None
    in_specs = [pl.BlockSpec((step, BRANCH_W), lambda b, i: (b * qt + i, 0)),
                pl.BlockSpec((seq_len, LANES), lambda b, i: (b, 4)),
                pl.BlockSpec((seq_len, LANES), lambda b, i: (b, 5))]
    args = [attn, attn, attn]
    if has_ctx:
        past = ctx_k.shape[1]
        in_specs += [pl.BlockSpec((1, past, LANES), lambda b, i: (b, 0, 0))] * 2
        args += [ctx_k, ctx_v]
    scratch = [pltpu.VMEM((stacked, LANES), F32)] * 3 + [pltpu.VMEM((stacked, LANES), BF16)]
    return pl.pallas_call(
        functools.partial(_attn_kernel, tq=tq, n_tiles=n_tiles, kv_chunk=kv_chunk, n_chunks=n_chunks,
                          has_ctx=has_ctx),
        grid=(batch, qt),
        in_specs=in_specs,
        out_specs=pl.BlockSpec((step, BRANCH_W), lambda b, i: (b * qt + i, 0)),
        out_shape=jax.ShapeDtypeStruct((rows, BRANCH_W), F32),
        scratch_shapes=scratch,
        compiler_params=_cparams(2),
        name="attention",
    )(*args)


def _retention_tables():
    C = RET_CHUNK
    h = np.arange(RET_HEADS, dtype=np.float64)
    pos = np.arange(C, dtype=np.float64)
    diff = pos[:, None] - pos[None, :]
    inner, qd, kd, cd = [], [], [], []
    for direction, expo in enumerate((RET_DECAY_EXP_FWD, RET_DECAY_EXP_BWD)):
        lg = np.log1p(-np.exp2(-expo - h))[:, None, None]
        if direction == 0:
            inner.append(np.where(diff >= 0, np.exp(lg * np.maximum(diff, 0.0)), 0.0))
            qd.append(np.exp(lg[:, :, 0] * (pos + 1.0)))
            kd.append(np.exp(lg[:, :, 0] * (C - 1.0 - pos)))
        else:
            inner.append(np.where(diff <= 0, np.exp(lg * np.maximum(-diff, 0.0)), 0.0))
            qd.append(np.exp(lg[:, :, 0] * (C - pos)))
            kd.append(np.exp(lg[:, :, 0] * pos))
        cd.append(np.exp(lg[:, 0, 0] * C))
    inner = np.stack(inner, axis=1)
    rowdec = np.stack([np.stack(qd, 1), np.stack(kd, 1)], axis=2)
    rowdec = np.repeat(rowdec[..., None], RET_DK, axis=-1)
    rowdec = np.concatenate([rowdec[0::2], rowdec[1::2]], axis=-1)
    cd = np.stack(cd, axis=1)
    block = np.kron(np.eye(2), np.ones((RET_DK, RET_DK)))
    per_row = np.repeat(np.stack([cd[0::2], cd[1::2]], axis=-1), RET_DK, axis=-1)
    cd = per_row[:, :, :, None] * block
    return (jnp.asarray(inner, F32), jnp.asarray(rowdec, F32), jnp.asarray(cd, F32))


RET_BLOCK = 16


def _ret_kernel(q_ref, k_ref, v_ref, g_ref, inner_ref, dec_ref, cd_ref, ones_ref, s0_ref, nw_ref,
                o_ref, st_ref, *, n_chunks, n_pairs):
    C = RET_CHUNK
    D = RET_DK
    nb = min(RET_BLOCK, n_chunks)
    lanes_of = [slice(p * LANES, (p + 1) * LANES) for p in range(n_pairs)]
    lane = lax.broadcasted_iota(jnp.int32, (C, LANES), 1)
    first = lane < D
    ones = ones_ref[...]

    def head_mean(x):
        hi, lo = _split_bf16(x)
        return (jnp.dot(hi, ones, preferred_element_type=F32)
                + jnp.dot(lo, ones, preferred_element_type=F32)) * (1.0 / D)

    def sweep(direction):
        def body(i, states):
            order = [i * nb + j for j in range(nb)]
            if direction == 1:
                order = [n_chunks - 1 - c for c in order]
            units = [(p, pl.multiple_of(c * C, C)) for p in range(n_pairs) for c in order]
            n = len(units)
            q = [q_ref[pl.ds(r, C), lanes_of[p]] for p, r in units]
            k = [k_ref[pl.ds(r, C), lanes_of[p]] for p, r in units]
            v = [v_ref[pl.ds(r, C), lanes_of[p]] for p, r in units]
            if direction == 1:
                prev = [o_ref[pl.ds(r, C), lanes_of[p]] for p, r in units]
                gate = [g_ref[pl.ds(r, C), lanes_of[p]] for p, r in units]
            att = [jnp.concatenate(
                [_dot_nt(jnp.where(first, q[e], 0.0), k[e]) * inner_ref[2 * units[e][0], direction],
                 _dot_nt(jnp.where(first, 0.0, q[e]), k[e]) * inner_ref[2 * units[e][0] + 1, direction]],
                axis=1) for e in range(n)]
            kv = [_dot_tn(k[e] * dec_ref[units[e][0], direction, 1], v[e]) for e in range(n)]
            seen = []
            states = list(states)
            for e, (p, _) in enumerate(units):
                cd = cd_ref[p, direction]
                seen.append(states[p])
                states[p] = states[p] * cd + jnp.where(cd != 0.0, kv[e], 0.0)
            v2 = [jnp.concatenate([jnp.where(first, v[e], 0.0), jnp.where(first, 0.0, v[e])], axis=0)
                  for e in range(n)]
            o = [_dot(att[e], v2[e]) + _dot(q[e] * dec_ref[units[e][0], direction, 0], seen[e])
                 for e in range(n)]
            if direction == 1:
                o = [o[e] + prev[e] for e in range(n)]
                mean = [head_mean(o[e]) for e in range(n)]
                d = [o[e] - mean[e] for e in range(n)]
                var = [head_mean(d[e] * d[e]) for e in range(n)]
                o = [d[e] * lax.rsqrt(var[e] + EPS) * nw_ref[:, lanes_of[units[e][0]]] * _silu(gate[e])
                     for e in range(n)]
            for e, (p, r) in enumerate(units):
                o_ref[pl.ds(r, C), lanes_of[p]] = o[e]
            return tuple(states)

        zero = jnp.zeros((D, D), F32)
        states = tuple(
            jnp.concatenate([jnp.concatenate([s0_ref[0, direction, 2 * p], zero], axis=1),
                             jnp.concatenate([zero, s0_ref[0, direction, 2 * p + 1]], axis=1)], axis=0)
            for p in range(n_pairs))
        states = lax.fori_loop(0, n_chunks // nb, body, states)
        for p in range(n_pairs):
            st_ref[0, direction, 2 * p] = states[p][:D, :D]
            st_ref[0, direction, 2 * p + 1] = states[p][D:, D:]

    sweep(0)
    sweep(1)


def _retention(ret, s0, nw, tables, ones_bd, *, batch, seq_len, n_pairs):
    rows = ret.shape[0]
    inner, rowdec, cd = tables
    C = RET_CHUNK
    U = n_pairs
    groups = RET_HEADS // 2 // U
    col = lambda j: pl.BlockSpec((seq_len, U * LANES), lambda b, hp, j=j: (b, groups * j + hp))
    st_spec = pl.BlockSpec((1, 2, 2 * U, RET_DK, RET_DK), lambda b, hp: (b, 0, hp, 0, 0))
    return pl.pallas_call(
        functools.partial(_ret_kernel, n_chunks=seq_len // C, n_pairs=U),
        grid=(batch, groups),
        in_specs=[col(0), col(1), col(2), col(3),
                  pl.BlockSpec((2 * U, 2, C, C), lambda b, hp: (hp, 0, 0, 0)),
                  pl.BlockSpec((U, 2, 2, C, LANES), lambda b, hp: (hp, 0, 0, 0, 0)),
                  pl.BlockSpec((U, 2, LANES, LANES), lambda b, hp: (hp, 0, 0, 0)),
                  pl.BlockSpec((LANES, LANES), lambda b, hp: (0, 0)),
                  st_spec,
                  pl.BlockSpec((1, U * LANES), lambda b, hp: (0, hp))],
        out_specs=[pl.BlockSpec((seq_len, U * LANES), lambda b, hp: (b, hp)), st_spec],
        out_shape=[jax.ShapeDtypeStruct((rows, BRANCH_W), F32),
                   jax.ShapeDtypeStruct((batch, 2, RET_HEADS, RET_DK, RET_DK), F32)],
        compiler_params=_cparams(2),
        name="retention",
    )(ret, ret, ret, ret, inner, rowdec, cd, ones_bd, s0, nw)


DN_BLOCK = 8
SOLVE_BASE = 8
CONV_BLOCK = 256
FIN_BLOCK = 1024


def _dn_kernel(q_ref, k_ref, v_ref, g_ref, ab_ref, cw_ref, alog_ref, dtb_ref, nw_ref, s0_ref,
               o_ref, st_ref, qs_ref, ks_ref, vs_ref, of_ref, ob_ref,
               wp_ref, bm_ref, qp_ref, op_ref, dec_ref, *, seq_len, n_heads):
    C = DN_CHUNK
    T = seq_len
    U = n_heads
    n_chunks = T // C
    head0 = pl.program_id(1) * U
    lanes_of = [slice(hh * LANES, (hh + 1) * LANES) for hh in range(U)]

    blk = min(CONV_BLOCK, T)
    row = lax.broadcasted_iota(jnp.int32, (blk, LANES), 0)
    tensors = [(src, dst, t, hh) for hh in range(U)
               for t, (src, dst) in enumerate(((q_ref, qs_ref), (k_ref, ks_ref), (v_ref, vs_ref)))]

    def conv(i, carry):
        r0 = pl.multiple_of(i * blk, blk)
        above = pl.multiple_of(jnp.maximum(r0 - SUBLANES, 0), SUBLANES)
        below = pl.multiple_of(jnp.minimum(r0 + blk, T - SUBLANES), SUBLANES)
        x = [src[pl.ds(r0, blk), lanes_of[hh]] for src, _, _, hh in tensors]
        up = [jnp.where(r0 > 0, src[pl.ds(above, SUBLANES), lanes_of[hh]], 0.0) for src, _, _, hh in tensors]
        dn = [jnp.where(r0 + blk < T, src[pl.ds(below, SUBLANES), lanes_of[hh]], 0.0)
              for src, _, _, hh in tensors]
        n = len(tensors)
        padded = [jnp.concatenate([up[e], x[e], dn[e]], axis=0) for e in range(n)]
        prev = [padded[e][SUBLANES - 1:SUBLANES - 1 + blk] for e in range(n)]
        nxt = [padded[e][SUBLANES + 1:SUBLANES + 1 + blk] for e in range(n)]
        w = [cw_ref[t, hh] for _, _, t, hh in tensors]
        y = [_silu(w[e][0:1] * prev[e] + w[e][1:2] * x[e] + w[e][2:3] * nxt[e]) for e in range(n)]
        for e, (_, dst, t, hh) in enumerate(tensors):
            if t < 2:
                scale = lax.rsqrt(jnp.sum(y[e] * y[e], axis=-1, keepdims=True) + EPS)
                y[e] = y[e] * (scale * (DN_DK ** -0.5) if t == 0 else scale)
        for e, (_, dst, t, hh) in enumerate(tensors):
            dst[pl.ds(r0, blk), lanes_of[hh]] = y[e]
        return carry

    lax.fori_loop(0, T // blk, conv, 0)

    ri = lax.broadcasted_iota(jnp.int32, (C, C), 0)
    ci = lax.broadcasted_iota(jnp.int32, (C, C), 1)
    eye = ri == ci
    eye_f = eye.astype(F32)
    lane = lax.broadcasted_iota(jnp.int32, (1, LANES), 1)
    masks = ((ri >= ci, ri > ci), (ri <= ci, ri < ci))
    incl_bf = tuple(m[0].astype(F32).astype(BF16) for m in masks)
    base_blocks = (ri // SOLVE_BASE) == (ci // SOLVE_BASE)
    level_masks = []
    size = SOLVE_BASE
    while size < C:
        level_masks.append(jnp.logical_and((ri // (2 * size)) == (ci // (2 * size)),
                                           (ri // size) != (ci // size)))
        size *= 2

    def load(r0, hh):
        return (ab_ref[pl.ds(r0, C), :],) + tuple(ref[pl.ds(r0, C), lanes_of[hh]]
                                                   for ref in (qs_ref, ks_ref, vs_ref))

    def prep(operands, direction, hh):
        ab, q, k, v = operands
        incl, strict = masks[direction]
        sel_a = (lane == direction * DN_HEADS + head0 + hh).astype(F32)
        sel_b = (lane == 2 * DN_HEADS + direction * DN_HEADS + head0 + hh).astype(F32)
        da = jnp.sum(ab * sel_a, axis=-1, keepdims=True)
        db = jnp.sum(ab * sel_b, axis=-1, keepdims=True)
        z = da + dtb_ref[direction, hh]
        softplus = jnp.maximum(z, 0.0) + jnp.log1p(jnp.exp(-jnp.abs(z)))
        g = -jnp.exp(alog_ref[direction, hh]) * softplus
        beta = _sigmoid(db)
        kb = k * beta
        g1 = g.astype(BF16)
        g2 = (g - g1.astype(F32)).astype(BF16)
        g3 = (g - g1.astype(F32) - g2.astype(F32)).astype(BF16)
        G12 = _dot(incl_bf[direction], jnp.concatenate([g1, g2], axis=1))
        G = G12[:, :LANES] + G12[:, LANES:] + _dot(incl_bf[direction], g3)
        kk = _dot_nt(kb, k)
        qk = _dot_nt(q, k)
        yield
        Gc = G[:, :C]
        Grow = jnp.sum(jnp.where(eye, Gc, 0.0), axis=0, keepdims=True)
        L = jnp.where(incl, jnp.exp(jnp.where(incl, Gc - Grow, 0.0)), 0.0)
        N = jnp.where(strict, -(kk * L), 0.0)
        P = jnp.where(base_blocks, N, 0.0)
        Tm = eye_f + P
        P = _dot(P, P)
        yield
        for _ in range(int(math.log2(SOLVE_BASE)) - 2):
            Tm, P = Tm + _dot(Tm, P), _dot(P, P)
            yield
        Tm = Tm + _dot(Tm, P)
        yield
        for off_blocks in level_masks:
            TX = _dot(Tm, jnp.where(off_blocks, N, 0.0))
            yield
            Tm = Tm + _dot(TX, Tm)
            yield
        eG = jnp.exp(G)
        g_last = G[C - 1:C] if direction == 0 else G[0:1]
        wu = _dot(Tm, jnp.concatenate([kb * eG, v * beta], axis=1))
        yield
        kd = k * jnp.exp(g_last - G)
        att = qk * L
        kd_wu = _dot_tn(kd, wu)
        att_wu = _dot(att, wu)
        return ((-kd_wu[:, :LANES]).astype(BF16), kd_wu[:, LANES:],
                (q * eG - att_wu[:, :LANES]).astype(BF16), att_wu[:, LANES:],
                jnp.broadcast_to(jnp.exp(g_last), (SUBLANES, LANES)))

    def run_staged(generators):
        results = [None] * len(generators)
        live = list(enumerate(generators))
        while live:
            still = []
            for idx, gen in live:
                try:
                    next(gen)
                    still.append((idx, gen))
                except StopIteration as done:
                    results[idx] = done.value
            live = still
        return results

    slots = (wp_ref, bm_ref, qp_ref, op_ref, dec_ref)
    nb = min(DN_BLOCK, n_chunks)
    n_blocks = n_chunks // nb
    per_half = 2 * nb * U

    def block_rows(i):
        return ([pl.multiple_of((i * nb + j) * C, C) for j in range(nb)]
                + [pl.multiple_of((n_chunks - 1 - (i * nb + j)) * C, C) for j in range(nb)])

    def prep_block(i):
        rows = block_rows(i)
        return [prep(load(r, hh), s // nb, hh) for hh in range(U) for s, r in enumerate(rows)]

    def store_block(prepared, base):
        for e in range(per_half):
            for ref, val in zip(slots, prepared[e]):
                ref[base + e] = val

    def recurrence(states, base):
        states = list(states)
        outs = []
        for j in range(nb):
            step_out = []
            for hh in range(U):
                for direction in range(2):
                    e = base + hh * 2 * nb + direction * nb + j
                    S = states[2 * hh + direction]
                    S16 = S.astype(BF16)
                    step_out.append(_dot(qp_ref[e], S16) + op_ref[e])
                    states[2 * hh + direction] = S * dec_ref[e, 0:1] + _dot(wp_ref[e], S16) + bm_ref[e]
            outs.append(step_out)
            yield
        return outs, tuple(states)

    def store_outputs(i, outs):
        rows = block_rows(i)
        for j in range(nb):
            for hh in range(U):
                of_ref[pl.ds(rows[j], C), lanes_of[hh]] = outs[j][2 * hh]
                ob_ref[pl.ds(rows[nb + j], C), lanes_of[hh]] = outs[j][2 * hh + 1]

    state = tuple(s0_ref[0, direction, hh] for hh in range(U) for direction in range(2))
    if n_blocks == 1:
        store_block(run_staged(prep_block(0)), 0)
        (outs, state), = run_staged([recurrence(state, 0)])
        store_outputs(0, outs)
    else:
        store_block(run_staged(prep_block(0)), 0)

        def body(i, carry):
            cur = (i % 2) * per_half
            nxt = per_half - cur
            *prepared, (outs, carry) = run_staged(prep_block(i + 1) + [recurrence(carry, cur)])
            store_outputs(i, outs)
            store_block(prepared, nxt)
            return carry

        state = lax.fori_loop(0, n_blocks - 1, body, state)
        (outs, state), = run_staged([recurrence(state, ((n_blocks - 1) % 2) * per_half)])
        store_outputs(n_blocks - 1, outs)
    for hh in range(U):
        for direction in range(2):
            st_ref[0, direction, hh] = state[2 * hh + direction]

    fblk = min(FIN_BLOCK, T)

    def fin(i, carry):
        r0 = pl.multiple_of(i * fblk, fblk)
        o = [of_ref[pl.ds(r0, fblk), lanes_of[hh]] + ob_ref[pl.ds(r0, fblk), lanes_of[hh]] for hh in range(U)]
        ms = [jnp.mean(o[hh] * o[hh], axis=-1, keepdims=True) for hh in range(U)]
        for hh in range(U):
            y = o[hh] * lax.rsqrt(ms[hh] + EPS) * nw_ref[...]
            o_ref[pl.ds(r0, fblk), lanes_of[hh]] = y * _silu(g_ref[pl.ds(r0, fblk), lanes_of[hh]])
        return carry

    lax.fori_loop(0, T // fblk, fin, 0)


def _deltanet(dn, ab, conv_w, alog, dtb, nw, s0, *, batch, seq_len, n_heads):
    rows = dn.shape[0]
    T = seq_len
    C = DN_CHUNK
    U = n_heads
    nb = min(DN_BLOCK, T // C)
    n_slots = 4 * nb * U
    groups = DN_HEADS // U
    col = lambda j: pl.BlockSpec((T, U * LANES), lambda b, h, j=j: (b, groups * j + h))
    st_spec = pl.BlockSpec((1, 2, U, DN_DK, DN_DK), lambda b, h: (b, 0, h, 0, 0))
    gate_spec = pl.BlockSpec((2, U, 1, LANES), lambda b, h: (0, h, 0, 0))
    return pl.pallas_call(
        functools.partial(_dn_kernel, seq_len=T, n_heads=U),
        grid=(batch, groups),
        in_specs=[col(0), col(1), col(2), col(3),
                  pl.BlockSpec((T, LANES), lambda b, h: (b, 0)),
                  pl.BlockSpec((3, U, 3, LANES), lambda b, h: (0, h, 0, 0)),
                  gate_spec, gate_spec,
                  pl.BlockSpec((1, LANES), lambda b, h: (0, 0)),
                  st_spec],
        out_specs=[pl.BlockSpec((T, U * LANES), lambda b, h: (b, h)), st_spec],
        out_shape=[jax.ShapeDtypeStruct((rows, BRANCH_W), F32),
                   jax.ShapeDtypeStruct((batch, 2, DN_HEADS, DN_DK, DN_DK), F32)],
        scratch_shapes=([pltpu.VMEM((T, U * LANES), F32)] * 5
                        + [pltpu.VMEM((n_slots, DN_DK, LANES), BF16), pltpu.VMEM((n_slots, DN_DK, LANES), F32),
                           pltpu.VMEM((n_slots, C, LANES), BF16), pltpu.VMEM((n_slots, C, LANES), F32),
                           pltpu.VMEM((n_slots, SUBLANES, LANES), F32)]),
        compiler_params=_cparams(2),
        name="deltanet",
    )(dn, dn, dn, dn, ab, conv_w, alog, dtb, nw, s0)


def _merge_ffn_kernel(x_ref, mod_ref, a_ref, r_ref, d_ref, mg_ref, wbr_ref, wo_ref,
                      nw_ref, win_ref, wout_ref, fnw_ref, o_ref, *, dff, final):
    d = x_ref.shape[-1]
    m = mod_ref[0]
    merged = jnp.zeros(x_ref.shape, F32)
    for i, br in enumerate((a_ref, r_ref, d_ref)):
        merged = merged + _sigmoid(mg_ref[:, i * d:(i + 1) * d]) * _dot(br[...], wbr_ref[i])
    x = x_ref[...] + m[5:6] * _dot(merged, wo_ref[...])
    o_ref[...] = _ffn_rows(x, m, nw_ref, win_ref, wout_ref, fnw_ref, mod_base=6, dff=dff, final=final)


def _merge_ffn(x, mod, a, r, dn, mg, w_br, w_o, nw, w_in, w_out, fnw, *, rows_per_cond, final, tm=512):
    rows, d = x.shape
    dff = w_out.shape[0]
    tiles_per_cond = rows_per_cond // tm
    row_spec = lambda wd: pl.BlockSpec((tm, wd), lambda i: (i, 0))
    return pl.pallas_call(
        functools.partial(_merge_ffn_kernel, dff=dff, final=final),
        grid=(rows // tm,),
        in_specs=[row_spec(d),
                  pl.BlockSpec((1, N_MOD, d), lambda i: (i // tiles_per_cond, 0, 0)),
                  row_spec(BRANCH_W), row_spec(BRANCH_W), row_spec(BRANCH_W), row_spec(MG_W),
                  _resident(w_br.shape), _resident(w_o.shape),
                  _resident((1, d)), _resident(w_in.shape), _resident(w_out.shape), _resident((1, d))],
        out_specs=row_spec(d),
        out_shape=jax.ShapeDtypeStruct((rows, d), F32),
        compiler_params=_cparams(1),
        name="merge_ffn",
    )(x, mod, a, r, dn, mg, w_br, w_o, nw, w_in, w_out, fnw)


def _rope_tables(seq_len):
    n_freq = HEAD_DIM // 4
    inv = ROPE_THETA ** (-np.arange(n_freq, dtype=np.float64) / n_freq)
    t = np.arange(seq_len)
    row = (t // GRID_W).astype(np.float64)
    colp = (t % GRID_W).astype(np.float64)
    ang = np.concatenate([row[:, None] * inv, colp[:, None] * inv], axis=-1)
    c, s = np.cos(ang), np.sin(ang)
    cos = np.concatenate([c, c, c, c], axis=-1)
    sin = np.concatenate([-s, s, -s, s], axis=-1)
    return jnp.asarray(cos, F32), jnp.asarray(sin, F32)


def _reorder_w_in(w):
    d = w.shape[0]
    o_da = ATTN_W + RET_W + 3 * BRANCH_W
    o_dg = o_da + 4 * DN_HEADS
    o_mg = o_dg + BRANCH_W
    parts = [w[:, :o_da], w[:, o_dg:o_mg], w[:, o_da:o_dg],
             jnp.zeros((d, AB_W - 4 * DN_HEADS), w.dtype), w[:, o_mg:]]
    return jnp.concatenate([p.astype(BF16) for p in parts], axis=1)


def kernel(x_prompt, x_sample, cache_k, cache_v, state_ret, state_delta, c, c_ctx,
           w_mod, b_mod, norm_ffn1, ffn1_w_in, ffn1_w_out, norm_mix, w_in,
           attn_q_norm, attn_k_norm, ret_norm, dn_conv, dn_a_log, dn_dt_bias, dn_norm,
           w_br_attn, w_br_ret, w_br_dn, w_out, norm_ffn2, ffn2_w_in, ffn2_w_out, norm_final):
    bp, tp, d = x_prompt.shape
    bs, ts, _ = x_sample.shape
    depth = w_mod.shape[0]
    past = cache_k.shape[2]

    conds = jnp.concatenate([c_ctx[None, :], c], axis=0)
    mod = _modulation(conds, w_mod, b_mod).reshape(depth, 1 + bs, N_MOD, d)

    ones_bd = jnp.asarray(np.kron(np.eye(ATTN_HEADS), np.ones((HEAD_DIM, HEAD_DIM))), BF16)
    ret_tabs = _retention_tables()
    rope_tabs = _rope_tables(ts)
    ret_zero = jnp.zeros((bp, 2, RET_HEADS, RET_DK, RET_DK), F32)
    dn_zero = jnp.zeros((bp, 2, DN_HEADS, DN_DK, DN_DK), F32)
    fnw = norm_final.reshape(1, d)

    groups = {
        "prompt": dict(x=x_prompt.reshape(bp * tp, d), batch=bp, seq=tp, rows_per_cond=bp * tp, rope=None,
                       dn_heads=DN_HEADS, ret_pairs=RET_HEADS // 2),
        "sample": dict(x=x_sample.reshape(bs * ts, d), batch=bs, seq=ts, rows_per_cond=ts, rope=rope_tabs,
                       dn_heads=1, ret_pairs=1),
    }
    new_k, new_v, new_rs, new_ds = [], [], [], []
    w_in_bf = w_in.astype(BF16)
    for l in range(depth):
        w1_in, w1_out = ffn1_w_in[l].astype(BF16), ffn1_w_out[l].astype(BF16)
        w2_in, w2_out = ffn2_w_in[l].astype(BF16), ffn2_w_out[l].astype(BF16)
        w_proj = _reorder_w_in(w_in_bf[l])
        w_br = jnp.stack([w_br_attn[l], w_br_ret[l], w_br_dn[l]]).astype(BF16)
        w_o = w_out[l].astype(BF16)
        gq = jnp.tile(attn_q_norm[l], ATTN_HEADS).reshape(1, BRANCH_W)
        gk = jnp.tile(attn_k_norm[l], ATTN_KV_HEADS).reshape(1, LANES)
        conv_w = dn_conv[l].reshape(3, 3, DN_HEADS, LANES).transpose(1, 2, 0, 3)
        alog = jnp.broadcast_to(dn_a_log[l][:, :, None, None], (2, DN_HEADS, 1, LANES))
        dtb = jnp.broadcast_to(dn_dt_bias[l][:, :, None, None], (2, DN_HEADS, 1, LANES))
        for name, grp in groups.items():
            is_prompt = name == "prompt"
            x = grp["x"]
            gmod = mod[l, :1] if is_prompt else mod[l, 1:]
            rpc = grp["rows_per_cond"]
            x = _ffn(x, gmod, norm_ffn1[l].reshape(1, d), w1_in, w1_out, fnw,
                     mod_base=0, rows_per_cond=rpc, final=False)
            attn, ret, dn, ab, mg = _inproj(x, gmod, norm_mix[l].reshape(1, d), w_proj, ones_bd, gq, gk,
                                            grp["rope"], rows_per_cond=rpc, seq_len=grp["seq"])
            if is_prompt:
                a_out = _attention(attn, None, None, batch=bp, seq_len=tp, tq=tp, n_tiles=1)
                rs0, ds0 = ret_zero, dn_zero
            else:
                a_out = _attention(attn, cache_k[:, l].reshape(bs, past, LANES),
                                   cache_v[:, l].reshape(bs, past, LANES), batch=bs, seq_len=ts,
                                   tq=128, n_tiles=2)
                rs0, ds0 = state_ret[:, l], state_delta[:, l]
            r_out, rs = _retention(ret, rs0, ret_norm[l].reshape(1, BRANCH_W), ret_tabs, ones_bd,
                                   batch=grp["batch"], seq_len=grp["seq"], n_pairs=grp["ret_pairs"])
            d_out, ds = _deltanet(dn, ab, conv_w, alog, dtb, dn_norm[l].reshape(1, LANES), ds0,
                                  batch=grp["batch"], seq_len=grp["seq"], n_heads=grp["dn_heads"])
            x = _merge_ffn(x, gmod, a_out, r_out, d_out, mg, w_br, w_o, norm_ffn2[l].reshape(1, d),
                           w2_in, w2_out, fnw, rows_per_cond=rpc, final=(l == depth - 1))
            grp["x"] = x
            if is_prompt:
                new_k.append(attn[:, BRANCH_W:BRANCH_W + KV_W].reshape(bp, tp, ATTN_KV_HEADS, HEAD_DIM))
                new_v.append(attn[:, BRANCH_W + KV_W:ATTN_W].reshape(bp, tp, ATTN_KV_HEADS, HEAD_DIM))
                new_rs.append(rs)
                new_ds.append(ds)

    y_prompt = groups["prompt"]["x"].reshape(bp, tp, d)
    y_sample = groups["sample"]["x"].reshape(bs, ts, d)
    return (y_prompt, y_sample, jnp.stack(new_k, axis=1), jnp.stack(new_v, axis=1),
            jnp.stack(new_rs, axis=1), jnp.stack(new_ds, axis=1))
```

```python
import functools
import math

import numpy as np
import jax
import jax.numpy as jnp
from jax import lax
from jax.experimental import pallas as pl
from jax.experimental.pallas import tpu as pltpu

F32 = jnp.float32
BF16 = jnp.bfloat16

EPS = 1e-6
ROPE_THETA = 10000.0
GRID_W = 64
N_MOD = 9

ATTN_HEADS = 8
ATTN_KV_HEADS = 2
HEAD_DIM = 64
RET_HEADS = 8
RET_DK = 64
RET_CHUNK = 128
RET_DECAY_EXP_FWD = 5.0
RET_DECAY_EXP_BWD = 5.5
DN_HEADS = 4
DN_DK = 128
DN_CHUNK = 64

LANES = 128
SUBLANES = 8
VMEM_LIMIT = 56 * 1024 * 1024

BRANCH_W = 512
KV_W = 128
ATTN_W = 768
RET_W = 2048
DN_W = 2048
AB_W = 128
MG_W = 3072


def _cparams(n_axes):
    return pltpu.CompilerParams(dimension_semantics=("parallel",) * n_axes,
                                vmem_limit_bytes=VMEM_LIMIT)


def _resident(shape):
    zeros = (0,) * len(shape)
    return pl.BlockSpec(shape, lambda *_: zeros, pipeline_mode=pl.Buffered(1))


def _dot(a, b):
    return jnp.dot(a.astype(BF16), b.astype(BF16), preferred_element_type=F32)


def _dot_nt(a, b):
    return lax.dot_general(a.astype(BF16), b.astype(BF16), (((1,), (1,)), ((), ())),
                           preferred_element_type=F32)


def _dot_tn(a, b):
    return lax.dot_general(a.astype(BF16), b.astype(BF16), (((0,), (0,)), ((), ())),
                           preferred_element_type=F32)


def _split_bf16(x):
    hi = x.astype(BF16)
    return hi, (x - hi.astype(F32)).astype(BF16)


def _sigmoid(x):
    return 1.0 / (1.0 + jnp.exp(-x))


def _silu(x):
    return x * _sigmoid(x)


def _norm_mod(x, nw, shift, scale):
    y = x * lax.rsqrt(jnp.mean(x * x, axis=-1, keepdims=True) + EPS) * nw
    return y * (1.0 + scale) + shift


def _mod_kernel(c_ref, w_ref, b_ref, o_ref):
    o_ref[0] = _dot(_silu(c_ref[...]), w_ref[0]) + b_ref[0]


def _modulation(conds, w_mod, b_mod):
    depth, d, n = w_mod.shape
    nc = conds.shape[0]
    tn = n // N_MOD
    return pl.pallas_call(
        _mod_kernel,
        grid=(depth, n // tn),
        in_specs=[pl.BlockSpec((nc, d), lambda l, j: (0, 0)),
                  pl.BlockSpec((1, d, tn), lambda l, j: (l, 0, j)),
                  pl.BlockSpec((1, 1, tn), lambda l, j: (l, 0, j))],
        out_specs=pl.BlockSpec((1, nc, tn), lambda l, j: (l, 0, j)),
        out_shape=jax.ShapeDtypeStruct((depth, nc, n), F32),
        compiler_params=_cparams(2),
        name="modulation",
    )(conds, w_mod, b_mod.reshape(depth, 1, n))


FFN_CHUNK = 256


def _ffn_rows(x, m, nw_ref, win_ref, wout_ref, fnw_ref, *, mod_base, dff, final):
    shift, scale, gate = (m[mod_base + i:mod_base + i + 1] for i in range(3))
    h = _norm_mod(x, nw_ref[...], shift, scale).astype(BF16)
    acc = jnp.zeros(x.shape, F32)
    for c in range(dff // FFN_CHUNK):
        lo = c * FFN_CHUNK
        hg = jnp.dot(h, win_ref[:, lo:lo + FFN_CHUNK], preferred_element_type=F32)
        hu = jnp.dot(h, win_ref[:, dff + lo:dff + lo + FFN_CHUNK], preferred_element_type=F32)
        a = (_silu(hg) * hu).astype(BF16)
        acc = acc + jnp.dot(a, wout_ref[lo:lo + FFN_CHUNK, :], preferred_element_type=F32)
    y = x + 0.5 * gate * acc
    if final:
        y = y * lax.rsqrt(jnp.mean(y * y, axis=-1, keepdims=True) + EPS) * fnw_ref[...]
    return y


def _ffn_kernel(x_ref, mod_ref, nw_ref, win_ref, wout_ref, fnw_ref, o_ref, *, mod_base, dff, final):
    o_ref[...] = _ffn_rows(x_ref[...], mod_ref[0], nw_ref, win_ref, wout_ref, fnw_ref,
                           mod_base=mod_base, dff=dff, final=final)


def _ffn(x, mod, nw, w_in, w_out, fnw, *, mod_base, rows_per_cond, final, tm=512):
    rows, d = x.shape
    dff = w_out.shape[0]
    tiles_per_cond = rows_per_cond // tm
    return pl.pallas_call(
        functools.partial(_ffn_kernel, mod_base=mod_base, dff=dff, final=final),
        grid=(rows // tm,),
        in_specs=[pl.BlockSpec((tm, d), lambda i: (i, 0)),
                  pl.BlockSpec((1, N_MOD, d), lambda i: (i // tiles_per_cond, 0, 0)),
                  _resident((1, d)),
                  _resident(w_in.shape),
                  _resident(w_out.shape),
                  _resident((1, d))],
        out_specs=pl.BlockSpec((tm, d), lambda i: (i, 0)),
        out_shape=jax.ShapeDtypeStruct((rows, d), F32),
        compiler_params=_cparams(1),
        name="ffn",
    )(x, mod, nw, w_in, w_out, fnw)


def _swap_halves(x):
    n = x.shape[-1]
    lane = lax.broadcasted_iota(jnp.int32, x.shape, 1)
    first = (lane % HEAD_DIM) < (HEAD_DIM // 2)
    return jnp.where(first, pltpu.roll(x, n - HEAD_DIM // 2, 1), pltpu.roll(x, HEAD_DIM // 2, 1))


def _rope(x, cos, sin):
    reps = x.shape[-1] // LANES
    c = jnp.concatenate([cos] * reps, axis=1) if reps > 1 else cos
    s = jnp.concatenate([sin] * reps, axis=1) if reps > 1 else sin
    return x * c + _swap_halves(x) * s


def _head_rms(x, ones_bd, gain):
    hi, lo = _split_bf16(x * x)
    ss = (jnp.dot(hi, ones_bd, preferred_element_type=F32)
          + jnp.dot(lo, ones_bd, preferred_element_type=F32))
    return x * lax.rsqrt(ss * (1.0 / HEAD_DIM) + EPS) * gain


def _inproj_kernel(*refs, rope):
    if rope:
        (x_ref, mod_ref, nw_ref, w_ref, ones_ref, gq_ref, gk_ref, cos_ref, sin_ref,
         attn_ref, ret_ref, dn_ref, ab_ref, mg_ref) = refs
    else:
        (x_ref, mod_ref, nw_ref, w_ref, ones_ref, gq_ref, gk_ref,
         attn_ref, ret_ref, dn_ref, ab_ref, mg_ref) = refs
    m = mod_ref[0]
    h = _norm_mod(x_ref[...], nw_ref[...], m[3:4], m[4:5]).astype(BF16)

    def proj(lo, width):
        return jnp.dot(h, w_ref[:, lo:lo + width], preferred_element_type=F32)

    def rot(v):
        return _rope(v, cos_ref[...], sin_ref[...]) if rope else v

    a = proj(0, ATTN_W)
    q = _head_rms(a[:, :BRANCH_W], ones_ref[...], gq_ref[...])
    k = _head_rms(a[:, BRANCH_W:BRANCH_W + KV_W], ones_ref[:LANES, :LANES], gk_ref[...])
    attn_ref[:, :BRANCH_W] = rot(q) * (HEAD_DIM ** -0.5)
    attn_ref[:, BRANCH_W:BRANCH_W + KV_W] = rot(k)
    attn_ref[:, BRANCH_W + KV_W:ATTN_W] = a[:, BRANCH_W + KV_W:ATTN_W]

    r = proj(ATTN_W, RET_W)
    ret_ref[:, :BRANCH_W] = rot(r[:, :BRANCH_W]) * (RET_DK ** -0.5)
    ret_ref[:, BRANCH_W:2 * BRANCH_W] = rot(r[:, BRANCH_W:2 * BRANCH_W])
    ret_ref[:, 1024:] = r[:, 1024:]

    dn_ref[...] = proj(ATTN_W + RET_W, DN_W)
    ab_ref[...] = proj(ATTN_W + RET_W + DN_W, AB_W)
    mg_ref[...] = proj(ATTN_W + RET_W + DN_W + AB_W, MG_W)


def _inproj(x, mod, nw, w, ones_bd, gq, gk, rope_tabs, *, rows_per_cond, seq_len, tm=256):
    rows, d = x.shape
    tiles_per_cond = rows_per_cond // tm
    tiles_per_seq = seq_len // tm
    rope = rope_tabs is not None
    in_specs = [pl.BlockSpec((tm, d), lambda i: (i, 0)),
                pl.BlockSpec((1, N_MOD, d), lambda i: (i // tiles_per_cond, 0, 0)),
                _resident((1, d)),
                _resident(w.shape),
                _resident(ones_bd.shape),
                _resident(gq.shape),
                _resident(gk.shape)]
    args = [x, mod, nw, w, ones_bd, gq, gk]
    if rope:
        in_specs += [pl.BlockSpec((tm, LANES), lambda i: (i % tiles_per_seq, 0))] * 2
        args += list(rope_tabs)
    widths = (ATTN_W, RET_W, DN_W, AB_W, MG_W)
    return pl.pallas_call(
        functools.partial(_inproj_kernel, rope=rope),
        grid=(rows // tm,),
        in_specs=in_specs,
        out_specs=[pl.BlockSpec((tm, wd), lambda i: (i, 0)) for wd in widths],
        out_shape=[jax.ShapeDtypeStruct((rows, wd), F32) for wd in widths],
        compiler_params=_cparams(1),
        name="inproj",
    )(*args)


ATTN_GROUP = ATTN_HEADS // ATTN_KV_HEADS
LOG2E = math.log2(math.e)
ATTN_KV_CHUNK = 512
ATTN_MAX_CHUNKS = 8
ATTN_EPILOGUE_DELAY = 2
ATTN_ROW_BLOCKS = 2


def _attn_kernel(*refs, tq, n_tiles, kv_chunk, n_chunks, has_ctx):
    if has_ctx:
        q_ref, k_ref, v_ref, ck_ref, cv_ref, o_ref, mx_ref, ls_ref, acc_ref, qs_ref = refs
    else:
        q_ref, k_ref, v_ref, o_ref, mx_ref, ls_ref, acc_ref, qs_ref = refs
    lane = lax.broadcasted_iota(jnp.int32, (tq, LANES), 1)
    low = lane < HEAD_DIM
    stacked = ATTN_HEADS * tq
    rbs = stacked // ATTN_ROW_BLOCKS

    def prologue(t):
        parts = []
        for hd in range(ATTN_HEADS):
            g = hd // ATTN_GROUP
            blk = q_ref[t * tq:(t + 1) * tq, (hd // 2) * LANES:(hd // 2 + 1) * LANES] * LOG2E
            if hd % 2 != g:
                blk = pltpu.roll(blk, HEAD_DIM, 1)
            parts.append(jnp.where(low if g == 0 else jnp.logical_not(low), blk, 0.0))
        rows = slice(t * stacked, (t + 1) * stacked)
        qs_ref[rows] = jnp.concatenate(parts, axis=0).astype(BF16)
        mx_ref[rows] = jnp.full((stacked, LANES), -jnp.inf, F32)
        ls_ref[rows] = jnp.zeros((stacked, LANES), F32)
        acc_ref[rows] = jnp.zeros((stacked, LANES), F32)

    def epilogue(t):
        rows = slice(t * stacked, (t + 1) * stacked)
        res = acc_ref[rows] / jnp.sum(ls_ref[rows], axis=-1, keepdims=True)
        for b in range(ATTN_HEADS // 2):
            g = (2 * b) // ATTN_GROUP
            even = res[2 * b * tq:(2 * b + 1) * tq]
            odd = res[(2 * b + 1) * tq:(2 * b + 2) * tq]
            if g == 0:
                odd = pltpu.roll(odd, HEAD_DIM, 1)
            else:
                even = pltpu.roll(even, HEAD_DIM, 1)
            o_ref[t * tq:(t + 1) * tq, b * LANES:(b + 1) * LANES] = jnp.where(low, even, odd)

    def lane_fold(op, acc, x):
        for b in range(x.shape[1] // LANES):
            acc = op(acc, x[:, b * LANES:(b + 1) * LANES])
        return acc

    def scores(sl, k):
        return _dot_nt(qs_ref[sl], k)

    def update(sl, s, v):
        reps = s.shape[1] // LANES
        m_old = mx_ref[sl]
        cmax = lane_fold(jnp.maximum, s[:, :LANES], s[:, LANES:])
        m_new = jnp.maximum(m_old, jnp.max(cmax, axis=-1, keepdims=True))
        alpha = jnp.exp2(m_old - m_new)
        p = jnp.exp2(s - jnp.concatenate([m_new] * reps, axis=1))
        mx_ref[sl] = m_new
        ls_ref[sl] = alpha * ls_ref[sl] + lane_fold(jnp.add, p[:, :LANES], p[:, LANES:])
        acc_ref[sl] = alpha * acc_ref[sl] + _dot(p, v)

    for t in range(n_tiles):
        prologue(t)
    kvs = [(k_ref[c * kv_chunk:(c + 1) * kv_chunk, :].astype(BF16),
            v_ref[c * kv_chunk:(c + 1) * kv_chunk, :].astype(BF16)) for c in range(n_chunks)]
    if has_ctx:
        kvs.append((ck_ref[0].astype(BF16), cv_ref[0].astype(BF16)))
    units = [(t, slice(t * stacked + r * rbs, t * stacked + (r + 1) * rbs), k, v)
             for t in range(n_tiles) for k, v in kvs for r in range(ATTN_ROW_BLOCKS)]
    last_of = {t: max(i for i, u in enumerate(units) if u[0] == t) for t in range(n_tiles)}
    ahead = None
    due = []
    for idx in range(len(units) + 1):
        nxt = None
        if idx < len(units):
            t, sl, k, v = units[idx]
            nxt = (idx, sl, scores(sl, k), v)
        if ahead is not None:
            done, sl, s, v = ahead
            update(sl, s, v)
            for t in range(n_tiles):
                if last_of[t] == done:
                    due.append((done + ATTN_EPILOGUE_DELAY, t))
        for pos, t in list(due):
            if pos <= idx or nxt is None:
                epilogue(t)
                due.remove((pos, t))
        ahead = nxt


def _attention(attn, ctx_k, ctx_v, *, batch, seq_len, tq, n_tiles):
    rows = attn.shape[0]
    kv_chunk = min(seq_len, ATTN_KV_CHUNK)
    n_chunks = seq_len // kv_chunk
    assert n_chunks <= ATTN_MAX_CHUNKS, "the key loop is fully unrolled"
    step = tq * n_tiles
    qt = seq_len // step
    stacked = ATTN_HEADS * step
    has_ctx = ctx_k is not None
    in_specs = [pl.BlockSpec((step, BRANCH_W), lambda b, i: (b * qt + i, 0)),
                pl.BlockSpec((seq_len, LANES), lambda b, i: (b, 4)),
                pl.BlockSpec((seq_len, LANES), lambda b, i: (b, 5))]
    args = [attn, attn, attn]
    if has_ctx:
        past = ctx_k.shape[1]
        in_specs += [pl.BlockSpec((1, past, LANES), lambda b, i: (b, 0, 0))] * 2
        args += [ctx_k, ctx_v]
    scratch = [pltpu.VMEM((stacked, LANES), F32)] * 3 + [pltpu.VMEM((stacked, LANES), BF16)]
    return pl.pallas_call(
        functools.partial(_attn_kernel, tq=tq, n_tiles=n_tiles, kv_chunk=kv_chunk, n_chunks=n_chunks,
                          has_ctx=has_ctx),
        grid=(batch, qt),
        in_specs=in_specs,
        out_specs=pl.BlockSpec((step, BRANCH_W), lambda b, i: (b * qt + i, 0)),
        out_shape=jax.ShapeDtypeStruct((rows, BRANCH_W), F32),
        scratch_shapes=scratch,
        compiler_params=_cparams(2),
        name="attention",
    )(*args)


def _retention_tables():
    C = RET_CHUNK
    h = np.arange(RET_HEADS, dtype=np.float64)
    pos = np.arange(C, dtype=np.float64)
    diff = pos[:, None] - pos[None, :]
    inner, qd, kd, cd = [], [], [], []
    for direction, expo in enumerate((RET_DECAY_EXP_FWD, RET_DECAY_EXP_BWD)):
        lg = np.log1p(-np.exp2(-expo - h))[:, None, None]
        if direction == 0:
            inner.append(np.where(diff >= 0, np.exp(lg * np.maximum(diff, 0.0)), 0.0))
            qd.append(np.exp(lg[:, :, 0] * (pos + 1.0)))
            kd.append(np.exp(lg[:, :, 0] * (C - 1.0 - pos)))
        else:
            inner.append(np.where(diff <= 0, np.exp(lg * np.maximum(-diff, 0.0)), 0.0))
            qd.append(np.exp(lg[:, :, 0] * (C - pos)))
            kd.append(np.exp(lg[:, :, 0] * pos))
        cd.append(np.exp(lg[:, 0, 0] * C))
    inner = np.stack(inner, axis=1)
    rowdec = np.stack([np.stack(qd, 1), np.stack(kd, 1)], axis=2)
    rowdec = np.repeat(rowdec[..., None], RET_DK, axis=-1)
    rowdec = np.concatenate([rowdec[0::2], rowdec[1::2]], axis=-1)
    cd = np.stack(cd, axis=1)
    block = np.kron(np.eye(2), np.ones((RET_DK, RET_DK)))
    per_row = np.repeat(np.stack([cd[0::2], cd[1::2]], axis=-1), RET_DK, axis=-1)
    cd = per_row[:, :, :, None] * block
    return (jnp.asarray(inner, F32), jnp.asarray(rowdec, F32), jnp.asarray(cd, F32))


RET_BLOCK = 16


def _ret_kernel(q_ref, k_ref, v_ref, g_ref, inner_ref, dec_ref, cd_ref, ones_ref, nw_ref, *rest,
                n_chunks, n_pairs, has_s0):
    s0_ref = rest[0] if has_s0 else None
    o_ref, st_ref = rest[-2:]
    C = RET_CHUNK
    D = RET_DK
    nb = min(RET_BLOCK, n_chunks)
    lanes_of = [slice(p * LANES, (p + 1) * LANES) for p in range(n_pairs)]
    lane = lax.broadcasted_iota(jnp.int32, (C, LANES), 1)
    first = lane < D
    ones = ones_ref[...]

    def head_mean(x):
        hi, lo = _split_bf16(x)
        return (jnp.dot(hi, ones, preferred_element_type=F32)
                + jnp.dot(lo, ones, preferred_element_type=F32)) * (1.0 / D)

    def sweep(direction):
        def body(i, states):
            order = [i * nb + j for j in range(nb)]
            if direction == 1:
                order = [n_chunks - 1 - c for c in order]
            units = [(p, pl.multiple_of(c * C, C)) for p in range(n_pairs) for c in order]
            n = len(units)
            q = [q_ref[pl.ds(r, C), lanes_of[p]] for p, r in units]
            k = [k_ref[pl.ds(r, C), lanes_of[p]] for p, r in units]
            v = [v_ref[pl.ds(r, C), lanes_of[p]] for p, r in units]
            if direction == 1:
                prev = [o_ref[pl.ds(r, C), lanes_of[p]] for p, r in units]
                gate = [g_ref[pl.ds(r, C), lanes_of[p]] for p, r in units]
            att = [jnp.concatenate(
                [_dot_nt(jnp.where(first, q[e], 0.0), k[e]) * inner_ref[2 * units[e][0], direction],
                 _dot_nt(jnp.where(first, 0.0, q[e]), k[e]) * inner_ref[2 * units[e][0] + 1, direction]],
                axis=1) for e in range(n)]
            kv = [_dot_tn(k[e] * dec_ref[units[e][0], direction, 1], v[e]) for e in range(n)]
            seen = []
            states = list(states)
            for e, (p, _) in enumerate(units):
                cd = cd_ref[p, direction]
                seen.append(states[p])
                states[p] = states[p] * cd + jnp.where(cd != 0.0, kv[e], 0.0)
            v2 = [jnp.concatenate([jnp.where(first, v[e], 0.0), jnp.where(first, 0.0, v[e])], axis=0)
                  for e in range(n)]
            o = [_dot(att[e], v2[e]) + _dot(q[e] * dec_ref[units[e][0], direction, 0], seen[e])
                 for e in range(n)]
            if direction == 1:
                o = [o[e] + prev[e] for e in range(n)]
                mean = [head_mean(o[e]) for e in range(n)]
                d = [o[e] - mean[e] for e in range(n)]
                var = [head_mean(d[e] * d[e]) for e in range(n)]
                o = [d[e] * lax.rsqrt(var[e] + EPS) * nw_ref[:, lanes_of[units[e][0]]] * _silu(gate[e])
                     for e in range(n)]
            for e, (p, r) in enumerate(units):
                o_ref[pl.ds(r, C), lanes_of[p]] = o[e]
            return tuple(states)

        zero = jnp.zeros((D, D), F32)
        if has_s0:
            states = tuple(
                jnp.concatenate([jnp.concatenate([s0_ref[0, direction, 2 * p], zero], axis=1),
                                 jnp.concatenate([zero, s0_ref[0, direction, 2 * p + 1]], axis=1)], axis=0)
                for p in range(n_pairs))
        else:
            states = tuple(jnp.zeros((2 * D, 2 * D), F32) for _ in range(n_pairs))
        states = lax.fori_loop(0, n_chunks // nb, body, states)
        for p in range(n_pairs):
            st_ref[0, direction, 2 * p] = states[p][:D, :D]
            st_ref[0, direction, 2 * p + 1] = states[p][D:, D:]

    sweep(0)
    sweep(1)


def _retention(ret, s0, nw, tables, ones_bd, *, batch, seq_len, n_pairs):
    rows = ret.shape[0]
    inner, rowdec, cd = tables
    C = RET_CHUNK
    U = n_pairs
    groups = RET_HEADS // 2 // U
    col = lambda j: pl.BlockSpec((seq_len, U * LANES), lambda b, hp, j=j: (b, groups * j + hp))
    st_spec = pl.BlockSpec((1, 2, 2 * U, RET_DK, RET_DK), lambda b, hp: (b, 0, hp, 0, 0))
    has_s0 = s0 is not None
    return pl.pallas_call(
        functools.partial(_ret_kernel, n_chunks=seq_len // C, n_pairs=U, has_s0=has_s0),
        grid=(batch, groups),
        in_specs=[col(0), col(1), col(2), col(3),
                  pl.BlockSpec((2 * U, 2, C, C), lambda b, hp: (hp, 0, 0, 0)),
                  pl.BlockSpec((U, 2, 2, C, LANES), lambda b, hp: (hp, 0, 0, 0, 0)),
                  pl.BlockSpec((U, 2, LANES, LANES), lambda b, hp: (hp, 0, 0, 0)),
                  pl.BlockSpec((LANES, LANES), lambda b, hp: (0, 0)),
                  pl.BlockSpec((1, U * LANES), lambda b, hp: (0, hp))] + ([st_spec] if has_s0 else []),
        out_specs=[pl.BlockSpec((seq_len, U * LANES), lambda b, hp: (b, hp)), st_spec],
        out_shape=[jax.ShapeDtypeStruct((rows, BRANCH_W), F32),
                   jax.ShapeDtypeStruct((batch, 2, RET_HEADS, RET_DK, RET_DK), F32)],
        compiler_params=_cparams(2),
        name="retention",
    )(ret, ret, ret, ret, inner, rowdec, cd, ones_bd, nw, *([s0] if has_s0 else []))


DN_BLOCK = 8
SOLVE_BASE = 8
CONV_BLOCK = 256
FIN_BLOCK = 1024


def _dn_kernel(q_ref, k_ref, v_ref, g_ref, ab_ref, cw_ref, alog_ref, dtb_ref, nw_ref, *rest,
               seq_len, n_heads, has_s0):
    s0_ref = rest[0] if has_s0 else None
    (o_ref, st_ref, qs_ref, ks_ref, vs_ref, of_ref, ob_ref,
     wp_ref, bm_ref, qp_ref, op_ref, dec_ref) = rest[-12:]
    C = DN_CHUNK
    T = seq_len
    U = n_heads
    n_chunks = T // C
    head0 = pl.program_id(1) * U
    lanes_of = [slice(hh * LANES, (hh + 1) * LANES) for hh in range(U)]

    blk = min(CONV_BLOCK, T)
    row = lax.broadcasted_iota(jnp.int32, (blk, LANES), 0)
    tensors = [(src, dst, t, hh) for hh in range(U)
               for t, (src, dst) in enumerate(((q_ref, qs_ref), (k_ref, ks_ref), (v_ref, vs_ref)))]

    def conv(i, carry):
        r0 = pl.multiple_of(i * blk, blk)
        above = pl.multiple_of(jnp.maximum(r0 - SUBLANES, 0), SUBLANES)
        below = pl.multiple_of(jnp.minimum(r0 + blk, T - SUBLANES), SUBLANES)
        x = [src[pl.ds(r0, blk), lanes_of[hh]] for src, _, _, hh in tensors]
        up = [jnp.where(r0 > 0, src[pl.ds(above, SUBLANES), lanes_of[hh]], 0.0) for src, _, _, hh in tensors]
        dn = [jnp.where(r0 + blk < T, src[pl.ds(below, SUBLANES), lanes_of[hh]], 0.0)
              for src, _, _, hh in tensors]
        n = len(tensors)
        padded = [jnp.concatenate([up[e], x[e], dn[e]], axis=0) for e in range(n)]
        prev = [padded[e][SUBLANES - 1:SUBLANES - 1 + blk] for e in range(n)]
        nxt = [padded[e][SUBLANES + 1:SUBLANES + 1 + blk] for e in range(n)]
        w = [cw_ref[t, hh] for _, _, t, hh in tensors]
        y = [_silu(w[e][0:1] * prev[e] + w[e][1:2] * x[e] + w[e][2:3] * nxt[e]) for e in range(n)]
        for e, (_, dst, t, hh) in enumerate(tensors):
            if t < 2:
                scale = lax.rsqrt(jnp.sum(y[e] * y[e], axis=-1, keepdims=True) + EPS)
                y[e] = y[e] * (scale * (DN_DK ** -0.5) if t == 0 else scale)
        for e, (_, dst, t, hh) in enumerate(tensors):
            dst[pl.ds(r0, blk), lanes_of[hh]] = y[e]
        return carry

    lax.fori_loop(0, T // blk, conv, 0)

    ri = lax.broadcasted_iota(jnp.int32, (C, C), 0)
    ci = lax.broadcasted_iota(jnp.int32, (C, C), 1)
    eye = ri == ci
    eye_f = eye.astype(F32)
    lane = lax.broadcasted_iota(jnp.int32, (1, LANES), 1)
    masks = ((ri >= ci, ri > ci), (ri <= ci, ri < ci))
    incl_bf = tuple(m[0].astype(F32).astype(BF16) for m in masks)
    base_blocks = (ri // SOLVE_BASE) == (ci // SOLVE_BASE)
    level_masks = []
    size = SOLVE_BASE
    while size < C:
        level_masks.append(jnp.logical_and((ri // (2 * size)) == (ci // (2 * size)),
                                           (ri // size) != (ci // size)))
        size *= 2

    def load(r0, hh):
        return (ab_ref[pl.ds(r0, C), :],) + tuple(ref[pl.ds(r0, C), lanes_of[hh]]
                                                   for ref in (qs_ref, ks_ref, vs_ref))

    def prep(operands, direction, hh):
        ab, q, k, v = operands
        incl, strict = masks[direction]
        sel_a = (lane == direction * DN_HEADS + head0 + hh).astype(F32)
        sel_b = (lane == 2 * DN_HEADS + direction * DN_HEADS + head0 + hh).astype(F32)
        da = jnp.sum(ab * sel_a, axis=-1, keepdims=True)
        db = jnp.sum(ab * sel_b, axis=-1, keepdims=True)
        z = da + dtb_ref[direction, hh]
        softplus = jnp.maximum(z, 0.0) + jnp.log1p(jnp.exp(-jnp.abs(z)))
        g = -jnp.exp(alog_ref[direction, hh]) * softplus
        beta = _sigmoid(db)
        kb = k * beta
        g1 = g.astype(BF16)
        g2 = (g - g1.astype(F32)).astype(BF16)
        g3 = (g - g1.astype(F32) - g2.astype(F32)).astype(BF16)
        G12 = _dot(incl_bf[direction], jnp.concatenate([g1, g2], axis=1))
        G = G12[:, :LANES] + G12[:, LANES:] + _dot(incl_bf[direction], g3)
        kk = _dot_nt(kb, k)
        qk = _dot_nt(q, k)
        yield
        Gc = G[:, :C]
        Grow = jnp.sum(jnp.where(eye, Gc, 0.0), axis=0, keepdims=True)
        L = jnp.where(incl, jnp.exp(jnp.where(incl, Gc - Grow, 0.0)), 0.0)
        N = jnp.where(strict, -(kk * L), 0.0)
        P = jnp.where(base_blocks, N, 0.0)
        Tm = eye_f + P
        P = _dot(P, P)
        yield
        for _ in range(int(math.log2(SOLVE_BASE)) - 2):
            Tm, P = Tm + _dot(Tm, P), _dot(P, P)
            yield
        Tm = Tm + _dot(Tm, P)
        yield
        for off_blocks in level_masks:
            TX = _dot(Tm, jnp.where(off_blocks, N, 0.0))
            yield
            Tm = Tm + _dot(TX, Tm)
            yield
        eG = jnp.exp(G)
        g_last = G[C - 1:C] if direction == 0 else G[0:1]
        wu = _dot(Tm, jnp.concatenate([kb * eG, v * beta], axis=1))
        yield
        kd = k * jnp.exp(g_last - G)
        att = qk * L
        kd_wu = _dot_tn(kd, wu)
        att_wu = _dot(att, wu)
        return ((-kd_wu[:, :LANES]).astype(BF16), kd_wu[:, LANES:],
                (q * eG - att_wu[:, :LANES]).astype(BF16), att_wu[:, LANES:],
                jnp.broadcast_to(jnp.exp(g_last), (SUBLANES, LANES)))

    def run_staged(generators):
        results = [None] * len(generators)
        live = list(enumerate(generators))
        while live:
            still = []
            for idx, gen in live:
                try:
                    next(gen)
                    still.append((idx, gen))
                except StopIteration as done:
                    results[idx] = done.value
            live = still
        return results

    slots = (wp_ref, bm_ref, qp_ref, op_ref, dec_ref)
    nb = min(DN_BLOCK, n_chunks)
    n_blocks = n_chunks // nb
    per_half = 2 * nb * U

    def block_rows(i):
        return ([pl.multiple_of((i * nb + j) * C, C) for j in range(nb)]
                + [pl.multiple_of((n_chunks - 1 - (i * nb + j)) * C, C) for j in range(nb)])

    def prep_block(i):
        rows = block_rows(i)
        return [prep(load(r, hh), s // nb, hh) for hh in range(U) for s, r in enumerate(rows)]

    def store_block(prepared, base):
        for e in range(per_half):
            for ref, val in zip(slots, prepared[e]):
                ref[base + e] = val

    def recurrence(states, base):
        states = list(states)
        outs = []
        for j in range(nb):
            step_out = []
            for hh in range(U):
                for direction in range(2):
                    e = base + hh * 2 * nb + direction * nb + j
                    S = states[2 * hh + direction]
                    S16 = S.astype(BF16)
                    step_out.append(_dot(qp_ref[e], S16) + op_ref[e])
                    states[2 * hh + direction] = S * dec_ref[e, 0:1] + _dot(wp_ref[e], S16) + bm_ref[e]
            outs.append(step_out)
            yield
        return outs, tuple(states)

    def store_outputs(i, outs):
        rows = block_rows(i)
        for j in range(nb):
            for hh in range(U):
                of_ref[pl.ds(rows[j], C), lanes_of[hh]] = outs[j][2 * hh]
                ob_ref[pl.ds(rows[nb + j], C), lanes_of[hh]] = outs[j][2 * hh + 1]

    if has_s0:
        state = tuple(s0_ref[0, direction, hh] for hh in range(U) for direction in range(2))
    else:
        state = tuple(jnp.zeros((DN_DK, LANES), F32) for _ in range(2 * U))
    if n_blocks == 1:
        store_block(run_staged(prep_block(0)), 0)
        (outs, state), = run_staged([recurrence(state, 0)])
        store_outputs(0, outs)
    else:
        store_block(run_staged(prep_block(0)), 0)

        def body(i, carry):
            cur = (i % 2) * per_half
            nxt = per_half - cur
            *prepared, (outs, carry) = run_staged(prep_block(i + 1) + [recurrence(carry, cur)])
            store_outputs(i, outs)
            store_block(prepared, nxt)
            return carry

        state = lax.fori_loop(0, n_blocks - 1, body, state)
        (outs, state), = run_staged([recurrence(state, ((n_blocks - 1) % 2) * per_half)])
        store_outputs(n_blocks - 1, outs)
    for hh in range(U):
        for direction in range(2):
            st_ref[0, direction, hh] = state[2 * hh + direction]

    fblk = min(FIN_BLOCK, T)

    def fin(i, carry):
        r0 = pl.multiple_of(i * fblk, fblk)
        o = [of_ref[pl.ds(r0, fblk), lanes_of[hh]] + ob_ref[pl.ds(r0, fblk), lanes_of[hh]] for hh in range(U)]
        ms = [jnp.mean(o[hh] * o[hh], axis=-1, keepdims=True) for hh in range(U)]
        for hh in range(U):
            y = o[hh] * lax.rsqrt(ms[hh] + EPS) * nw_ref[...]
            o_ref[pl.ds(r0, fblk), lanes_of[hh]] = y * _silu(g_ref[pl.ds(r0, fblk), lanes_of[hh]])
        return carry

    lax.fori_loop(0, T // fblk, fin, 0)


def _deltanet(dn, ab, conv_w, alog, dtb, nw, s0, *, batch, seq_len, n_heads):
    rows = dn.shape[0]
    T = seq_len
    C = DN_CHUNK
    U = n_heads
    nb = min(DN_BLOCK, T // C)
    n_slots = 4 * nb * U
    groups = DN_HEADS // U
    col = lambda j: pl.BlockSpec((T, U * LANES), lambda b, h, j=j: (b, groups * j + h))
    st_spec = pl.BlockSpec((1, 2, U, DN_DK, DN_DK), lambda b, h: (b, 0, h, 0, 0))
    gate_spec = pl.BlockSpec((2, U, 1, LANES), lambda b, h: (0, h, 0, 0))
    has_s0 = s0 is not None
    return pl.pallas_call(
        functools.partial(_dn_kernel, seq_len=T, n_heads=U, has_s0=has_s0),
        grid=(batch, groups),
        in_specs=[col(0), col(1), col(2), col(3),
                  pl.BlockSpec((T, LANES), lambda b, h: (b, 0)),
                  pl.BlockSpec((3, U, 3, LANES), lambda b, h: (0, h, 0, 0)),
                  gate_spec, gate_spec,
                  pl.BlockSpec((1, LANES), lambda b, h: (0, 0))] + ([st_spec] if has_s0 else []),
        out_specs=[pl.BlockSpec((T, U * LANES), lambda b, h: (b, h)), st_spec],
        out_shape=[jax.ShapeDtypeStruct((rows, BRANCH_W), F32),
                   jax.ShapeDtypeStruct((batch, 2, DN_HEADS, DN_DK, DN_DK), F32)],
        scratch_shapes=([pltpu.VMEM((T, U * LANES), F32)] * 5
                        + [pltpu.VMEM((n_slots, DN_DK, LANES), BF16), pltpu.VMEM((n_slots, DN_DK, LANES), F32),
                           pltpu.VMEM((n_slots, C, LANES), BF16), pltpu.VMEM((n_slots, C, LANES), F32),
                           pltpu.VMEM((n_slots, SUBLANES, LANES), F32)]),
        compiler_params=_cparams(2),
        name="deltanet",
    )(dn, dn, dn, dn, ab, conv_w, alog, dtb, nw, *([s0] if has_s0 else []))


def _merge_ffn_kernel(x_ref, mod_ref, a_ref, r_ref, d_ref, mg_ref, wbr_ref, wo_ref,
                      nw_ref, win_ref, wout_ref, fnw_ref, o_ref, *, dff, final):
    d = x_ref.shape[-1]
    m = mod_ref[0]
    merged = jnp.zeros(x_ref.shape, F32)
    for i, br in enumerate((a_ref, r_ref, d_ref)):
        merged = merged + _sigmoid(mg_ref[:, i * d:(i + 1) * d]) * _dot(br[...], wbr_ref[i])
    x = x_ref[...] + m[5:6] * _dot(merged, wo_ref[...])
    o_ref[...] = _ffn_rows(x, m, nw_ref, win_ref, wout_ref, fnw_ref, mod_base=6, dff=dff, final=final)


def _merge_ffn(x, mod, a, r, dn, mg, w_br, w_o, nw, w_in, w_out, fnw, *, rows_per_cond, final, tm=512):
    rows, d = x.shape
    dff = w_out.shape[0]
    tiles_per_cond = rows_per_cond // tm
    row_spec = lambda wd: pl.BlockSpec((tm, wd), lambda i: (i, 0))
    return pl.pallas_call(
        functools.partial(_merge_ffn_kernel, dff=dff, final=final),
        grid=(rows // tm,),
        in_specs=[row_spec(d),
                  pl.BlockSpec((1, N_MOD, d), lambda i: (i // tiles_per_cond, 0, 0)),
                  row_spec(BRANCH_W), row_spec(BRANCH_W), row_spec(BRANCH_W), row_spec(MG_W),
                  _resident(w_br.shape), _resident(w_o.shape),
                  _resident((1, d)), _resident(w_in.shape), _resident(w_out.shape), _resident((1, d))],
        out_specs=row_spec(d),
        out_shape=jax.ShapeDtypeStruct((rows, d), F32),
        compiler_params=_cparams(1),
        name="merge_ffn",
    )(x, mod, a, r, dn, mg, w_br, w_o, nw, w_in, w_out, fnw)


def _rope_tables(seq_len):
    n_freq = HEAD_DIM // 4
    inv = ROPE_THETA ** (-np.arange(n_freq, dtype=np.float64) / n_freq)
    t = np.arange(seq_len)
    row = (t // GRID_W).astype(np.float64)
    colp = (t % GRID_W).astype(np.float64)
    ang = np.concatenate([row[:, None] * inv, colp[:, None] * inv], axis=-1)
    c, s = np.cos(ang), np.sin(ang)
    cos = np.concatenate([c, c, c, c], axis=-1)
    sin = np.concatenate([-s, s, -s, s], axis=-1)
    return jnp.asarray(cos, F32), jnp.asarray(sin, F32)


def _reorder_w_in(w):
    d = w.shape[0]
    o_da = ATTN_W + RET_W + 3 * BRANCH_W
    o_dg = o_da + 4 * DN_HEADS
    o_mg = o_dg + BRANCH_W
    parts = [w[:, :o_da], w[:, o_dg:o_mg], w[:, o_da:o_dg],
             jnp.zeros((d, AB_W - 4 * DN_HEADS), w.dtype), w[:, o_mg:]]
    return jnp.concatenate([p.astype(BF16) for p in parts], axis=1)


def kernel(x_prompt, x_sample, cache_k, cache_v, state_ret, state_delta, c, c_ctx,
           w_mod, b_mod, norm_ffn1, ffn1_w_in, ffn1_w_out, norm_mix, w_in,
           attn_q_norm, attn_k_norm, ret_norm, dn_conv, dn_a_log, dn_dt_bias, dn_norm,
           w_br_attn, w_br_ret, w_br_dn, w_out, norm_ffn2, ffn2_w_in, ffn2_w_out, norm_final):
    bp, tp, d = x_prompt.shape
    bs, ts, _ = x_sample.shape
    depth = w_mod.shape[0]
    past = cache_k.shape[2]

    conds = jnp.concatenate([c_ctx[None, :], c], axis=0)
    mod = _modulation(conds, w_mod, b_mod).reshape(depth, 1 + bs, N_MOD, d)

    ones_bd = jnp.asarray(np.kron(np.eye(ATTN_HEADS), np.ones((HEAD_DIM, HEAD_DIM))), BF16)
    ret_tabs = _retention_tables()
    rope_tabs = _rope_tables(ts)
    fnw = norm_final.reshape(1, d)

    groups = {
        "prompt": dict(x=x_prompt.reshape(bp * tp, d), batch=bp, seq=tp, rows_per_cond=bp * tp, rope=None,
                       dn_heads=DN_HEADS, ret_pairs=RET_HEADS // 2),
        "sample": dict(x=x_sample.reshape(bs * ts, d), batch=bs, seq=ts, rows_per_cond=ts, rope=rope_tabs,
                       dn_heads=1, ret_pairs=1),
    }
    new_k, new_v, new_rs, new_ds = [], [], [], []
    for l in range(depth):
        w1_in, w1_out = ffn1_w_in[l].astype(BF16), ffn1_w_out[l].astype(BF16)
        w2_in, w2_out = ffn2_w_in[l].astype(BF16), ffn2_w_out[l].astype(BF16)
        w_proj = _reorder_w_in(w_in[l])
        w_br = jnp.stack([w_br_attn[l], w_br_ret[l], w_br_dn[l]]).astype(BF16)
        w_o = w_out[l].astype(BF16)
        gq = jnp.tile(attn_q_norm[l], ATTN_HEADS).reshape(1, BRANCH_W)
        gk = jnp.tile(attn_k_norm[l], ATTN_KV_HEADS).reshape(1, LANES)
        conv_w = dn_conv[l].reshape(3, 3, DN_HEADS, LANES).transpose(1, 2, 0, 3)
        alog = jnp.broadcast_to(dn_a_log[l][:, :, None, None], (2, DN_HEADS, 1, LANES))
        dtb = jnp.broadcast_to(dn_dt_bias[l][:, :, None, None], (2, DN_HEADS, 1, LANES))
        for name, grp in groups.items():
            is_prompt = name == "prompt"
            x = grp["x"]
            gmod = mod[l, :1] if is_prompt else mod[l, 1:]
            rpc = grp["rows_per_cond"]
            x = _ffn(x, gmod, norm_ffn1[l].reshape(1, d), w1_in, w1_out, fnw,
                     mod_base=0, rows_per_cond=rpc, final=False)
            attn, ret, dn, ab, mg = _inproj(x, gmod, norm_mix[l].reshape(1, d), w_proj, ones_bd, gq, gk,
                                            grp["rope"], rows_per_cond=rpc, seq_len=grp["seq"])
            if is_prompt:
                a_out = _attention(attn, None, None, batch=bp, seq_len=tp, tq=tp, n_tiles=1)
                rs0, ds0 = None, None
            else:
                a_out = _attention(attn, cache_k[:, l].reshape(bs, past, LANES),
                                   cache_v[:, l].reshape(bs, past, LANES), batch=bs, seq_len=ts,
                                   tq=128, n_tiles=2)
                rs0, ds0 = state_ret[:, l], state_delta[:, l]
            r_out, rs = _retention(ret, rs0, ret_norm[l].reshape(1, BRANCH_W), ret_tabs, ones_bd,
                                   batch=grp["batch"], seq_len=grp["seq"], n_pairs=grp["ret_pairs"])
            d_out, ds = _deltanet(dn, ab, conv_w, alog, dtb, dn_norm[l].reshape(1, LANES), ds0,
                                  batch=grp["batch"], seq_len=grp["seq"], n_heads=grp["dn_heads"])
            x = _merge_ffn(x, gmod, a_out, r_out, d_out, mg, w_br, w_o, norm_ffn2[l].reshape(1, d),
                           w2_in, w2_out, fnw, rows_per_cond=rpc, final=(l == depth - 1))
            grp["x"] = x
            if is_prompt:
                new_k.append(attn[:, BRANCH_W:BRANCH_W + KV_W].reshape(bp, tp, ATTN_KV_HEADS, HEAD_DIM))
                new_v.append(attn[:, BRANCH_W + KV_W:ATTN_W].reshape(bp, tp, ATTN_KV_HEADS, HEAD_DIM))
                new_rs.append(rs)
                new_ds.append(ds)

    y_prompt = groups["prompt"]["x"].reshape(bp, tp, d)
    y_sample = groups["sample"]["x"].reshape(bs, ts, d)
    return (y_prompt, y_sample, jnp.stack(new_k, axis=1), jnp.stack(new_v, axis=1),
            jnp.stack(new_rs, axis=1), jnp.stack(new_ds, axis=1))
```
